```python
import jax, jax.numpy as jnp
from jax import lax
import numpy as np

D_MODEL = 1024
BATCH = 2
SEQ = 16384
DEPTH = 1
DEC_BATCH = 32
DEC_SEQ = 64
PAST_LEN = 1024

CHUNK = 64
D_MIX = D_MODEL
NH_M = 4
DH_M = D_MIX // 2 // NH_M
D_M = NH_M * DH_M
NH_A = 8
DH_A = (D_MIX - D_M) // NH_A
D_A = NH_A * DH_A
CONV_W = 4
LEFT_CHUNKS = 8
ATT_WINDOW = LEFT_CHUNKS * CHUNK
MAX_REL = 2 * CHUNK
N_REL = (CHUNK - 1) + MAX_REL + 1
D_FF = 2816
N_MOD = 9
EPS = 1e-6
OFF_MQ = 0
OFF_MK = D_M
OFF_MV = 2 * D_M
OFF_MO = 3 * D_M
OFF_MI = 4 * D_M
OFF_MF = 4 * D_M + NH_M
OFF_AQ = 4 * D_M + 2 * NH_M
OFF_AK = OFF_AQ + D_A
OFF_AV = OFF_AK + D_A
D_IN = OFF_AV + D_A

kernel_name = 'hybrid_mlstm_chunkband_stream_step'


def rmsnorm(x, g):
    xf = x.astype(jnp.float32)
    y = xf * lax.rsqrt(jnp.mean(xf * xf, axis=-1, keepdims=True) + EPS)
    return (y * g.astype(jnp.float32)).astype(x.dtype)


def adaln_params(c, w_ada, b_ada):
    mod = jnp.einsum('bd,de->be', jax.nn.silu(c), w_ada) + b_ada
    return jnp.split(mod[:, None, :], N_MOD, axis=-1)


def modulate(h, shift, scale):
    return h * (1 + scale) + shift


def swiglu(h, w_up, w_down):
    g, u = jnp.split(h @ w_up, 2, axis=-1)
    return (jax.nn.silu(g) * u) @ w_down


def half_ffn(x, shift, scale, gate, g, w_up, w_down):
    h = modulate(rmsnorm(x, g), shift, scale)
    return x + 0.5 * gate * swiglu(h, w_up, w_down)


def causal_conv(u, prev, w, b):
    L = u.shape[1]
    up = jnp.concatenate([prev.astype(u.dtype), u], axis=1)
    out = b + w[0] * up[:, 0:L]
    for j in range(1, CONV_W):
        out = out + w[j] * up[:, j:j + L]
    return jax.nn.silu(out), up[:, up.shape[1] - (CONV_W - 1):]


def project_in(h, w_in, conv_prev, conv_w, conv_b, b_gates):
    B, L, _ = h.shape
    proj = h @ w_in
    qk_m, conv_new = causal_conv(proj[..., OFF_MQ:OFF_MV], conv_prev, conv_w, conv_b)
    heads = lambda t, nh: t.astype(jnp.float32).reshape(B, L, nh, -1)
    mq = heads(qk_m[..., :D_M], NH_M)
    mk = heads(qk_m[..., D_M:], NH_M) * DH_M ** -0.5
    mv = heads(proj[..., OFF_MV:OFF_MO], NH_M)
    mo = proj[..., OFF_MO:OFF_MI]
    gates = proj[..., OFF_MI:OFF_AQ].astype(jnp.float32) + b_gates.astype(jnp.float32)
    ig, fg = gates[..., :NH_M], gates[..., NH_M:]
    aq = proj[..., OFF_AQ:OFF_AK].reshape(B, L, NH_A, DH_A)
    ak = proj[..., OFF_AK:OFF_AV].reshape(B, L, NH_A, DH_A)
    av = proj[..., OFF_AV:D_IN].reshape(B, L, NH_A, DH_A)
    return mq, mk, mv, mo, ig, fg, aq, ak, av, conv_new


def mlstm_chunk(state, q, k, v, ig, fg):
    C, n, m = state
    L = q.shape[2]
    b = jnp.cumsum(jax.nn.log_sigmoid(fg), axis=-1)
    causal = jnp.tril(jnp.ones((L, L), dtype=bool))
    log_w = jnp.where(causal, b[..., :, None] - b[..., None, :] + ig[..., None, :], -jnp.inf)
    log_inter = b + m[..., None]
    m_t = jnp.maximum(log_inter, jnp.max(log_w, axis=-1))
    w_inter = jnp.exp(log_inter - m_t)
    a = jnp.exp(log_w - m_t[..., None]) * jnp.einsum('bhtd,bhsd->bhts', q, k)
    num = w_inter[..., None] * jnp.einsum('bhte,bhed->bhtd', q, C) + jnp.einsum('bhts,bhsd->bhtd', a, v)
    den = w_inter * jnp.einsum('bhte,bhe->bht', q, n) + jnp.sum(a, axis=-1)
    h = num / jnp.maximum(jnp.abs(den), jnp.exp(-m_t))[..., None]
    b_end = b[..., -1]
    log_g = b_end[..., None] - b + ig
    m_end = jnp.maximum(b_end + m, jnp.max(log_g, axis=-1))
    decay = jnp.exp(b_end + m - m_end)
    g = jnp.exp(log_g - m_end[..., None])
    C_new = decay[..., None, None] * C + jnp.einsum('bhs,bhse,bhsd->bhed', g, k, v)
    n_new = decay[..., None] * n + jnp.einsum('bhs,bhse->bhe', g, k)
    return (C_new, n_new, m_end), h


def mlstm_prompt(q, k, v, ig, fg):
    B, S = q.shape[:2]
    nc = S // CHUNK
    def chunks(t):
        t = t.reshape((B, nc, CHUNK) + t.shape[2:])
        return jnp.moveaxis(jnp.moveaxis(t, 1, 0), 3, 2)
    init = (jnp.zeros((B, NH_M, DH_M, DH_M), jnp.float32),
            jnp.zeros((B, NH_M, DH_M), jnp.float32),
            jnp.zeros((B, NH_M), jnp.float32))
    state, h = lax.scan(lambda s, xs: mlstm_chunk(s, *xs), init,
                        (chunks(q), chunks(k), chunks(v), chunks(ig), chunks(fg)))
    h = jnp.transpose(h, (1, 0, 3, 2, 4)).reshape(B, S, NH_M, DH_M)
    return state, h


def rel_position_bias(rel_table, q_pos, k_pos):
    d = jnp.clip(q_pos[:, None] - k_pos[None, :], -(CHUNK - 1), MAX_REL) + (CHUNK - 1)
    return rel_table[:, d]


def band_attention(q, k, v, q_pos, k_pos, rel_table):
    qc, kc = q_pos // CHUNK, k_pos // CHUNK
    valid = (k_pos[None, :] >= 0) & (kc[None, :] <= qc[:, None]) & (kc[None, :] >= qc[:, None] - LEFT_CHUNKS)
    s = jnp.einsum('bqhd,bkhd->bhqk', q, k).astype(jnp.float32) * DH_A ** -0.5
    s = s + rel_position_bias(rel_table, q_pos, k_pos).astype(jnp.float32)
    s = jnp.where(valid, s, -jnp.inf)
    p = jax.nn.softmax(s, axis=-1).astype(v.dtype)
    return jnp.einsum('bhqk,bkhd->bqhd', p, v)


def attention_prompt(q, k, v, rel_table):
    B, S = q.shape[:2]
    nc = S // CHUNK
    band = ATT_WINDOW + CHUNK
    pad = ((0, 0), (ATT_WINDOW, 0), (0, 0), (0, 0))
    kp, vp = jnp.pad(k, pad), jnp.pad(v, pad)
    def one_chunk(ci):
        start = ci * CHUNK
        qn = lax.dynamic_slice_in_dim(q, start, CHUNK, axis=1)
        kn = lax.dynamic_slice_in_dim(kp, start, band, axis=1)
        vn = lax.dynamic_slice_in_dim(vp, start, band, axis=1)
        q_pos = start + jnp.arange(CHUNK)
        k_pos = start - ATT_WINDOW + jnp.arange(band)
        return band_attention(qn, kn, vn, q_pos, k_pos, rel_table)
    out = lax.map(one_chunk, jnp.arange(nc))
    return jnp.moveaxis(out, 0, 1).reshape(B, S, D_A)


def mixer_out(hm, mo, g_mlstm, att, w_out):
    B, L = hm.shape[:2]
    hn = hm * lax.rsqrt(jnp.mean(hm * hm, axis=-1, keepdims=True) + EPS)
    hn = hn.reshape(B, L, D_M) * g_mlstm.astype(jnp.float32)
    hm_out = (jax.nn.sigmoid(mo.astype(jnp.float32)) * hn).astype(att.dtype)
    return jnp.concatenate([hm_out, att], axis=-1) @ w_out


def run_layer(x, c, conv_prev, m_state, kv_cache, w_ada, b_ada, g_norm, w_ffn1_up, w_ffn1_down,
              w_ffn2_up, w_ffn2_down, w_in, conv_w, conv_b, b_gates, g_mlstm, rel_table, w_out):
    B, L, _ = x.shape
    sh1, sc1, gt1, sh2, sc2, gt2, sh3, sc3, gt3 = adaln_params(c, w_ada, b_ada)
    x = half_ffn(x, sh1, sc1, gt1, g_norm[0], w_ffn1_up, w_ffn1_down)
    h = modulate(rmsnorm(x, g_norm[1]), sh2, sc2)
    if conv_prev is None:
        conv_prev = jnp.zeros((B, CONV_W - 1, 2 * D_M), h.dtype)
    mq, mk, mv, mo, ig, fg, aq, ak, av, conv_new = project_in(h, w_in, conv_prev, conv_w, conv_b, b_gates)
    if kv_cache is None:
        m_new, hm = mlstm_prompt(mq, mk, mv, ig, fg)
        att = attention_prompt(aq, ak, av, rel_table)
        n_keep = min(ATT_WINDOW, L)
        k_new, v_new = ak[:, L - n_keep:], av[:, L - n_keep:]
    else:
        cache_k, cache_v = kv_cache
        to_bhl = lambda t: jnp.swapaxes(t, 1, 2)
        m_prev = tuple(s.astype(jnp.float32) for s in m_state)
        m_new, hm = mlstm_chunk(m_prev, to_bhl(mq), to_bhl(mk), to_bhl(mv), to_bhl(ig), to_bhl(fg))
        hm = to_bhl(hm)
        n_c = cache_k.shape[1]
        q_pos = PAST_LEN + jnp.arange(L)
        k_pos = jnp.concatenate([PAST_LEN - n_c + jnp.arange(n_c), q_pos])
        keys = jnp.concatenate([cache_k.astype(ak.dtype), ak], axis=1)
        vals = jnp.concatenate([cache_v.astype(av.dtype), av], axis=1)
        att = band_attention(aq, keys, vals, q_pos, k_pos, rel_table).reshape(B, L, D_A)
        k_new, v_new = ak, av
    x = x + gt2 * mixer_out(hm, mo, g_mlstm, att, w_out)
    x = half_ffn(x, sh3, sc3, gt3, g_norm[2], w_ffn2_up, w_ffn2_down)
    return x, m_new + (conv_new, k_new, v_new)


def setup_inputs(seed: int = 0) -> dict:
    key = jax.random.key(seed)
    ks = jax.random.split(key, 32)
    nrm = lambda k, shape, s: s * jax.random.normal(k, shape, jnp.float32)
    n_att = min(ATT_WINDOW, PAST_LEN)
    return {
        'x_prompt': nrm(ks[0], (BATCH, SEQ, D_MODEL), 1.0),
        'x_sample': nrm(ks[1], (DEC_BATCH, DEC_SEQ, D_MODEL), 1.0),
        'state_mlstm_C': nrm(ks[2], (DEPTH, DEC_BATCH, NH_M, DH_M, DH_M), 0.05),
        'state_mlstm_n': nrm(ks[3], (DEPTH, DEC_BATCH, NH_M, DH_M), 0.1),
        'state_mlstm_m': nrm(ks[4], (DEPTH, DEC_BATCH, NH_M), 1.0),
        'state_conv': nrm(ks[5], (DEPTH, DEC_BATCH, CONV_W - 1, 2 * D_M), 1.0),
        'cache_att_k': nrm(ks[6], (DEPTH, DEC_BATCH, n_att, NH_A, DH_A), 1.0),
        'cache_att_v': nrm(ks[7], (DEPTH, DEC_BATCH, n_att, NH_A, DH_A), 1.0),
        'c_prompt': nrm(ks[8], (BATCH, D_MODEL), 1.0),
        'c_sample': nrm(ks[9], (DEC_BATCH, D_MODEL), 1.0),
        'w_ada': nrm(ks[10], (DEPTH, D_MODEL, N_MOD * D_MODEL), 0.5 * D_MODEL ** -0.5),
        'b_ada': nrm(ks[11], (DEPTH, N_MOD * D_MODEL), 0.1),
        'g_norm': 1.0 + nrm(ks[12], (DEPTH, 3, D_MODEL), 0.1),
        'w_ffn1_up': nrm(ks[13], (DEPTH, D_MODEL, 2 * D_FF), D_MODEL ** -0.5),
        'w_ffn1_down': nrm(ks[14], (DEPTH, D_FF, D_MODEL), D_FF ** -0.5),
        'w_ffn2_up': nrm(ks[15], (DEPTH, D_MODEL, 2 * D_FF), D_MODEL ** -0.5),
        'w_ffn2_down': nrm(ks[16], (DEPTH, D_FF, D_MODEL), D_FF ** -0.5),
        'w_in': nrm(ks[17], (DEPTH, D_MODEL, D_IN), D_MODEL ** -0.5),
        'conv_w': nrm(ks[18], (DEPTH, CONV_W, 2 * D_M), CONV_W ** -0.5),
        'conv_b': nrm(ks[19], (DEPTH, 2 * D_M), 0.01),
        'b_gates': jnp.concatenate([nrm(ks[20], (DEPTH, NH_M), 0.1),
                                    3.0 + nrm(ks[21], (DEPTH, NH_M), 0.5)], axis=-1),
        'g_mlstm': 1.0 + nrm(ks[22], (DEPTH, D_M), 0.1),
        'rel_bias_table': nrm(ks[23], (DEPTH, NH_A, N_REL), 0.5),
        'w_out': nrm(ks[24], (DEPTH, D_MIX, D_MODEL), D_MIX ** -0.5),
        'g_final': 1.0 + nrm(ks[25], (D_MODEL,), 0.1),
    }


def reference(x_prompt, x_sample, state_mlstm_C, state_mlstm_n, state_mlstm_m, state_conv,
              cache_att_k, cache_att_v, c_prompt, c_sample, w_ada, b_ada, g_norm,
              w_ffn1_up, w_ffn1_down, w_ffn2_up, w_ffn2_down, w_in, conv_w, conv_b, b_gates,
              g_mlstm, rel_bias_table, w_out, g_final):
    xp, xs = x_prompt, x_sample
    st_p, st_s = [], []
    for l in range(DEPTH):
        w = (w_ada[l], b_ada[l], g_norm[l], w_ffn1_up[l], w_ffn1_down[l], w_ffn2_up[l], w_ffn2_down[l],
             w_in[l], conv_w[l], conv_b[l], b_gates[l], g_mlstm[l], rel_bias_table[l], w_out[l])
        xp, sp = run_layer(xp, c_prompt, None, None, None, *w)
        xs, ss = run_layer(xs, c_sample, state_conv[l],
                           (state_mlstm_C[l], state_mlstm_n[l], state_mlstm_m[l]),
                           (cache_att_k[l], cache_att_v[l]), *w)
        st_p.append(sp)
        st_s.append(ss)
    stk = lambda sts, i: jnp.stack([s[i] for s in sts])
    mlstm_C_prompt = stk(st_p, 0)
    mlstm_n_prompt = stk(st_p, 1)
    mlstm_m_prompt = stk(st_p, 2)
    conv_prompt = stk(st_p, 3)
    att_k_prompt = stk(st_p, 4)
    att_v_prompt = stk(st_p, 5)
    mlstm_C_sample = stk(st_s, 0)
    mlstm_n_sample = stk(st_s, 1)
    mlstm_m_sample = stk(st_s, 2)
    conv_sample = stk(st_s, 3)
    att_k_sample = stk(st_s, 4)
    att_v_sample = stk(st_s, 5)
    y_prompt = rmsnorm(xp, g_final)
    y_sample = rmsnorm(xs, g_final)
    return (y_prompt, y_sample, mlstm_C_prompt, mlstm_n_prompt, mlstm_m_prompt, conv_prompt,
            att_k_prompt, att_v_prompt, mlstm_C_sample, mlstm_n_sample, mlstm_m_sample, conv_sample,
            att_k_sample, att_v_sample)
```

```python
import functools

import jax
import jax.numpy as jnp
from jax import lax
from jax.experimental import pallas as pl
from jax.experimental.pallas import tpu as pltpu

F32 = jnp.float32
BF16 = jnp.bfloat16

CHUNK = 64
NH_M = 4
DH_M = 128
D_M = NH_M * DH_M
NH_A = 8
DH_A = 64
D_A = NH_A * DH_A
CONV_W = 4
LEFT_CHUNKS = 8
ATT_WINDOW = LEFT_CHUNKS * CHUNK
MAX_REL = 2 * CHUNK
N_MOD = 9
EPS = 1e-6
NEG = -1e30

LANES = 128
V7X_VMEM_BYTES = 64 * 1024 * 1024
VMEM_LIMIT = V7X_VMEM_BYTES - 8 * 1024 * 1024

FF_CHUNK = 256
MLSTM_CHUNK = 256
ATT_TQ = 256


def _cparams(n_axes):
    return pltpu.CompilerParams(dimension_semantics=("arbitrary",) * n_axes,
                                vmem_limit_bytes=VMEM_LIMIT)


def _const_spec(shape):
    nd = len(shape)
    return pl.BlockSpec(shape, lambda *_: (0,) * nd, pipeline_mode=pl.Buffered(1))


def _dot(a, b):
    return jnp.dot(a, b, preferred_element_type=F32)


def _dot_nt(a, b):
    return lax.dot_general(a, b, (((1,), (1,)), ((), ())), preferred_element_type=F32)


def _dot_tn(a, b):
    return lax.dot_general(a, b, (((0,), (0,)), ((), ())), preferred_element_type=F32)


def _sigmoid(x):
    return 1.0 / (1.0 + jnp.exp(-x))


def _log_sigmoid(x):
    return jnp.minimum(x, 0.0) - jnp.log(1.0 + jnp.exp(-jnp.abs(x)))


def _norm_mod(x, g, shift, scale):
    ms = jnp.mean(x * x, axis=-1, keepdims=True)
    h = x * lax.rsqrt(ms + EPS) * g
    return h * (1.0 + scale) + shift


def _adaln_kernel(c_ref, w_ref, b_ref, o_ref):
    c = c_ref[...]
    o_ref[...] = _dot(c * _sigmoid(c), w_ref[...]) + b_ref[...]


def _adaln(c_all, w_ada, b_ada):
    rows, d = c_all.shape
    n = w_ada.shape[1]
    tn = d
    return pl.pallas_call(
        _adaln_kernel,
        out_shape=jax.ShapeDtypeStruct((rows, n), F32),
        grid=(n // tn,),
        in_specs=[pl.BlockSpec((rows, d), lambda j: (0, 0)),
                  pl.BlockSpec((d, tn), lambda j: (0, j)),
                  pl.BlockSpec((1, tn), lambda j: (0, j))],
        out_specs=pl.BlockSpec((rows, tn), lambda j: (0, j)),
        compiler_params=_cparams(1),
        name="adaln",
    )(c_all, w_ada, b_ada.reshape(1, n))


def _ffn_kernel(*refs, nb, tl, sub, with_mix, with_final):
    if with_mix:
        x_ref, hm_ref, att_ref, mod_ref, gn_ref, wout_ref, wup_ref, wdn_ref, gfin_ref, o_ref, hb_ref, act_ref = refs
    else:
        x_ref, mod_ref, gn_ref, wup_ref, wdn_ref, gfin_ref, o_ref, hb_ref, act_ref = refs
    m = nb * tl
    d = x_ref.shape[-1]
    dff = wdn_ref.shape[0]
    mod = mod_ref[...]
    shift, scale, gate = (mod[:, 3 * sub + i:3 * sub + i + 1, :] for i in range(3))

    if with_mix:
        mix = _dot(hm_ref[...].reshape(m, D_M), wout_ref[0:D_M, :])
        mix = mix + _dot(att_ref[...].reshape(m, D_A), wout_ref[D_M:, :])
        o_ref[...] = x_ref[...] + mod[:, 5:6, :] * mix.reshape(nb, tl, d)
    else:
        o_ref[...] = x_ref[...]

    h = _norm_mod(o_ref[...], gn_ref[...], shift, scale)
    hb_ref[...] = h.reshape(m, d).astype(BF16)
    for c in range(dff // FF_CHUNK):
        lo = c * FF_CHUNK
        g = _dot(hb_ref[...], wup_ref[:, lo:lo + FF_CHUNK])
        u = _dot(hb_ref[...], wup_ref[:, dff + lo:dff + lo + FF_CHUNK])
        act_ref[:, lo:lo + FF_CHUNK] = (g * _sigmoid(g) * u).astype(BF16)
    dn = _dot(act_ref[...], wdn_ref[...])
    y = o_ref[...] + 0.5 * gate * dn.reshape(nb, tl, d)
    if with_final:
        ms = jnp.mean(y * y, axis=-1, keepdims=True)
        y = y * lax.rsqrt(ms + EPS) * gfin_ref[...]
    o_ref[...] = y


def _ffn(x, mod, g_norm_row, w_up, w_dn, g_final, *, nb, tl, sub, mix=None, final=False):
    NB, L, d = x.shape
    dff = w_dn.shape[0]
    m = nb * tl
    grid = (NB // nb, L // tl)
    row = lambda w: pl.BlockSpec((nb, tl, w), lambda i, j: (i, j, 0))
    in_specs = [row(d)]
    args = [x]
    if mix is not None:
        hm, att, w_out = mix
        in_specs += [row(D_M), row(D_A)]
        args += [hm, att]
    in_specs += [pl.BlockSpec((nb, N_MOD, d), lambda i, j: (i, 0, 0)), _const_spec((1, d))]
    args += [mod, g_norm_row]
    if mix is not None:
        in_specs.append(_const_spec(w_out.shape))
        args.append(w_out)
    in_specs += [_const_spec(w_up.shape), _const_spec(w_dn.shape), _const_spec((1, d))]
    args += [w_up, w_dn, g_final]
    kern = functools.partial(_ffn_kernel, nb=nb, tl=tl, sub=sub, with_mix=mix is not None,
                             with_final=final)
    return pl.pallas_call(
        kern,
        out_shape=jax.ShapeDtypeStruct((NB, L, d), F32),
        grid=grid,
        in_specs=in_specs,
        out_specs=row(d),
        scratch_shapes=[pltpu.VMEM((m, d), BF16), pltpu.VMEM((m, dff), BF16)],
        compiler_params=_cparams(2),
        name="ffn%d" % sub,
    )(*args)


def _proj_kernel(x_ref, mod_ref, gn_ref, w_ref, wg_ref, bg_ref, cw_ref, cb_ref, cprev_ref,
                 q_ref, k_ref, v_ref, o_ref, gates_ref, aq_ref, ak_ref, av_ref,
                 kt_ref, vt_ref, cnew_ref, hb_ref, ext_ref, *, nb, tl):
    m = nb * tl
    d = x_ref.shape[-1]
    j = pl.program_id(1)
    mod = mod_ref[...]
    h = _norm_mod(x_ref[...], gn_ref[...], mod[:, 3:4, :], mod[:, 4:5, :])
    hb_ref[...] = h.reshape(m, d).astype(BF16)

    @pl.when(j == 0)
    def _():
        ext_ref[:, 8 - (CONV_W - 1):8, :] = cprev_ref[...]

    for c in range(2 * D_M // FF_CHUNK):
        lo = c * FF_CHUNK
        pre = _dot(hb_ref[...], w_ref[:, lo:lo + FF_CHUNK])
        ext_ref[:, 8:8 + tl, lo:lo + FF_CHUNK] = pre.reshape(nb, tl, FF_CHUNK)
    for c in range(2 * D_M // FF_CHUNK):
        lo = c * FF_CHUNK
        acc = cb_ref[:, lo:lo + FF_CHUNK]
        for t in range(CONV_W):
            s0 = 8 - (CONV_W - 1) + t
            acc = acc + cw_ref[t:t + 1, lo:lo + FF_CHUNK] * ext_ref[:, s0:s0 + tl, lo:lo + FF_CHUNK]
        y = acc * _sigmoid(acc)
        if lo < D_M:
            q_ref[:, :, lo:lo + FF_CHUNK] = y.astype(BF16)
        else:
            k_ref[:, :, lo - D_M:lo - D_M + FF_CHUNK] = (y * DH_M ** -0.5).astype(BF16)
    cnew_ref[...] = ext_ref[:, 8 + tl - (CONV_W - 1):8 + tl, :]
    ext_ref[:, 0:8, :] = ext_ref[:, tl:tl + 8, :]

    def cols(c0):
        return _dot(hb_ref[...], w_ref[:, c0:c0 + D_M]).reshape(nb, tl, D_M)

    v_ref[...] = cols(2 * D_M).astype(BF16)
    o_ref[...] = cols(3 * D_M)
    aq_ref[...] = cols(4 * D_M).astype(BF16)
    ak = cols(4 * D_M + D_A)
    ak_ref[...] = ak.astype(BF16)
    kt_ref[...] = ak
    av = cols(4 * D_M + 2 * D_A)
    av_ref[...] = av.astype(BF16)
    vt_ref[...] = av
    gates_ref[...] = (_dot(hb_ref[...], wg_ref[...]) + bg_ref[...]).reshape(nb, tl, LANES)


def _proj(x, mod, g_norm_row, w_main, w_gates, b_gates, conv_w, conv_b, conv_prev, *, nb, tl):
    NB, L, d = x.shape
    assert tl == min(ATT_WINDOW, L)
    m = nb * tl
    grid = (NB // nb, L // tl)
    row = lambda w: pl.BlockSpec((nb, tl, w), lambda i, j: (i, j, 0))
    per_b = lambda r, w: pl.BlockSpec((nb, r, w), lambda i, j: (i, 0, 0))
    bshape = lambda w, dt: jax.ShapeDtypeStruct((NB, L, w), dt)
    in_specs = [row(d), per_b(N_MOD, d), _const_spec((1, d)), _const_spec(w_main.shape),
                _const_spec(w_gates.shape), _const_spec((1, LANES)), _const_spec(conv_w.shape),
                _const_spec((1, 2 * D_M)), per_b(CONV_W - 1, 2 * D_M)]
    out_shape = [bshape(D_M, BF16), bshape(D_M, BF16), bshape(D_M, BF16), bshape(D_M, F32),
                 bshape(LANES, F32), bshape(D_A, BF16), bshape(D_A, BF16), bshape(D_A, BF16),
                 jax.ShapeDtypeStruct((NB, tl, D_A), F32), jax.ShapeDtypeStruct((NB, tl, D_A), F32),
                 jax.ShapeDtypeStruct((NB, CONV_W - 1, 2 * D_M), F32)]
    out_specs = [row(D_M), row(D_M), row(D_M), row(D_M), row(LANES), row(D_A), row(D_A), row(D_A),
                 per_b(tl, D_A), per_b(tl, D_A), per_b(CONV_W - 1, 2 * D_M)]
    return pl.pallas_call(
        functools.partial(_proj_kernel, nb=nb, tl=tl),
        out_shape=out_shape,
        grid=grid,
        in_specs=in_specs,
        out_specs=out_specs,
        scratch_shapes=[pltpu.VMEM((m, d), BF16), pltpu.VMEM((nb, tl + 8, 2 * D_M), F32)],
        compiler_params=_cparams(2),
        name="proj",
    )(x, mod, g_norm_row, w_main, w_gates, b_gates, conv_w, conv_b, conv_prev)


def _mlstm_kernel(q_ref, k_ref, v_ref, gates_ref, mo_ref, c0_ref, n0_ref, m0_ref, gm_ref,
                  hm_ref, c_ref, n_ref, m_ref, *, lc):
    j = pl.program_id(1)

    @pl.when(j == 0)
    def _():
        c_ref[...] = c0_ref[...]
        n_ref[...] = n0_ref[...]
        m_ref[...] = m0_ref[...]

    gates = gates_ref[0]
    gates_t = jnp.transpose(gates)
    r_i = lax.broadcasted_iota(jnp.int32, (lc, lc), 0)
    c_i = lax.broadcasted_iota(jnp.int32, (lc, lc), 1)
    causal = r_i >= c_i
    lower = jnp.where(causal, 1.0, 0.0).astype(F32)
    upper = jnp.where(r_i <= c_i, 1.0, 0.0).astype(F32)
    hi = lax.Precision.HIGHEST
    b_col_all = jnp.dot(lower, _log_sigmoid(gates), precision=hi, preferred_element_type=F32)
    b_row_all = jnp.dot(_log_sigmoid(gates_t), upper, precision=hi, preferred_element_type=F32)

    for h in range(NH_M):
        sl = slice(h * DH_M, (h + 1) * DH_M)
        q = q_ref[0, :, sl]
        k = k_ref[0, :, sl]
        v = v_ref[0, :, sl]
        b_col = b_col_all[:, NH_M + h:NH_M + h + 1]
        b_row = b_row_all[NH_M + h:NH_M + h + 1, :]
        ig_col = gates[:, h:h + 1]
        ig_row = gates_t[h:h + 1, :]
        m_prev = m_ref[0, h][:, 0:1]
        c_prev = c_ref[0, h]
        n_prev = n_ref[0, h]

        log_w = jnp.where(causal, b_col - b_row + ig_row, NEG)
        log_inter = b_col + m_prev
        m_t = jnp.maximum(log_inter, jnp.max(log_w, axis=-1, keepdims=True))
        w_inter = jnp.exp(log_inter - m_t)
        a = jnp.exp(log_w - m_t) * _dot_nt(q, k)
        qf = q.astype(F32)
        num = w_inter * _dot(q, c_prev.astype(BF16)) + _dot(a.astype(BF16), v)
        den = w_inter * jnp.sum(qf * n_prev, axis=-1, keepdims=True) + jnp.sum(a, axis=-1, keepdims=True)
        hh = num / jnp.maximum(jnp.abs(den), jnp.exp(-m_t))

        b_end = b_col[lc - 1:lc, :]
        log_g_row = b_end - b_row + ig_row
        m_end = jnp.maximum(b_end + m_prev, jnp.max(log_g_row, axis=-1, keepdims=True))
        decay = jnp.exp(b_end + m_prev - m_end)
        g_col = jnp.exp(b_end - b_col + ig_col - m_end)
        kg = k.astype(F32) * g_col
        c_ref[0, h] = decay * c_prev + _dot_tn(kg.astype(BF16), v)
        n_ref[0, h] = decay * n_prev + jnp.sum(kg, axis=0, keepdims=True)
        m_ref[0, h] = jnp.broadcast_to(m_end, (1, LANES))

        hn = hh * lax.rsqrt(jnp.mean(hh * hh, axis=-1, keepdims=True) + EPS) * gm_ref[:, sl]
        hm_ref[0, :, sl] = (_sigmoid(mo_ref[0, :, sl]) * hn).astype(BF16)


def _mlstm(q, k, v, gates, mo, c0, n0, m0, g_mlstm_row, *, lc):
    NB, L, _ = q.shape
    grid = (NB, L // lc)
    row = lambda w: pl.BlockSpec((1, lc, w), lambda b, j: (b, j, 0))
    st = lambda s: pl.BlockSpec((1,) + s, lambda b, j: (b,) + (0,) * len(s))
    c_s, n_s = (NH_M, DH_M, DH_M), (NH_M, 1, DH_M)
    return pl.pallas_call(
        functools.partial(_mlstm_kernel, lc=lc),
        out_shape=[jax.ShapeDtypeStruct((NB, L, D_M), BF16),
                   jax.ShapeDtypeStruct((NB,) + c_s, F32),
                   jax.ShapeDtypeStruct((NB,) + n_s, F32),
                   jax.ShapeDtypeStruct((NB,) + n_s, F32)],
        grid=grid,
        in_specs=[row(D_M), row(D_M), row(D_M), row(LANES), row(D_M), st(c_s), st(n_s), st(n_s),
                  _const_spec((1, D_M))],
        out_specs=[row(D_M), st(c_s), st(n_s), st(n_s)],
        compiler_params=_cparams(2),
        name="mlstm",
    )(q, k, v, gates, mo, c0, n0, m0, g_mlstm_row)


def _attn_tile(q, kvs, out_ref_set):
    tq = q.shape[0]
    lane = lax.broadcasted_iota(jnp.int32, (1, LANES), 1)
    first = lane < DH_A
    s0 = jnp.where(first, DH_A ** -0.5, 0.0).astype(BF16)
    s1 = jnp.where(first, 0.0, DH_A ** -0.5).astype(BF16)
    for p in range(NH_A // 2):
        sl = slice(p * LANES, (p + 1) * LANES)
        qp = q[:, sl]
        q2 = jnp.concatenate([qp * s0, qp * s1], axis=0)
        ss = []
        for k, _, bias_ref, pen in kvs:
            s = _dot_nt(q2, k[:, sl]) + bias_ref[p]
            if pen is not None:
                s = s + pen
            ss.append(s)
        mx = functools.reduce(jnp.maximum, [jnp.max(s, axis=-1, keepdims=True) for s in ss])
        l = 0.0
        o2 = 0.0
        for s, (_, v, _, _) in zip(ss, kvs):
            e = jnp.exp(s - mx)
            l = l + jnp.sum(e, axis=-1, keepdims=True)
            o2 = o2 + _dot(e.astype(BF16), v[:, sl])
        o2 = o2 / l
        out_ref_set(sl, jnp.where(first, o2[:tq], o2[tq:]))


def _attn_prompt_kernel(q_ref, k0_ref, k1_ref, k2_ref, v0_ref, v1_ref, v2_ref,
                        b0_ref, b1_ref, b2_ref, o_ref):
    j = pl.program_id(1)
    pen0 = jnp.where(j >= 2, 0.0, NEG).astype(F32)
    pen1 = jnp.where(j >= 1, 0.0, NEG).astype(F32)

    def put(sl, val):
        o_ref[0, :, sl] = val.astype(o_ref.dtype)

    _attn_tile(q_ref[0],
               [(k0_ref[0], v0_ref[0], b0_ref, pen0), (k1_ref[0], v1_ref[0], b1_ref, pen1),
                (k2_ref[0], v2_ref[0], b2_ref, None)], put)


def _attn_prompt(aq, ak, av, bias):
    NB, L, _ = aq.shape
    tq = ATT_TQ
    grid = (NB, L // tq)
    row = pl.BlockSpec((1, tq, D_A), lambda b, j: (b, j, 0))
    back = lambda n: pl.BlockSpec((1, tq, D_A), lambda b, j: (b, jnp.maximum(j - n, 0), 0))
    bias_spec = lambda i: pl.BlockSpec((NH_A // 2, 2 * tq, tq), lambda b, j: (0, 0, i),
                                       pipeline_mode=pl.Buffered(1))
    return pl.pallas_call(
        _attn_prompt_kernel,
        out_shape=jax.ShapeDtypeStruct((NB, L, D_A), BF16),
        grid=grid,
        in_specs=[row, back(2), back(1), row, back(2), back(1), row,
                  bias_spec(0), bias_spec(1), bias_spec(2)],
        out_specs=row,
        compiler_params=_cparams(2),
        name="attn_prompt",
    )(aq, ak, ak, ak, av, av, av, bias, bias, bias)


def _attn_sample_kernel(q_ref, k_ref, v_ref, ck_ref, cv_ref, bc_ref, bo_ref, o_ref, *, nb):
    for b in range(nb):
        def put(sl, val, b=b):
            o_ref[b, :, sl] = val.astype(o_ref.dtype)

        _attn_tile(q_ref[b],
                   [(ck_ref[b].astype(BF16), cv_ref[b].astype(BF16), bc_ref, None),
                    (k_ref[b], v_ref[b], bo_ref, None)], put)


def _attn_sample(aq, ak, av, cache_k, cache_v, bias_c, bias_o, *, nb):
    NB, L, _ = aq.shape
    nc = cache_k.shape[1]
    row = pl.BlockSpec((nb, L, D_A), lambda i: (i, 0, 0))
    crow = pl.BlockSpec((nb, nc, D_A), lambda i: (i, 0, 0))
    return pl.pallas_call(
        functools.partial(_attn_sample_kernel, nb=nb),
        out_shape=jax.ShapeDtypeStruct((NB, L, D_A), BF16),
        grid=(NB // nb,),
        in_specs=[row, row, row, crow, crow, _const_spec(bias_c.shape), _const_spec(bias_o.shape)],
        out_specs=row,
        compiler_params=_cparams(1),
        name="attn_sample",
    )(aq, ak, av, cache_k, cache_v, bias_c, bias_o)


def _band_bias(rel_table, tq, klen):
    q_pos = (klen - tq) + jnp.arange(tq)
    k_pos = jnp.arange(klen)
    dist = jnp.clip(q_pos[:, None] - k_pos[None, :], -(CHUNK - 1), MAX_REL) + (CHUNK - 1)
    qc, kc = q_pos // CHUNK, k_pos // CHUNK
    valid = (kc[None, :] <= qc[:, None]) & (kc[None, :] >= qc[:, None] - LEFT_CHUNKS)
    bias = jnp.where(valid[None], rel_table[:, dist].astype(F32), NEG)
    return bias.reshape(NH_A // 2, 2 * tq, klen)


def _layer(x, mod, conv_prev, state, cache, w, *, nb, tl, lc, final):
    NB, L, d = x.shape
    x1 = _ffn(x, mod, w["g0"], w["up1"], w["dn1"], w["g_final"], nb=nb, tl=tl, sub=0)
    (q, k, v, mo, gates, aq, ak, av, k_tail, v_tail, conv_new) = _proj(
        x1, mod, w["g1"], w["w_main"], w["w_gates"], w["b_gates"], w["conv_w"], w["conv_b"],
        conv_prev, nb=nb, tl=tl)
    c0, n0, m0 = state
    hm, c_new, n_new, m_new = _mlstm(q, k, v, gates, mo, c0, n0, m0, w["g_mlstm"], lc=lc)
    if cache is None:
        att = _attn_prompt(aq, ak, av, w["bias_prompt"])
    else:
        att = _attn_sample(aq, ak, av, cache[0], cache[1], w["bias_cache"], w["bias_own"], nb=nb)
    y = _ffn(x1, mod, w["g2"], w["up2"], w["dn2"], w["g_final"], nb=nb, tl=tl, sub=2,
             mix=(hm, att, w["w_out"]), final=final)
    n_keep = k_tail.shape[1]
    states = (c_new, n_new[:, :, 0, :], m_new[:, :, 0, 0], conv_new,
              k_tail.reshape(NB, n_keep, NH_A, DH_A), v_tail.reshape(NB, n_keep, NH_A, DH_A))
    return y, states


def kernel(x_prompt, x_sample, state_mlstm_C, state_mlstm_n, state_mlstm_m, state_conv, cache_att_k, cache_att_v, c_prompt, c_sample, w_ada, b_ada, g_norm, w_ffn1_up, w_ffn1_down, w_ffn2_up, w_ffn2_down, w_in, conv_w, conv_b, b_gates, g_mlstm, rel_bias_table, w_out, g_final):
    depth = w_ada.shape[0]
    bp, seq, d = x_prompt.shape
    bs, dseq, _ = x_sample.shape
    xp, xs = x_prompt, x_sample
    st_p, st_s = [], []
    n_c = cache_att_k.shape[2]
    tl_p = min(ATT_WINDOW, seq)
    lc_p = min(MLSTM_CHUNK, seq)
    nb_s = max(1, min(bs, ATT_WINDOW // dseq))
    for l in range(depth):
        rows = bp + bs
        pad = (-rows) % 8
        c_all = jnp.concatenate([c_prompt, c_sample, jnp.zeros((pad, d), F32)], axis=0)
        mod = _adaln(c_all, w_ada[l], b_ada[l])
        mod_p = mod[:bp].reshape(bp, N_MOD, d)
        mod_s = mod[bp:rows].reshape(bs, N_MOD, d)
        off_g = 4 * D_M
        off_a = off_g + 2 * NH_M
        wl = w_in[l]
        rel = rel_bias_table[l]
        bias_s = _band_bias(rel, dseq, n_c + dseq)
        w = {
            "g0": g_norm[l, 0:1], "g1": g_norm[l, 1:2], "g2": g_norm[l, 2:3],
            "g_final": g_final.reshape(1, d),
            "up1": w_ffn1_up[l].astype(BF16), "dn1": w_ffn1_down[l].astype(BF16),
            "up2": w_ffn2_up[l].astype(BF16), "dn2": w_ffn2_down[l].astype(BF16),
            "w_main": jnp.concatenate([wl[:, :off_g], wl[:, off_a:]], axis=1).astype(BF16),
            "w_gates": jnp.pad(wl[:, off_g:off_a], ((0, 0), (0, LANES - 2 * NH_M))).astype(BF16),
            "b_gates": jnp.pad(b_gates[l], (0, LANES - 2 * NH_M)).reshape(1, LANES),
            "conv_w": conv_w[l], "conv_b": conv_b[l].reshape(1, 2 * D_M),
            "g_mlstm": g_mlstm[l].reshape(1, D_M),
            "w_out": w_out[l].astype(BF16),
            "bias_prompt": _band_bias(rel, ATT_TQ, 3 * ATT_TQ),
            "bias_cache": bias_s[:, :, :n_c], "bias_own": bias_s[:, :, n_c:],
        }
        zero_state = (jnp.zeros((bp, NH_M, DH_M, DH_M), F32), jnp.zeros((bp, NH_M, 1, DH_M), F32),
                      jnp.zeros((bp, NH_M, 1, LANES), F32))
        xp, sp = _layer(xp, mod_p, jnp.zeros((bp, CONV_W - 1, 2 * D_M), F32), zero_state, None, w,
                        nb=1, tl=tl_p, lc=lc_p, final=l == depth - 1)
        state_s = (state_mlstm_C[l], state_mlstm_n[l][:, :, None, :],
                   jnp.broadcast_to(state_mlstm_m[l][:, :, None, None], (bs, NH_M, 1, LANES)))
        cache = (cache_att_k[l].reshape(bs, n_c, D_A), cache_att_v[l].reshape(bs, n_c, D_A))
        xs, ss = _layer(xs, mod_s, state_conv[l], state_s, cache, w, nb=nb_s, tl=dseq, lc=dseq,
                        final=l == depth - 1)
        st_p.append(sp)
        st_s.append(ss)
    stk = lambda sts, i: jnp.stack([s[i] for s in sts])
    return ((xp, xs) + tuple(stk(st_p, i) for i in range(6)) + tuple(stk(st_s, i) for i in range(6)))
```

```python
import functools

import jax
import jax.numpy as jnp
from jax import lax
from jax.experimental import pallas as pl
from jax.experimental.pallas import tpu as pltpu

F32 = jnp.float32
BF16 = jnp.bfloat16

CHUNK = 64
NH_M = 4
DH_M = 128
D_M = NH_M * DH_M
NH_A = 8
DH_A = 64
D_A = NH_A * DH_A
CONV_W = 4
LEFT_CHUNKS = 8
ATT_WINDOW = LEFT_CHUNKS * CHUNK
MAX_REL = 2 * CHUNK
N_MOD = 9
EPS = 1e-6
NEG = -1e30

LANES = 128
V7X_VMEM_BYTES = 64 * 1024 * 1024
VMEM_LIMIT = V7X_VMEM_BYTES - 8 * 1024 * 1024

FF_CHUNK = 256
MLSTM_CHUNK = 256
ATT_TQ = 256


def _cparams(n_axes):
    return pltpu.CompilerParams(dimension_semantics=("arbitrary",) * n_axes,
                                vmem_limit_bytes=VMEM_LIMIT)


def _const_spec(shape):
    nd = len(shape)
    return pl.BlockSpec(shape, lambda *_: (0,) * nd, pipeline_mode=pl.Buffered(1))


def _dot(a, b):
    return jnp.dot(a, b, preferred_element_type=F32)


def _dot_nt(a, b):
    return lax.dot_general(a, b, (((1,), (1,)), ((), ())), preferred_element_type=F32)


def _dot_tn(a, b):
    return lax.dot_general(a, b, (((0,), (0,)), ((), ())), preferred_element_type=F32)


def _sigmoid(x):
    return 1.0 / (1.0 + jnp.exp(-x))


def _log_sigmoid(x):
    return jnp.minimum(x, 0.0) - jnp.log(1.0 + jnp.exp(-jnp.abs(x)))


def _norm_mod(x, g, shift, scale):
    ms = jnp.mean(x * x, axis=-1, keepdims=True)
    h = x * lax.rsqrt(ms + EPS) * g
    return h * (1.0 + scale) + shift


def _adaln_kernel(c_ref, w_ref, b_ref, o_ref):
    c = c_ref[...]
    o_ref[...] = _dot(c * _sigmoid(c), w_ref[...]) + b_ref[...]


def _adaln(c_all, w_ada, b_ada):
    rows, d = c_all.shape
    n = w_ada.shape[1]
    tn = d
    return pl.pallas_call(
        _adaln_kernel,
        out_shape=jax.ShapeDtypeStruct((rows, n), F32),
        grid=(n // tn,),
        in_specs=[pl.BlockSpec((rows, d), lambda j: (0, 0)),
                  pl.BlockSpec((d, tn), lambda j: (0, j)),
                  pl.BlockSpec((1, tn), lambda j: (0, j))],
        out_specs=pl.BlockSpec((rows, tn), lambda j: (0, j)),
        compiler_params=_cparams(1),
        name="adaln",
    )(c_all, w_ada, b_ada.reshape(1, n))


def _ffn_kernel(*refs, nb, tl, sub, with_mix, with_final):
    if with_mix:
        x_ref, hm_ref, att_ref, mod_ref, gn_ref, wout_ref, wup_ref, wdn_ref, gfin_ref, o_ref, hb_ref, act_ref = refs
    else:
        x_ref, mod_ref, gn_ref, wup_ref, wdn_ref, gfin_ref, o_ref, hb_ref, act_ref = refs
    m = nb * tl
    d = x_ref.shape[-1]
    dff = wdn_ref.shape[0]
    mod = mod_ref[...]
    shift, scale, gate = (mod[:, 3 * sub + i:3 * sub + i + 1, :] for i in range(3))

    if with_mix:
        mix = _dot(hm_ref[...].reshape(m, D_M), wout_ref[0:D_M, :])
        mix = mix + _dot(att_ref[...].reshape(m, D_A), wout_ref[D_M:, :])
        o_ref[...] = x_ref[...] + mod[:, 5:6, :] * mix.reshape(nb, tl, d)
    else:
        o_ref[...] = x_ref[...]

    h = _norm_mod(o_ref[...], gn_ref[...], shift, scale)
    hb_ref[...] = h.reshape(m, d).astype(BF16)
    for c in range(dff // FF_CHUNK):
        lo = c * FF_CHUNK
        g = _dot(hb_ref[...], wup_ref[:, lo:lo + FF_CHUNK])
        u = _dot(hb_ref[...], wup_ref[:, dff + lo:dff + lo + FF_CHUNK])
        act_ref[:, lo:lo + FF_CHUNK] = (g * _sigmoid(g) * u).astype(BF16)
    dn = _dot(act_ref[...], wdn_ref[...])
    y = o_ref[...] + 0.5 * gate * dn.reshape(nb, tl, d)
    if with_final:
        ms = jnp.mean(y * y, axis=-1, keepdims=True)
        y = y * lax.rsqrt(ms + EPS) * gfin_ref[...]
    o_ref[...] = y


def _ffn(x, mod, g_norm_row, w_up, w_dn, g_final, *, nb, tl, sub, mix=None, final=False):
    NB, L, d = x.shape
    dff = w_dn.shape[0]
    m = nb * tl
    grid = (NB // nb, L // tl)
    row = lambda w: pl.BlockSpec((nb, tl, w), lambda i, j: (i, j, 0))
    in_specs = [row(d)]
    args = [x]
    if mix is not None:
        hm, att, w_out = mix
        in_specs += [row(D_M), row(D_A)]
        args += [hm, att]
    in_specs += [pl.BlockSpec((nb, N_MOD, d), lambda i, j: (i, 0, 0)), _const_spec((1, d))]
    args += [mod, g_norm_row]
    if mix is not None:
        in_specs.append(_const_spec(w_out.shape))
        args.append(w_out)
    in_specs += [_const_spec(w_up.shape), _const_spec(w_dn.shape), _const_spec((1, d))]
    args += [w_up, w_dn, g_final]
    kern = functools.partial(_ffn_kernel, nb=nb, tl=tl, sub=sub, with_mix=mix is not None,
                             with_final=final)
    return pl.pallas_call(
        kern,
        out_shape=jax.ShapeDtypeStruct((NB, L, d), F32),
        grid=grid,
        in_specs=in_specs,
        out_specs=row(d),
        scratch_shapes=[pltpu.VMEM((m, d), BF16), pltpu.VMEM((m, dff), BF16)],
        compiler_params=_cparams(2),
        name="ffn%d" % sub,
    )(*args)


def _proj_kernel(x_ref, mod_ref, gn_ref, w_ref, wg_ref, bg_ref, cw_ref, cb_ref, cprev_ref,
                 q_ref, k_ref, v_ref, o_ref, gates_ref, aq_ref, ak_ref, av_ref,
                 kt_ref, vt_ref, cnew_ref, hb_ref, ext_ref, *, nb, tl):
    m = nb * tl
    d = x_ref.shape[-1]
    j = pl.program_id(1)
    mod = mod_ref[...]
    h = _norm_mod(x_ref[...], gn_ref[...], mod[:, 3:4, :], mod[:, 4:5, :])
    hb_ref[...] = h.reshape(m, d).astype(BF16)

    @pl.when(j == 0)
    def _():
        ext_ref[:, 8 - (CONV_W - 1):8, :] = cprev_ref[...]

    for c in range(2 * D_M // FF_CHUNK):
        lo = c * FF_CHUNK
        pre = _dot(hb_ref[...], w_ref[:, lo:lo + FF_CHUNK])
        ext_ref[:, 8:8 + tl, lo:lo + FF_CHUNK] = pre.reshape(nb, tl, FF_CHUNK)
    for c in range(2 * D_M // FF_CHUNK):
        lo = c * FF_CHUNK
        acc = cb_ref[:, lo:lo + FF_CHUNK]
        for t in range(CONV_W):
            s0 = 8 - (CONV_W - 1) + t
            acc = acc + cw_ref[t:t + 1, lo:lo + FF_CHUNK] * ext_ref[:, s0:s0 + tl, lo:lo + FF_CHUNK]
        y = acc * _sigmoid(acc)
        if lo < D_M:
            q_ref[:, :, lo:lo + FF_CHUNK] = y.astype(BF16)
        else:
            k_ref[:, :, lo - D_M:lo - D_M + FF_CHUNK] = (y * DH_M ** -0.5).astype(BF16)
    cnew_ref[...] = ext_ref[:, 8 + tl - (CONV_W - 1):8 + tl, :]
    ext_ref[:, 0:8, :] = ext_ref[:, tl:tl + 8, :]

    def cols(c0):
        return _dot(hb_ref[...], w_ref[:, c0:c0 + D_M]).reshape(nb, tl, D_M)

    v_ref[...] = cols(2 * D_M).astype(BF16)
    o_ref[...] = cols(3 * D_M)
    aq_ref[...] = cols(4 * D_M).astype(BF16)
    ak = cols(4 * D_M + D_A)
    ak_ref[...] = ak.astype(BF16)
    kt_ref[...] = ak
    av = cols(4 * D_M + 2 * D_A)
    av_ref[...] = av.astype(BF16)
    vt_ref[...] = av
    gates_ref[...] = (_dot(hb_ref[...], wg_ref[...]) + bg_ref[...]).reshape(nb, tl, LANES)


def _proj(x, mod, g_norm_row, w_main, w_gates, b_gates, conv_w, conv_b, conv_prev, *, nb, tl):
    NB, L, d = x.shape
    assert tl == min(ATT_WINDOW, L)
    m = nb * tl
    grid = (NB // nb, L // tl)
    row = lambda w: pl.BlockSpec((nb, tl, w), lambda i, j: (i, j, 0))
    per_b = lambda r, w: pl.BlockSpec((nb, r, w), lambda i, j: (i, 0, 0))
    bshape = lambda w, dt: jax.ShapeDtypeStruct((NB, L, w), dt)
    in_specs = [row(d), per_b(N_MOD, d), _const_spec((1, d)), _const_spec(w_main.shape),
                _const_spec(w_gates.shape), _const_spec((1, LANES)), _const_spec(conv_w.shape),
                _const_spec((1, 2 * D_M)), per_b(CONV_W - 1, 2 * D_M)]
    out_shape = [bshape(D_M, BF16), bshape(D_M, BF16), bshape(D_M, BF16), bshape(D_M, F32),
                 bshape(LANES, F32), bshape(D_A, BF16), bshape(D_A, BF16), bshape(D_A, BF16),
                 jax.ShapeDtypeStruct((NB, tl, D_A), F32), jax.ShapeDtypeStruct((NB, tl, D_A), F32),
                 jax.ShapeDtypeStruct((NB, CONV_W - 1, 2 * D_M), F32)]
    out_specs = [row(D_M), row(D_M), row(D_M), row(D_M), row(LANES), row(D_A), row(D_A), row(D_A),
                 per_b(tl, D_A), per_b(tl, D_A), per_b(CONV_W - 1, 2 * D_M)]
    return pl.pallas_call(
        functools.partial(_proj_kernel, nb=nb, tl=tl),
        out_shape=out_shape,
        grid=grid,
        in_specs=in_specs,
        out_specs=out_specs,
        scratch_shapes=[pltpu.VMEM((m, d), BF16), pltpu.VMEM((nb, tl + 8, 2 * D_M), F32)],
        compiler_params=_cparams(2),
        name="proj",
    )(x, mod, g_norm_row, w_main, w_gates, b_gates, conv_w, conv_b, conv_prev)


def _mlstm_kernel(q_ref, k_ref, v_ref, gates_ref, mo_ref, c0_ref, n0_ref, m0_ref, gm_ref,
                  hm_ref, c_ref, n_ref, m_ref, *, lc):
    j = pl.program_id(1)

    @pl.when(j == 0)
    def _():
        c_ref[...] = c0_ref[...]
        n_ref[...] = n0_ref[...]
        m_ref[...] = m0_ref[...]

    gates = gates_ref[0]
    gates_t = jnp.transpose(gates)
    r_i = lax.broadcasted_iota(jnp.int32, (lc, lc), 0)
    c_i = lax.broadcasted_iota(jnp.int32, (lc, lc), 1)
    causal = r_i >= c_i
    lower = jnp.where(causal, 1.0, 0.0).astype(F32)
    upper = jnp.where(r_i <= c_i, 1.0, 0.0).astype(F32)
    hi = lax.Precision.HIGHEST
    b_col_all = jnp.dot(lower, _log_sigmoid(gates), precision=hi, preferred_element_type=F32)
    b_row_all = jnp.dot(_log_sigmoid(gates_t), upper, precision=hi, preferred_element_type=F32)

    for h in range(NH_M):
        sl = slice(h * DH_M, (h + 1) * DH_M)
        q = q_ref[0, :, sl]
        k = k_ref[0, :, sl]
        v = v_ref[0, :, sl]
        b_col = b_col_all[:, NH_M + h:NH_M + h + 1]
        b_row = b_row_all[NH_M + h:NH_M + h + 1, :]
        ig_col = gates[:, h:h + 1]
        ig_row = gates_t[h:h + 1, :]
        m_prev = m_ref[0, h][:, 0:1]
        c_prev = c_ref[0, h]
        n_prev = n_ref[0, h]

        log_w = jnp.where(causal, b_col - b_row + ig_row, NEG)
        log_inter = b_col + m_prev
        m_t = jnp.maximum(log_inter, jnp.max(log_w, axis=-1, keepdims=True))
        w_inter = jnp.exp(log_inter - m_t)
        a = jnp.exp(log_w - m_t) * _dot_nt(q, k)
        qf = q.astype(F32)
        num = w_inter * _dot(q, c_prev.astype(BF16)) + _dot(a.astype(BF16), v)
        den = w_inter * jnp.sum(qf * n_prev, axis=-1, keepdims=True) + jnp.sum(a, axis=-1, keepdims=True)
        hh = num / jnp.maximum(jnp.abs(den), jnp.exp(-m_t))

        b_end = b_col[lc - 1:lc, :]
        log_g_row = b_end - b_row + ig_row
        m_end = jnp.maximum(b_end + m_prev, jnp.max(log_g_row, axis=-1, keepdims=True))
        decay = jnp.exp(b_end + m_prev - m_end)
        g_col = jnp.exp(b_end - b_col + ig_col - m_end)
        kg = k.astype(F32) * g_col
        c_ref[0, h] = decay * c_prev + _dot_tn(kg.astype(BF16), v)
        n_ref[0, h] = decay * n_prev + jnp.sum(kg, axis=0, keepdims=True)
        m_ref[0, h] = jnp.broadcast_to(m_end, (1, LANES))

        hn = hh * lax.rsqrt(jnp.mean(hh * hh, axis=-1, keepdims=True) + EPS) * gm_ref[:, sl]
        hm_ref[0, :, sl] = (_sigmoid(mo_ref[0, :, sl]) * hn).astype(BF16)


def _mlstm(q, k, v, gates, mo, c0, n0, m0, g_mlstm_row, *, lc):
    NB, L, _ = q.shape
    grid = (NB, L // lc)
    row = lambda w: pl.BlockSpec((1, lc, w), lambda b, j: (b, j, 0))
    st = lambda s: pl.BlockSpec((1,) + s, lambda b, j: (b,) + (0,) * len(s))
    c_s, n_s = (NH_M, DH_M, DH_M), (NH_M, 1, DH_M)
    return pl.pallas_call(
        functools.partial(_mlstm_kernel, lc=lc),
        out_shape=[jax.ShapeDtypeStruct((NB, L, D_M), BF16),
                   jax.ShapeDtypeStruct((NB,) + c_s, F32),
                   jax.ShapeDtypeStruct((NB,) + n_s, F32),
                   jax.ShapeDtypeStruct((NB,) + n_s, F32)],
        grid=grid,
        in_specs=[row(D_M), row(D_M), row(D_M), row(LANES), row(D_M), st(c_s), st(n_s), st(n_s),
                  _const_spec((1, D_M))],
        out_specs=[row(D_M), st(c_s), st(n_s), st(n_s)],
        compiler_params=_cparams(2),
        name="mlstm",
    )(q, k, v, gates, mo, c0, n0, m0, g_mlstm_row)


def _build_band_bias(ext_ref, bias_ref, tq):
    klen = bias_ref.shape[-1]
    p_len = ext_ref.shape[-1]
    q_pos = lax.broadcasted_iota(jnp.int32, (tq, klen), 0) + (klen - tq)
    k_pos = lax.broadcasted_iota(jnp.int32, (tq, klen), 1)
    back = q_pos // CHUNK - k_pos // CHUNK
    for h in range(NH_A):
        base = jnp.broadcast_to(ext_ref[h:h + 1, :], (tq, p_len))
        toeplitz = pltpu.roll(base, 0, 1, stride=1, stride_axis=0)[:, :klen]
        banded = jnp.where(back >= 0, jnp.where(back <= LEFT_CHUNKS, toeplitz, NEG), NEG)
        bias_ref[h // 2, (h % 2) * tq:(h % 2 + 1) * tq, :] = banded


def _attn_tile(q, kvs, out_ref_set):
    tq = q.shape[0]
    lane = lax.broadcasted_iota(jnp.int32, (1, LANES), 1)
    first = lane < DH_A
    s0 = jnp.where(first, DH_A ** -0.5, 0.0).astype(BF16)
    s1 = jnp.where(first, 0.0, DH_A ** -0.5).astype(BF16)
    for p in range(NH_A // 2):
        sl = slice(p * LANES, (p + 1) * LANES)
        qp = q[:, sl]
        q2 = jnp.concatenate([qp * s0, qp * s1], axis=0)
        ss = []
        for k, _, bias, pen in kvs:
            s = _dot_nt(q2, k[:, sl]) + bias(p)
            if pen is not None:
                s = s + pen
            ss.append(s)
        mx = functools.reduce(jnp.maximum, [jnp.max(s, axis=-1, keepdims=True) for s in ss])
        l = 0.0
        o2 = 0.0
        for s, (_, v, _, _) in zip(ss, kvs):
            e = jnp.exp(s - mx)
            l = l + jnp.sum(e, axis=-1, keepdims=True)
            o2 = o2 + _dot(e.astype(BF16), v[:, sl])
        o2 = o2 / l
        out_ref_set(sl, jnp.where(first, o2[:tq], o2[tq:]))


def _attn_prompt_kernel(q_ref, k0_ref, k1_ref, k2_ref, v0_ref, v1_ref, v2_ref, ext_ref, o_ref,
                        bias_ref):
    tq = q_ref.shape[1]
    j = pl.program_id(1)

    @pl.when((pl.program_id(0) == 0) & (j == 0))
    def _():
        _build_band_bias(ext_ref, bias_ref, tq)

    pen0 = jnp.where(j >= 2, 0.0, NEG).astype(F32)
    pen1 = jnp.where(j >= 1, 0.0, NEG).astype(F32)
    bias = lambda i: (lambda p: bias_ref[p, :, i * tq:(i + 1) * tq])

    def put(sl, val):
        o_ref[0, :, sl] = val.astype(o_ref.dtype)

    _attn_tile(q_ref[0],
               [(k0_ref[0], v0_ref[0], bias(0), pen0), (k1_ref[0], v1_ref[0], bias(1), pen1),
                (k2_ref[0], v2_ref[0], bias(2), None)], put)


def _attn_prompt(aq, ak, av, ext):
    NB, L, _ = aq.shape
    tq = ATT_TQ
    grid = (NB, L // tq)
    row = pl.BlockSpec((1, tq, D_A), lambda b, j: (b, j, 0))
    back = lambda n: pl.BlockSpec((1, tq, D_A), lambda b, j: (b, jnp.maximum(j - n, 0), 0))
    return pl.pallas_call(
        _attn_prompt_kernel,
        out_shape=jax.ShapeDtypeStruct((NB, L, D_A), BF16),
        grid=grid,
        in_specs=[row, back(2), back(1), row, back(2), back(1), row, _const_spec(ext.shape)],
        out_specs=row,
        scratch_shapes=[pltpu.VMEM((NH_A // 2, 2 * tq, 3 * tq), F32)],
        compiler_params=_cparams(2),
        name="attn_prompt",
    )(aq, ak, ak, ak, av, av, av, ext)


def _attn_sample_kernel(q_ref, k_ref, v_ref, ck_ref, cv_ref, ext_ref, o_ref, bias_ref, *, nb):
    tq = q_ref.shape[1]
    nc = ck_ref.shape[1]

    @pl.when(pl.program_id(0) == 0)
    def _():
        _build_band_bias(ext_ref, bias_ref, tq)

    for b in range(nb):
        def put(sl, val, b=b):
            o_ref[b, :, sl] = val.astype(o_ref.dtype)

        _attn_tile(q_ref[b],
                   [(ck_ref[b].astype(BF16), cv_ref[b].astype(BF16),
                     lambda p: bias_ref[p, :, 0:nc], None),
                    (k_ref[b], v_ref[b], lambda p: bias_ref[p, :, nc:nc + tq], None)], put)


def _attn_sample(aq, ak, av, cache_k, cache_v, ext, *, nb):
    NB, L, _ = aq.shape
    nc = cache_k.shape[1]
    row = pl.BlockSpec((nb, L, D_A), lambda i: (i, 0, 0))
    crow = pl.BlockSpec((nb, nc, D_A), lambda i: (i, 0, 0))
    return pl.pallas_call(
        functools.partial(_attn_sample_kernel, nb=nb),
        out_shape=jax.ShapeDtypeStruct((NB, L, D_A), BF16),
        grid=(NB // nb,),
        in_specs=[row, row, row, crow, crow, _const_spec(ext.shape)],
        out_specs=row,
        scratch_shapes=[pltpu.VMEM((NH_A // 2, 2 * L, nc + L), F32)],
        compiler_params=_cparams(1),
        name="attn_sample",
    )(aq, ak, av, cache_k, cache_v, ext)


def _rel_ext(rel_table, tq, klen):
    p_len = -(-(klen + tq) // LANES) * LANES
    n_lo = tq - CHUNK
    n_hi = p_len - tq - MAX_REL
    rep = lambda col, n: jnp.broadcast_to(col, (NH_A, n))
    by_dist = jnp.concatenate([rep(rel_table[:, :1], n_lo), rel_table, rep(rel_table[:, -1:], n_hi)],
                              axis=1)
    rev = by_dist[:, ::-1]
    n = p_len - klen
    return jnp.concatenate([rev[:, n:], rev[:, :n]], axis=1).astype(F32)


def _layer(x, mod, conv_prev, state, cache, w, *, nb, tl, lc, final):
    NB, L, d = x.shape
    x1 = _ffn(x, mod, w["g0"], w["up1"], w["dn1"], w["g_final"], nb=nb, tl=tl, sub=0)
    (q, k, v, mo, gates, aq, ak, av, k_tail, v_tail, conv_new) = _proj(
        x1, mod, w["g1"], w["w_main"], w["w_gates"], w["b_gates"], w["conv_w"], w["conv_b"],
        conv_prev, nb=nb, tl=tl)
    c0, n0, m0 = state
    hm, c_new, n_new, m_new = _mlstm(q, k, v, gates, mo, c0, n0, m0, w["g_mlstm"], lc=lc)
    if cache is None:
        att = _attn_prompt(aq, ak, av, w["ext_prompt"])
    else:
        att = _attn_sample(aq, ak, av, cache[0], cache[1], w["ext_sample"], nb=nb)
    y = _ffn(x1, mod, w["g2"], w["up2"], w["dn2"], w["g_final"], nb=nb, tl=tl, sub=2,
             mix=(hm, att, w["w_out"]), final=final)
    n_keep = k_tail.shape[1]
    states = (c_new, n_new[:, :, 0, :], m_new[:, :, 0, 0], conv_new,
              k_tail.reshape(NB, n_keep, NH_A, DH_A), v_tail.reshape(NB, n_keep, NH_A, DH_A))
    return y, states


def kernel(x_prompt, x_sample, state_mlstm_C, state_mlstm_n, state_mlstm_m, state_conv, cache_att_k, cache_att_v, c_prompt, c_sample, w_ada, b_ada, g_norm, w_ffn1_up, w_ffn1_down, w_ffn2_up, w_ffn2_down, w_in, conv_w, conv_b, b_gates, g_mlstm, rel_bias_table, w_out, g_final):
    depth = w_ada.shape[0]
    bp, seq, d = x_prompt.shape
    bs, dseq, _ = x_sample.shape
    xp, xs = x_prompt, x_sample
    st_p, st_s = [], []
    n_c = cache_att_k.shape[2]
    tl_p = min(ATT_WINDOW, seq)
    lc_p = min(MLSTM_CHUNK, seq)
    nb_s = max(1, min(bs, ATT_WINDOW // dseq))
    for l in range(depth):
        rows = bp + bs
        pad = (-rows) % 8
        c_all = jnp.concatenate([c_prompt, c_sample, jnp.zeros((pad, d), F32)], axis=0)
        mod = _adaln(c_all, w_ada[l], b_ada[l])
        mod_p = mod[:bp].reshape(bp, N_MOD, d)
        mod_s = mod[bp:rows].reshape(bs, N_MOD, d)
        off_g = 4 * D_M
        off_a = off_g + 2 * NH_M
        wl = w_in[l]
        rel = rel_bias_table[l]
        w = {
            "g0": g_norm[l, 0:1], "g1": g_norm[l, 1:2], "g2": g_norm[l, 2:3],
            "g_final": g_final.reshape(1, d),
            "up1": w_ffn1_up[l].astype(BF16), "dn1": w_ffn1_down[l].astype(BF16),
            "up2": w_ffn2_up[l].astype(BF16), "dn2": w_ffn2_down[l].astype(BF16),
            "w_main": jnp.concatenate([wl[:, :off_g], wl[:, off_a:]], axis=1).astype(BF16),
            "w_gates": jnp.pad(wl[:, off_g:off_a], ((0, 0), (0, LANES - 2 * NH_M))).astype(BF16),
            "b_gates": jnp.pad(b_gates[l], (0, LANES - 2 * NH_M)).reshape(1, LANES),
            "conv_w": conv_w[l], "conv_b": conv_b[l].reshape(1, 2 * D_M),
            "g_mlstm": g_mlstm[l].reshape(1, D_M),
            "w_out": w_out[l].astype(BF16),
            "ext_prompt": _rel_ext(rel, ATT_TQ, 3 * ATT_TQ),
            "ext_sample": _rel_ext(rel, dseq, n_c + dseq),
        }
        zero_state = (jnp.zeros((bp, NH_M, DH_M, DH_M), F32), jnp.zeros((bp, NH_M, 1, DH_M), F32),
                      jnp.zeros((bp, NH_M, 1, LANES), F32))
        xp, sp = _layer(xp, mod_p, jnp.zeros((bp, CONV_W - 1, 2 * D_M), F32), zero_state, None, w,
                        nb=1, tl=tl_p, lc=lc_p, final=l == depth - 1)
        state_s = (state_mlstm_C[l], state_mlstm_n[l][:, :, None, :],
                   jnp.broadcast_to(state_mlstm_m[l][:, :, None, None], (bs, NH_M, 1, LANES)))
        cache = (cache_att_k[l].reshape(bs, n_c, D_A), cache_att_v[l].reshape(bs, n_c, D_A))
        xs, ss = _layer(xs, mod_s, state_conv[l], state_s, cache, w, nb=nb_s, tl=dseq, lc=dseq,
                        final=l == depth - 1)
        st_p.append(sp)
        st_s.append(ss)
    stk = lambda sts, i: jnp.stack([s[i] for s in sts])
    return ((xp, xs) + tuple(stk(st_p, i) for i in range(6)) + tuple(stk(st_s, i) for i in range(6)))
```

```python
import functools

import jax
import jax.numpy as jnp
from jax import lax
from jax.experimental import pallas as pl
from jax.experimental.pallas import tpu as pltpu

F32 = jnp.float32
BF16 = jnp.bfloat16

CHUNK = 64
NH_M = 4
DH_M = 128
D_M = NH_M * DH_M
NH_A = 8
DH_A = 64
D_A = NH_A * DH_A
CONV_W = 4
LEFT_CHUNKS = 8
ATT_WINDOW = LEFT_CHUNKS * CHUNK
MAX_REL = 2 * CHUNK
N_MOD = 9
EPS = 1e-6
NEG = -1e30

LANES = 128
V7X_VMEM_BYTES = 64 * 1024 * 1024
VMEM_LIMIT = V7X_VMEM_BYTES - 8 * 1024 * 1024

FF_CHUNK = 256
MLSTM_CHUNK = 256
ATT_TQ = 256


def _cparams(n_axes):
    return pltpu.CompilerParams(dimension_semantics=("arbitrary",) * n_axes,
                                vmem_limit_bytes=VMEM_LIMIT)


def _const_spec(shape):
    nd = len(shape)
    return pl.BlockSpec(shape, lambda *_: (0,) * nd, pipeline_mode=pl.Buffered(1))


def _dot(a, b):
    return jnp.dot(a, b, preferred_element_type=F32)


def _dot_nt(a, b):
    return lax.dot_general(a, b, (((1,), (1,)), ((), ())), preferred_element_type=F32)


def _dot_tn(a, b):
    return lax.dot_general(a, b, (((0,), (0,)), ((), ())), preferred_element_type=F32)


def _sigmoid(x):
    return 1.0 / (1.0 + jnp.exp(-x))


def _log_sigmoid(x):
    return jnp.minimum(x, 0.0) - jnp.log(1.0 + jnp.exp(-jnp.abs(x)))


def _norm_mod(x, g, shift, scale):
    ms = jnp.mean(x * x, axis=-1, keepdims=True)
    h = x * lax.rsqrt(ms + EPS) * g
    return h * (1.0 + scale) + shift


def _adaln_kernel(c_ref, w_ref, b_ref, o_ref):
    c = c_ref[...]
    o_ref[...] = _dot(c * _sigmoid(c), w_ref[...]) + b_ref[...]


def _adaln(c_all, w_ada, b_ada):
    rows, d = c_all.shape
    n = w_ada.shape[1]
    tn = d
    return pl.pallas_call(
        _adaln_kernel,
        out_shape=jax.ShapeDtypeStruct((rows, n), F32),
        grid=(n // tn,),
        in_specs=[pl.BlockSpec((rows, d), lambda j: (0, 0)),
                  pl.BlockSpec((d, tn), lambda j: (0, j)),
                  pl.BlockSpec((1, tn), lambda j: (0, j))],
        out_specs=pl.BlockSpec((rows, tn), lambda j: (0, j)),
        compiler_params=_cparams(1),
        name="adaln",
    )(c_all, w_ada, b_ada.reshape(1, n))


def _ffn_kernel(*refs, nb, tl, sub, with_mix, with_final):
    if with_mix:
        x_ref, hm_ref, att_ref, mod_ref, gn_ref, wout_ref, wup_ref, wdn_ref, gfin_ref, o_ref, hb_ref, act_ref = refs
    else:
        x_ref, mod_ref, gn_ref, wup_ref, wdn_ref, gfin_ref, o_ref, hb_ref, act_ref = refs
    m = nb * tl
    d = x_ref.shape[-1]
    dff = wdn_ref.shape[0]
    mod = mod_ref[...]
    shift, scale, gate = (mod[:, 3 * sub + i:3 * sub + i + 1, :] for i in range(3))

    if with_mix:
        mix = _dot(hm_ref[...].reshape(m, D_M), wout_ref[0:D_M, :])
        mix = mix + _dot(att_ref[...].reshape(m, D_A), wout_ref[D_M:, :])
        o_ref[...] = x_ref[...] + mod[:, 5:6, :] * mix.reshape(nb, tl, d)
    else:
        o_ref[...] = x_ref[...]

    h = _norm_mod(o_ref[...], gn_ref[...], shift, scale)
    hb_ref[...] = h.reshape(m, d).astype(BF16)
    for c in range(dff // FF_CHUNK):
        lo = c * FF_CHUNK
        g = _dot(hb_ref[...], wup_ref[:, lo:lo + FF_CHUNK])
        u = _dot(hb_ref[...], wup_ref[:, dff + lo:dff + lo + FF_CHUNK])
        act_ref[:, lo:lo + FF_CHUNK] = (g * _sigmoid(g) * u).astype(BF16)
    dn = _dot(act_ref[...], wdn_ref[...])
    y = o_ref[...] + 0.5 * gate * dn.reshape(nb, tl, d)
    if with_final:
        ms = jnp.mean(y * y, axis=-1, keepdims=True)
        y = y * lax.rsqrt(ms + EPS) * gfin_ref[...]
    o_ref[...] = y


def _ffn(x, mod, g_norm_row, w_up, w_dn, g_final, *, nb, tl, sub, mix=None, final=False):
    NB, L, d = x.shape
    dff = w_dn.shape[0]
    m = nb * tl
    grid = (NB // nb, L // tl)
    row = lambda w: pl.BlockSpec((nb, tl, w), lambda i, j: (i, j, 0))
    in_specs = [row(d)]
    args = [x]
    if mix is not None:
        hm, att, w_out = mix
        in_specs += [row(D_M), row(D_A)]
        args += [hm, att]
    in_specs += [pl.BlockSpec((nb, N_MOD, d), lambda i, j: (i, 0, 0)), _const_spec((1, d))]
    args += [mod, g_norm_row]
    if mix is not None:
        in_specs.append(_const_spec(w_out.shape))
        args.append(w_out)
    in_specs += [_const_spec(w_up.shape), _const_spec(w_dn.shape), _const_spec((1, d))]
    args += [w_up, w_dn, g_final]
    kern = functools.partial(_ffn_kernel, nb=nb, tl=tl, sub=sub, with_mix=mix is not None,
                             with_final=final)
    return pl.pallas_call(
        kern,
        out_shape=jax.ShapeDtypeStruct((NB, L, d), F32),
        grid=grid,
        in_specs=in_specs,
        out_specs=row(d),
        scratch_shapes=[pltpu.VMEM((m, d), BF16), pltpu.VMEM((m, dff), BF16)],
        compiler_params=_cparams(2),
        name="ffn%d" % sub,
    )(*args)


def _proj_kernel(x_ref, mod_ref, gn_ref, w_ref, wg_ref, bg_ref, cw_ref, cb_ref, cprev_ref,
                 q_ref, k_ref, v_ref, o_ref, gates_ref, aq_ref, ak_ref, av_ref,
                 kt_ref, vt_ref, cnew_ref, hb_ref, ext_ref, *, nb, tl):
    m = nb * tl
    d = x_ref.shape[-1]
    j = pl.program_id(1)
    mod = mod_ref[...]
    h = _norm_mod(x_ref[...], gn_ref[...], mod[:, 3:4, :], mod[:, 4:5, :])
    hb_ref[...] = h.reshape(m, d).astype(BF16)

    @pl.when(j == 0)
    def _():
        ext_ref[:, 8 - (CONV_W - 1):8, :] = cprev_ref[...]

    for c in range(2 * D_M // FF_CHUNK):
        lo = c * FF_CHUNK
        pre = _dot(hb_ref[...], w_ref[:, lo:lo + FF_CHUNK])
        ext_ref[:, 8:8 + tl, lo:lo + FF_CHUNK] = pre.reshape(nb, tl, FF_CHUNK)
    for c in range(2 * D_M // FF_CHUNK):
        lo = c * FF_CHUNK
        acc = cb_ref[:, lo:lo + FF_CHUNK]
        for t in range(CONV_W):
            s0 = 8 - (CONV_W - 1) + t
            acc = acc + cw_ref[t:t + 1, lo:lo + FF_CHUNK] * ext_ref[:, s0:s0 + tl, lo:lo + FF_CHUNK]
        y = acc * _sigmoid(acc)
        if lo < D_M:
            q_ref[:, :, lo:lo + FF_CHUNK] = y.astype(BF16)
        else:
            k_ref[:, :, lo - D_M:lo - D_M + FF_CHUNK] = (y * DH_M ** -0.5).astype(BF16)
    cnew_ref[...] = ext_ref[:, 8 + tl - (CONV_W - 1):8 + tl, :]
    ext_ref[:, 0:8, :] = ext_ref[:, tl:tl + 8, :]

    def cols(c0):
        return _dot(hb_ref[...], w_ref[:, c0:c0 + D_M]).reshape(nb, tl, D_M)

    v_ref[...] = cols(2 * D_M).astype(BF16)
    o_ref[...] = cols(3 * D_M)
    aq_ref[...] = cols(4 * D_M).astype(BF16)
    ak = cols(4 * D_M + D_A)
    ak_ref[...] = ak.astype(BF16)
    kt_ref[...] = ak
    av = cols(4 * D_M + 2 * D_A)
    av_ref[...] = av.astype(BF16)
    vt_ref[...] = av
    gates_ref[...] = (_dot(hb_ref[...], wg_ref[...]) + bg_ref[...]).reshape(nb, tl, LANES)


def _proj(x, mod, g_norm_row, w_main, w_gates, b_gates, conv_w, conv_b, conv_prev, *, nb, tl):
    NB, L, d = x.shape
    assert tl == min(ATT_WINDOW, L)
    m = nb * tl
    grid = (NB // nb, L // tl)
    row = lambda w: pl.BlockSpec((nb, tl, w), lambda i, j: (i, j, 0))
    per_b = lambda r, w: pl.BlockSpec((nb, r, w), lambda i, j: (i, 0, 0))
    bshape = lambda w, dt: jax.ShapeDtypeStruct((NB, L, w), dt)
    in_specs = [row(d), per_b(N_MOD, d), _const_spec((1, d)), _const_spec(w_main.shape),
                _const_spec(w_gates.shape), _const_spec((1, LANES)), _const_spec(conv_w.shape),
                _const_spec((1, 2 * D_M)), per_b(CONV_W - 1, 2 * D_M)]
    out_shape = [bshape(D_M, BF16), bshape(D_M, BF16), bshape(D_M, BF16), bshape(D_M, F32),
                 bshape(LANES, F32), bshape(D_A, BF16), bshape(D_A, BF16), bshape(D_A, BF16),
                 jax.ShapeDtypeStruct((NB, tl, D_A), F32), jax.ShapeDtypeStruct((NB, tl, D_A), F32),
                 jax.ShapeDtypeStruct((NB, CONV_W - 1, 2 * D_M), F32)]
    out_specs = [row(D_M), row(D_M), row(D_M), row(D_M), row(LANES), row(D_A), row(D_A), row(D_A),
                 per_b(tl, D_A), per_b(tl, D_A), per_b(CONV_W - 1, 2 * D_M)]
    return pl.pallas_call(
        functools.partial(_proj_kernel, nb=nb, tl=tl),
        out_shape=out_shape,
        grid=grid,
        in_specs=in_specs,
        out_specs=out_specs,
        scratch_shapes=[pltpu.VMEM((m, d), BF16), pltpu.VMEM((nb, tl + 8, 2 * D_M), F32)],
        compiler_params=_cparams(2),
        name="proj",
    )(x, mod, g_norm_row, w_main, w_gates, b_gates, conv_w, conv_b, conv_prev)


LOG2E = 1.4426950408889634


def _prefix_max_lanes(x):
    n = x.shape[-1]
    lane = lax.broadcasted_iota(jnp.int32, x.shape, 1)
    shift = 1
    while shift < n:
        x = jnp.maximum(x, jnp.where(lane >= shift, pltpu.roll(x, shift, 1), NEG))
        shift *= 2
    return x


def _split_bf16(x):
    hi = x.astype(BF16)
    r = x - hi.astype(F32)
    mid = r.astype(BF16)
    return hi, mid, (r - mid.astype(F32)).astype(BF16)


def _mlstm_kernel(q_ref, k_ref, v_ref, gates_ref, mo_ref, c0_ref, n0_ref, m0_ref, gm_ref,
                  hm_ref, c_ref, n_ref, m_ref, ct_ref, *, nbm, lc):
    j = pl.program_id(1)

    def transpose_states(src_ref, dst_ref):
        def one(i, carry):
            dst_ref[i // NH_M, i % NH_M] = jnp.transpose(src_ref[i // NH_M, i % NH_M])
            return carry
        lax.fori_loop(0, nbm * NH_M, one, 0)

    @pl.when(j == 0)
    def _():
        transpose_states(c0_ref, ct_ref)
        n_ref[...] = n0_ref[...]
        m_ref[...] = m0_ref[...]

    s_i = lax.broadcasted_iota(jnp.int32, (lc, lc), 0)
    t_i = lax.broadcasted_iota(jnp.int32, (lc, lc), 1)
    causal = s_i <= t_i
    upper = jnp.where(causal, 1.0, 0.0).astype(BF16)
    head_row = lax.broadcasted_iota(jnp.int32, (8, 1), 0)

    pairs = [(b, h) for b in range(nbm) for h in range(NH_M)]
    hsl = lambda h: slice(h * DH_M, (h + 1) * DH_M)

    rows = []
    for b in range(nbm):
        gates_t = jnp.transpose(gates_ref[b])
        ig_t = gates_t[0:8]
        lf_t = _log_sigmoid(gates_t[8:16])
        b_t = sum(_dot(part, upper) for part in _split_bf16(lf_t))
        c_t = ig_t - b_t
        b_end = b_t[:, lc - 1:lc]
        m_prev = m_ref[b][:, 0:1]
        log_g = b_end + c_t
        m_end = jnp.maximum(b_end + m_prev, jnp.max(log_g, axis=-1, keepdims=True))
        decay = jnp.exp(b_end + m_prev - m_end)
        g_rows = jnp.exp(log_g - m_end)
        log_inter = b_t + m_prev
        c2_t = c_t * LOG2E
        c2_cols = jnp.transpose(jnp.concatenate([c2_t, jnp.zeros((LANES - 8, lc), F32)], axis=0))
        if lc % LANES == 0:
            cmax2_t = _prefix_max_lanes(c2_t)
        else:
            cmax2_t = jnp.concatenate(
                [jnp.max(jnp.where(causal, c2_cols[:, h:h + 1], NEG), axis=0, keepdims=True)
                 for h in range(8)], axis=0)
        m_intra = b_t + cmax2_t * (1.0 / LOG2E)
        m_tok = jnp.maximum(log_inter, m_intra)
        rows.append(dict(decay=decay, g=g_rows, g_b=g_rows.astype(BF16), m_end=m_end,
                         c2_cols=c2_cols, cmax2=cmax2_t, r_intra=jnp.exp(m_intra - m_tok),
                         w_inter=jnp.exp(log_inter - m_tok), floor=jnp.exp(-m_tok),
                         n_prev_b=n_ref[b].astype(BF16)))

    early = {}
    for b, h in pairs:
        q = q_ref[b, :, hsl(h)]
        k = k_ref[b, :, hsl(h)]
        early[b, h] = dict(
            s_t=_dot_nt(k, q),
            cq_t=_dot_nt(ct_ref[b, h].astype(BF16), q),
            nq=_dot_nt(rows[b]["n_prev_b"], q)[h:h + 1, :],
            v_t=jnp.transpose(v_ref[b, :, hsl(h)].astype(F32)))

    n_upd = [jnp.zeros((8, DH_M), F32) for _ in range(nbm)]
    for b, h in pairs:
        r, e = rows[b], early[b, h]
        k = k_ref[b, :, hsl(h)]
        d_t = jnp.exp2(r["c2_cols"][:, h:h + 1] - r["cmax2"][h:h + 1, :])
        a_t = jnp.where(causal, d_t, 0.0) * e["s_t"]
        r_intra, w_inter = r["r_intra"][h:h + 1, :], r["w_inter"][h:h + 1, :]
        num_t = _dot(e["v_t"].astype(BF16), a_t.astype(BF16)) * r_intra + e["cq_t"] * w_inter
        den = jnp.sum(a_t, axis=0, keepdims=True) * r_intra + e["nq"] * w_inter
        h_t = num_t * (1.0 / jnp.maximum(jnp.abs(den), r["floor"][h:h + 1, :]))
        hn_t = h_t * lax.rsqrt(jnp.mean(h_t * h_t, axis=0, keepdims=True) + EPS)
        hn = jnp.transpose(hn_t) * gm_ref[:, hsl(h)]
        hm_ref[b, :, hsl(h)] = (_sigmoid(mo_ref[b, :, hsl(h)]) * hn).astype(BF16)

        vg_t = (e["v_t"] * r["g"][h:h + 1, :]).astype(BF16)
        ct_ref[b, h] = r["decay"][h:h + 1, :] * ct_ref[b, h] + _dot(vg_t, k)
        n_upd[b] = n_upd[b] + jnp.where(head_row == h, _dot(r["g_b"], k), 0.0)

    for b in range(nbm):
        n_ref[b] = rows[b]["decay"] * n_ref[b] + n_upd[b]
        m_ref[b] = jnp.broadcast_to(rows[b]["m_end"], (8, LANES))

    @pl.when(j == pl.num_programs(1) - 1)
    def _():
        transpose_states(ct_ref, c_ref)


def _mlstm(q, k, v, gates, mo, c0, n0, m0, g_mlstm_row, *, nbm, lc):
    NB, L, _ = q.shape
    grid = (NB // nbm, L // lc)
    row = lambda w: pl.BlockSpec((nbm, lc, w), lambda i, j: (i, j, 0))
    st = lambda s: pl.BlockSpec((nbm,) + s, lambda i, j: (i,) + (0,) * len(s))
    c_s, n_s = (NH_M, DH_M, DH_M), (8, DH_M)
    return pl.pallas_call(
        functools.partial(_mlstm_kernel, nbm=nbm, lc=lc),
        out_shape=[jax.ShapeDtypeStruct((NB, L, D_M), BF16),
                   jax.ShapeDtypeStruct((NB,) + c_s, F32),
                   jax.ShapeDtypeStruct((NB,) + n_s, F32),
                   jax.ShapeDtypeStruct((NB,) + n_s, F32)],
        grid=grid,
        in_specs=[row(D_M), row(D_M), row(D_M), row(LANES), row(D_M), st(c_s), st(n_s), st(n_s),
                  _const_spec((1, D_M))],
        out_specs=[row(D_M), st(c_s), st(n_s), st(n_s)],
        scratch_shapes=[pltpu.VMEM((nbm,) + c_s, F32)],
        compiler_params=_cparams(2),
        name="mlstm",
    )(q, k, v, gates, mo, c0, n0, m0, g_mlstm_row)


def _build_band_bias(ext_ref, bias_ref, tq):
    klen = bias_ref.shape[-1]
    p_len = ext_ref.shape[-1]
    q_pos = lax.broadcasted_iota(jnp.int32, (tq, klen), 0) + (klen - tq)
    k_pos = lax.broadcasted_iota(jnp.int32, (tq, klen), 1)
    back = q_pos // CHUNK - k_pos // CHUNK
    for h in range(NH_A):
        base = jnp.broadcast_to(ext_ref[h:h + 1, :], (tq, p_len))
        toeplitz = pltpu.roll(base, 0, 1, stride=1, stride_axis=0)[:, :klen]
        banded = jnp.where(back >= 0, jnp.where(back <= LEFT_CHUNKS, toeplitz, NEG), NEG)
        bias_ref[h // 2, (h % 2) * tq:(h % 2 + 1) * tq, :] = banded


def _attn_tile(q, kvs, out_ref_set):
    tq = q.shape[0]
    lane = lax.broadcasted_iota(jnp.int32, (1, LANES), 1)
    first = lane < DH_A
    s0 = jnp.where(first, DH_A ** -0.5, 0.0).astype(BF16)
    s1 = jnp.where(first, 0.0, DH_A ** -0.5).astype(BF16)
    for p in range(NH_A // 2):
        sl = slice(p * LANES, (p + 1) * LANES)
        qp = q[:, sl]
        q2 = jnp.concatenate([qp * s0, qp * s1], axis=0)
        ss = []
        for k, _, bias, pen in kvs:
            s = _dot_nt(q2, k[:, sl]) + bias(p)
            if pen is not None:
                s = s + pen
            ss.append(s)
        mx = functools.reduce(jnp.maximum, [jnp.max(s, axis=-1, keepdims=True) for s in ss])
        l = 0.0
        o2 = 0.0
        for s, (_, v, _, _) in zip(ss, kvs):
            e = jnp.exp(s - mx)
            l = l + jnp.sum(e, axis=-1, keepdims=True)
            o2 = o2 + _dot(e.astype(BF16), v[:, sl])
        o2 = o2 / l
        out_ref_set(sl, jnp.where(first, o2[:tq], o2[tq:]))


def _attn_prompt_kernel(q_ref, k0_ref, k1_ref, k2_ref, v0_ref, v1_ref, v2_ref, ext_ref, o_ref,
                        bias_ref):
    tq = q_ref.shape[1]
    j = pl.program_id(1)

    @pl.when((pl.program_id(0) == 0) & (j == 0))
    def _():
        _build_band_bias(ext_ref, bias_ref, tq)

    pen0 = jnp.where(j >= 2, 0.0, NEG).astype(F32)
    pen1 = jnp.where(j >= 1, 0.0, NEG).astype(F32)
    bias = lambda i: (lambda p: bias_ref[p, :, i * tq:(i + 1) * tq])

    def put(sl, val):
        o_ref[0, :, sl] = val.astype(o_ref.dtype)

    _attn_tile(q_ref[0],
               [(k0_ref[0], v0_ref[0], bias(0), pen0), (k1_ref[0], v1_ref[0], bias(1), pen1),
                (k2_ref[0], v2_ref[0], bias(2), None)], put)


def _attn_prompt(aq, ak, av, ext):
    NB, L, _ = aq.shape
    tq = ATT_TQ
    grid = (NB, L // tq)
    row = pl.BlockSpec((1, tq, D_A), lambda b, j: (b, j, 0))
    back = lambda n: pl.BlockSpec((1, tq, D_A), lambda b, j: (b, jnp.maximum(j - n, 0), 0))
    return pl.pallas_call(
        _attn_prompt_kernel,
        out_shape=jax.ShapeDtypeStruct((NB, L, D_A), BF16),
        grid=grid,
        in_specs=[row, back(2), back(1), row, back(2), back(1), row, _const_spec(ext.shape)],
        out_specs=row,
        scratch_shapes=[pltpu.VMEM((NH_A // 2, 2 * tq, 3 * tq), F32)],
        compiler_params=_cparams(2),
        name="attn_prompt",
    )(aq, ak, ak, ak, av, av, av, ext)


def _attn_sample_kernel(q_ref, k_ref, v_ref, ck_ref, cv_ref, ext_ref, o_ref, bias_ref, *, nb):
    tq = q_ref.shape[1]
    nc = ck_ref.shape[1]

    @pl.when(pl.program_id(0) == 0)
    def _():
        _build_band_bias(ext_ref, bias_ref, tq)

    for b in range(nb):
        def put(sl, val, b=b):
            o_ref[b, :, sl] = val.astype(o_ref.dtype)

        _attn_tile(q_ref[b],
                   [(ck_ref[b].astype(BF16), cv_ref[b].astype(BF16),
                     lambda p: bias_ref[p, :, 0:nc], None),
                    (k_ref[b], v_ref[b], lambda p: bias_ref[p, :, nc:nc + tq], None)], put)


def _attn_sample(aq, ak, av, cache_k, cache_v, ext, *, nb):
    NB, L, _ = aq.shape
    nc = cache_k.shape[1]
    row = pl.BlockSpec((nb, L, D_A), lambda i: (i, 0, 0))
    crow = pl.BlockSpec((nb, nc, D_A), lambda i: (i, 0, 0))
    return pl.pallas_call(
        functools.partial(_attn_sample_kernel, nb=nb),
        out_shape=jax.ShapeDtypeStruct((NB, L, D_A), BF16),
        grid=(NB // nb,),
        in_specs=[row, row, row, crow, crow, _const_spec(ext.shape)],
        out_specs=row,
        scratch_shapes=[pltpu.VMEM((NH_A // 2, 2 * L, nc + L), F32)],
        compiler_params=_cparams(1),
        name="attn_sample",
    )(aq, ak, av, cache_k, cache_v, ext)


def _rel_ext(rel_table, tq, klen):
    p_len = -(-(klen + tq) // LANES) * LANES
    n_lo = tq - CHUNK
    n_hi = p_len - tq - MAX_REL
    rep = lambda col, n: jnp.broadcast_to(col, (NH_A, n))
    by_dist = jnp.concatenate([rep(rel_table[:, :1], n_lo), rel_table, rep(rel_table[:, -1:], n_hi)],
                              axis=1)
    rev = by_dist[:, ::-1]
    n = p_len - klen
    return jnp.concatenate([rev[:, n:], rev[:, :n]], axis=1).astype(F32)


def _gate_lanes(g):
    z = lambda n: jnp.zeros((g.shape[0], n), g.dtype)
    return jnp.concatenate([g[:, :NH_M], z(8 - NH_M), g[:, NH_M:], z(LANES - 8 - NH_M)], axis=1)


def _layer(x, mod, conv_prev, state, cache, w, *, nb, tl, nbm, lc, final):
    NB, L, d = x.shape
    x1 = _ffn(x, mod, w["g0"], w["up1"], w["dn1"], w["g_final"], nb=nb, tl=tl, sub=0)
    (q, k, v, mo, gates, aq, ak, av, k_tail, v_tail, conv_new) = _proj(
        x1, mod, w["g1"], w["w_main"], w["w_gates"], w["b_gates"], w["conv_w"], w["conv_b"],
        conv_prev, nb=nb, tl=tl)
    c0, n0, m0 = state
    n0 = jnp.pad(n0, ((0, 0), (0, 8 - NH_M), (0, 0)))
    m0 = jnp.broadcast_to(jnp.pad(m0, ((0, 0), (0, 8 - NH_M)))[:, :, None], (NB, 8, LANES))
    hm, c_new, n_new, m_new = _mlstm(q, k, v, gates, mo, c0, n0, m0, w["g_mlstm"], nbm=nbm, lc=lc)
    if cache is None:
        att = _attn_prompt(aq, ak, av, w["ext_prompt"])
    else:
        att = _attn_sample(aq, ak, av, cache[0], cache[1], w["ext_sample"], nb=nb)
    y = _ffn(x1, mod, w["g2"], w["up2"], w["dn2"], w["g_final"], nb=nb, tl=tl, sub=2,
             mix=(hm, att, w["w_out"]), final=final)
    n_keep = k_tail.shape[1]
    states = (c_new, n_new[:, :NH_M, :], m_new[:, :NH_M, 0], conv_new,
              k_tail.reshape(NB, n_keep, NH_A, DH_A), v_tail.reshape(NB, n_keep, NH_A, DH_A))
    return y, states


def kernel(x_prompt, x_sample, state_mlstm_C, state_mlstm_n, state_mlstm_m, state_conv, cache_att_k, cache_att_v, c_prompt, c_sample, w_ada, b_ada, g_norm, w_ffn1_up, w_ffn1_down, w_ffn2_up, w_ffn2_down, w_in, conv_w, conv_b, b_gates, g_mlstm, rel_bias_table, w_out, g_final):
    depth = w_ada.shape[0]
    bp, seq, d = x_prompt.shape
    bs, dseq, _ = x_sample.shape
    xp, xs = x_prompt, x_sample
    st_p, st_s = [], []
    n_c = cache_att_k.shape[2]
    tl_p = min(ATT_WINDOW, seq)
    lc_p = min(MLSTM_CHUNK, seq)
    nb_s = max(1, min(bs, ATT_WINDOW // dseq))
    for l in range(depth):
        rows = bp + bs
        pad = (-rows) % 8
        c_all = jnp.concatenate([c_prompt, c_sample, jnp.zeros((pad, d), F32)], axis=0)
        mod = _adaln(c_all, w_ada[l], b_ada[l])
        mod_p = mod[:bp].reshape(bp, N_MOD, d)
        mod_s = mod[bp:rows].reshape(bs, N_MOD, d)
        off_g = 4 * D_M
        off_a = off_g + 2 * NH_M
        wl = w_in[l]
        rel = rel_bias_table[l]
        w = {
            "g0": g_norm[l, 0:1], "g1": g_norm[l, 1:2], "g2": g_norm[l, 2:3],
            "g_final": g_final.reshape(1, d),
            "up1": w_ffn1_up[l].astype(BF16), "dn1": w_ffn1_down[l].astype(BF16),
            "up2": w_ffn2_up[l].astype(BF16), "dn2": w_ffn2_down[l].astype(BF16),
            "w_main": jnp.concatenate([wl[:, :off_g], wl[:, off_a:]], axis=1).astype(BF16),
            "w_gates": _gate_lanes(wl[:, off_g:off_a]).astype(BF16),
            "b_gates": _gate_lanes(b_gates[l].reshape(1, 2 * NH_M)),
            "conv_w": conv_w[l], "conv_b": conv_b[l].reshape(1, 2 * D_M),
            "g_mlstm": g_mlstm[l].reshape(1, D_M),
            "w_out": w_out[l].astype(BF16),
            "ext_prompt": _rel_ext(rel, ATT_TQ, 3 * ATT_TQ),
            "ext_sample": _rel_ext(rel, dseq, n_c + dseq),
        }
        zero_state = (jnp.zeros((bp, NH_M, DH_M, DH_M), F32), jnp.zeros((bp, NH_M, DH_M), F32),
                      jnp.zeros((bp, NH_M), F32))
        xp, sp = _layer(xp, mod_p, jnp.zeros((bp, CONV_W - 1, 2 * D_M), F32), zero_state, None, w,
                        nb=1, tl=tl_p, nbm=bp, lc=lc_p, final=l == depth - 1)
        state_s = (state_mlstm_C[l], state_mlstm_n[l], state_mlstm_m[l])
        cache = (cache_att_k[l].reshape(bs, n_c, D_A), cache_att_v[l].reshape(bs, n_c, D_A))
        xs, ss = _layer(xs, mod_s, state_conv[l], state_s, cache, w, nb=nb_s, tl=dseq,
                        nbm=min(bs, 4), lc=dseq, final=l == depth - 1)
        st_p.append(sp)
        st_s.append(ss)
    stk = lambda sts, i: jnp.stack([s[i] for s in sts])
    return ((xp, xs) + tuple(stk(st_p, i) for i in range(6)) + tuple(stk(st_s, i) for i in range(6)))
```

```python
import functools

import jax
import jax.numpy as jnp
from jax import lax
from jax.experimental import pallas as pl
from jax.experimental.pallas import tpu as pltpu

F32 = jnp.float32
BF16 = jnp.bfloat16

CHUNK = 64
NH_M = 4
DH_M = 128
D_M = NH_M * DH_M
NH_A = 8
DH_A = 64
D_A = NH_A * DH_A
CONV_W = 4
LEFT_CHUNKS = 8
ATT_WINDOW = LEFT_CHUNKS * CHUNK
MAX_REL = 2 * CHUNK
N_MOD = 9
EPS = 1e-6
NEG = -1e30
LOG2E = 1.4426950408889634

LANES = 128
V7X_VMEM_BYTES = 64 * 1024 * 1024
VMEM_LIMIT = V7X_VMEM_BYTES - 8 * 1024 * 1024

FF_CHUNK = 256
MLSTM_CHUNK = 256
ATT_TQ = 256


def _cparams(n_axes):
    return pltpu.CompilerParams(dimension_semantics=("arbitrary",) * n_axes,
                                vmem_limit_bytes=VMEM_LIMIT)


def _const_spec(shape):
    nd = len(shape)
    return pl.BlockSpec(shape, lambda *_: (0,) * nd, pipeline_mode=pl.Buffered(1))


def _dot(a, b):
    return jnp.dot(a, b, preferred_element_type=F32)


def _dot_nt(a, b):
    return lax.dot_general(a, b, (((1,), (1,)), ((), ())), preferred_element_type=F32)


def _dot_tn(a, b):
    return lax.dot_general(a, b, (((0,), (0,)), ((), ())), preferred_element_type=F32)


def _sigmoid(x):
    return 1.0 / (1.0 + jnp.exp(-x))


def _log_sigmoid(x):
    return jnp.minimum(x, 0.0) - jnp.log(1.0 + jnp.exp(-jnp.abs(x)))


def _norm_mod(x, g, shift, scale):
    ms = jnp.mean(x * x, axis=-1, keepdims=True)
    h = x * lax.rsqrt(ms + EPS) * g
    return h * (1.0 + scale) + shift


def _adaln_kernel(c_ref, w_ref, b_ref, o_ref):
    c = c_ref[...]
    o_ref[...] = _dot(c * _sigmoid(c), w_ref[...]) + b_ref[...]


def _adaln(c_all, w_ada, b_ada):
    rows, d = c_all.shape
    n = w_ada.shape[1]
    tn = d
    return pl.pallas_call(
        _adaln_kernel,
        out_shape=jax.ShapeDtypeStruct((rows, n), F32),
        grid=(n // tn,),
        in_specs=[pl.BlockSpec((rows, d), lambda j: (0, 0)),
                  pl.BlockSpec((d, tn), lambda j: (0, j)),
                  pl.BlockSpec((1, tn), lambda j: (0, j))],
        out_specs=pl.BlockSpec((rows, tn), lambda j: (0, j)),
        compiler_params=_cparams(1),
        name="adaln",
    )(c_all, w_ada, b_ada.reshape(1, n))


def _ffn_kernel(*refs, nb, tl, sub, with_mix, with_final):
    if with_mix:
        x_ref, hm_ref, att_ref, mod_ref, gn_ref, wout_ref, wup_ref, wdn_ref, gfin_ref, o_ref, hb_ref, act_ref = refs
    else:
        x_ref, mod_ref, gn_ref, wup_ref, wdn_ref, gfin_ref, o_ref, hb_ref, act_ref = refs
    m = nb * tl
    d = x_ref.shape[-1]
    dff = wdn_ref.shape[0]
    mod = mod_ref[...]
    shift, scale, gate = (mod[:, 3 * sub + i:3 * sub + i + 1, :] for i in range(3))

    if with_mix:
        mix = _dot(hm_ref[...].reshape(m, D_M), wout_ref[0:D_M, :])
        mix = mix + _dot(att_ref[...].reshape(m, D_A), wout_ref[D_M:, :])
        o_ref[...] = x_ref[...] + mod[:, 5:6, :] * mix.reshape(nb, tl, d)
    else:
        o_ref[...] = x_ref[...]

    h = _norm_mod(o_ref[...], gn_ref[...], shift, scale)
    hb_ref[...] = h.reshape(m, d).astype(BF16)
    for c in range(dff // FF_CHUNK):
        lo = c * FF_CHUNK
        g = _dot(hb_ref[...], wup_ref[:, lo:lo + FF_CHUNK])
        u = _dot(hb_ref[...], wup_ref[:, dff + lo:dff + lo + FF_CHUNK])
        act_ref[:, lo:lo + FF_CHUNK] = (g * _sigmoid(g) * u).astype(BF16)
    dn = _dot(act_ref[...], wdn_ref[...])
    y = o_ref[...] + 0.5 * gate * dn.reshape(nb, tl, d)
    if with_final:
        ms = jnp.mean(y * y, axis=-1, keepdims=True)
        y = y * lax.rsqrt(ms + EPS) * gfin_ref[...]
    o_ref[...] = y


def _ffn(x, mod, g_norm_row, w_up, w_dn, g_final, *, nb, tl, sub, mix=None, final=False):
    NB, L, d = x.shape
    dff = w_dn.shape[0]
    m = nb * tl
    grid = (NB // nb, L // tl)
    row = lambda w: pl.BlockSpec((nb, tl, w), lambda i, j: (i, j, 0))
    in_specs = [row(d)]
    args = [x]
    if mix is not None:
        hm, att, w_out = mix
        in_specs += [row(D_M), row(D_A)]
        args += [hm, att]
    in_specs += [pl.BlockSpec((nb, N_MOD, d), lambda i, j: (i, 0, 0)), _const_spec((1, d))]
    args += [mod, g_norm_row]
    if mix is not None:
        in_specs.append(_const_spec(w_out.shape))
        args.append(w_out)
    in_specs += [_const_spec(w_up.shape), _const_spec(w_dn.shape), _const_spec((1, d))]
    args += [w_up, w_dn, g_final]
    kern = functools.partial(_ffn_kernel, nb=nb, tl=tl, sub=sub, with_mix=mix is not None,
                             with_final=final)
    return pl.pallas_call(
        kern,
        out_shape=jax.ShapeDtypeStruct((NB, L, d), F32),
        grid=grid,
        in_specs=in_specs,
        out_specs=row(d),
        scratch_shapes=[pltpu.VMEM((m, d), BF16), pltpu.VMEM((m, dff), BF16)],
        compiler_params=_cparams(2),
        name="ffn%d" % sub,
    )(*args)


def _proj_kernel(x_ref, mod_ref, gn_ref, w_ref, wg_ref, bg_ref, cw_ref, cb_ref, cprev_ref,
                 q_ref, k_ref, v_ref, o_ref, gates_ref, aq_ref, ak_ref, av_ref,
                 kt_ref, vt_ref, cnew_ref, hb_ref, carry_ref, *, nb, tl):
    m = nb * tl
    d = x_ref.shape[-1]
    j = pl.program_id(1)
    mod = mod_ref[...]
    h = _norm_mod(x_ref[...], gn_ref[...], mod[:, 3:4, :], mod[:, 4:5, :])
    hb_ref[...] = h.reshape(m, d).astype(BF16)

    @pl.when(j == 0)
    def _():
        carry_ref[...] = jnp.zeros(carry_ref.shape, F32)
        carry_ref[:, 8 - (CONV_W - 1):8, :] = cprev_ref[...]

    def qk_pre(c):
        lo = c * FF_CHUNK
        return _dot(hb_ref[...], w_ref[:, lo:lo + FF_CHUNK]).reshape(nb, tl, FF_CHUNK)

    def conv(c, pre):
        lo = c * FF_CHUNK
        ext = jnp.concatenate([carry_ref[:, :, lo:lo + FF_CHUNK], pre], axis=1)
        acc = cb_ref[:, lo:lo + FF_CHUNK]
        for t in reversed(range(CONV_W)):
            s0 = 8 - (CONV_W - 1) + t
            acc = acc + cw_ref[t:t + 1, lo:lo + FF_CHUNK] * ext[:, s0:s0 + tl, :]
        y = acc * _sigmoid(acc)
        if lo < D_M:
            q_ref[:, :, lo:lo + FF_CHUNK] = y.astype(BF16)
        else:
            k_ref[:, :, lo - D_M:lo - D_M + FF_CHUNK] = (y * DH_M ** -0.5).astype(BF16)
        cnew_ref[:, :, lo:lo + FF_CHUNK] = pre[:, tl - (CONV_W - 1):tl, :]
        carry_ref[:, :, lo:lo + FF_CHUNK] = pre[:, tl - 8:tl, :]

    def cols(c0):
        return _dot(hb_ref[...], w_ref[:, c0:c0 + D_M]).reshape(nb, tl, D_M)

    pre = qk_pre(0)
    v_ref[...] = cols(2 * D_M).astype(BF16)
    conv(0, pre)
    pre = qk_pre(1)
    o_ref[...] = cols(3 * D_M)
    conv(1, pre)
    pre = qk_pre(2)
    aq_ref[...] = (cols(4 * D_M) * (DH_A ** -0.5 * LOG2E)).astype(BF16)
    conv(2, pre)
    pre = qk_pre(3)
    ak = cols(4 * D_M + D_A)
    ak_ref[...] = ak.astype(BF16)
    kt_ref[...] = ak
    conv(3, pre)
    av = cols(4 * D_M + 2 * D_A)
    av_ref[...] = av.astype(BF16)
    vt_ref[...] = av
    gates_ref[...] = (_dot(hb_ref[...], wg_ref[...]) + bg_ref[...]).reshape(nb, tl, LANES)


def _proj(x, mod, g_norm_row, w_main, w_gates, b_gates, conv_w, conv_b, conv_prev, *, nb, tl):
    NB, L, d = x.shape
    assert tl == min(ATT_WINDOW, L)
    m = nb * tl
    grid = (NB // nb, L // tl)
    row = lambda w: pl.BlockSpec((nb, tl, w), lambda i, j: (i, j, 0))
    per_b = lambda r, w: pl.BlockSpec((nb, r, w), lambda i, j: (i, 0, 0))
    bshape = lambda w, dt: jax.ShapeDtypeStruct((NB, L, w), dt)
    in_specs = [row(d), per_b(N_MOD, d), _const_spec((1, d)), _const_spec(w_main.shape),
                _const_spec(w_gates.shape), _const_spec((1, LANES)), _const_spec(conv_w.shape),
                _const_spec((1, 2 * D_M)), per_b(CONV_W - 1, 2 * D_M)]
    out_shape = [bshape(D_M, BF16), bshape(D_M, BF16), bshape(D_M, BF16), bshape(D_M, F32),
                 bshape(LANES, F32), bshape(D_A, BF16), bshape(D_A, BF16), bshape(D_A, BF16),
                 jax.ShapeDtypeStruct((NB, tl, D_A), F32), jax.ShapeDtypeStruct((NB, tl, D_A), F32),
                 jax.ShapeDtypeStruct((NB, CONV_W - 1, 2 * D_M), F32)]
    out_specs = [row(D_M), row(D_M), row(D_M), row(D_M), row(LANES), row(D_A), row(D_A), row(D_A),
                 per_b(tl, D_A), per_b(tl, D_A), per_b(CONV_W - 1, 2 * D_M)]
    return pl.pallas_call(
        functools.partial(_proj_kernel, nb=nb, tl=tl),
        out_shape=out_shape,
        grid=grid,
        in_specs=in_specs,
        out_specs=out_specs,
        scratch_shapes=[pltpu.VMEM((m, d), BF16), pltpu.VMEM((nb, 8, 2 * D_M), F32)],
        compiler_params=_cparams(2),
        name="proj",
    )(x, mod, g_norm_row, w_main, w_gates, b_gates, conv_w, conv_b, conv_prev)


def _prefix_max_lanes(x):
    n = x.shape[-1]
    lane = lax.broadcasted_iota(jnp.int32, x.shape, 1)
    shift = 1
    while shift < n:
        x = jnp.maximum(x, jnp.where(lane >= shift, pltpu.roll(x, shift, 1), NEG))
        shift *= 2
    return x


def _split_bf16(x):
    hi = x.astype(BF16)
    r = x - hi.astype(F32)
    mid = r.astype(BF16)
    return hi, mid, (r - mid.astype(F32)).astype(BF16)


def _mlstm_kernel(q_ref, k_ref, v_ref, gates_ref, mo_ref, c0_ref, n0_ref, m0_ref, gm_ref,
                  hm_ref, c_ref, n_ref, m_ref, ct_ref, *, nbm, lc):
    j = pl.program_id(1)

    def transpose_states(src_ref, dst_ref):
        def one(i, carry):
            dst_ref[i // NH_M, i % NH_M] = jnp.transpose(src_ref[i // NH_M, i % NH_M])
            return carry
        lax.fori_loop(0, nbm * NH_M, one, 0)

    @pl.when(j == 0)
    def _():
        transpose_states(c0_ref, ct_ref)
        n_ref[...] = n0_ref[...]
        m_ref[...] = m0_ref[...]

    s_i = lax.broadcasted_iota(jnp.int32, (lc, lc), 0)
    t_i = lax.broadcasted_iota(jnp.int32, (lc, lc), 1)
    causal = s_i <= t_i
    upper = jnp.where(causal, 1.0, 0.0).astype(BF16)
    head_row = lax.broadcasted_iota(jnp.int32, (8, 1), 0)

    pairs = [(b, h) for b in range(nbm) for h in range(NH_M)]
    hsl = lambda h: slice(h * DH_M, (h + 1) * DH_M)

    rows = []
    for b in range(nbm):
        gates_t = jnp.transpose(gates_ref[b])
        ig_t = gates_t[0:8]
        lf_t = _log_sigmoid(gates_t[8:16])
        b_t = sum(_dot(part, upper) for part in _split_bf16(lf_t))
        c_t = ig_t - b_t
        b_end = b_t[:, lc - 1:lc]
        m_prev = m_ref[b][:, 0:1]
        log_g = b_end + c_t
        m_end = jnp.maximum(b_end + m_prev, jnp.max(log_g, axis=-1, keepdims=True))
        decay = jnp.exp(b_end + m_prev - m_end)
        g_rows = jnp.exp(log_g - m_end)
        log_inter = b_t + m_prev
        c2_t = c_t * LOG2E
        c2_cols = jnp.transpose(jnp.concatenate([c2_t, jnp.zeros((LANES - 8, lc), F32)], axis=0))
        if lc % LANES == 0:
            cmax2_t = _prefix_max_lanes(c2_t)
        else:
            cmax2_t = jnp.concatenate(
                [jnp.max(jnp.where(causal, c2_cols[:, h:h + 1], NEG), axis=0, keepdims=True)
                 for h in range(8)], axis=0)
        m_intra = b_t + cmax2_t * (1.0 / LOG2E)
        m_tok = jnp.maximum(log_inter, m_intra)
        rows.append(dict(decay=decay, g=g_rows, g_b=g_rows.astype(BF16), m_end=m_end,
                         c2_cols=c2_cols, cmax2=cmax2_t, r_intra=jnp.exp(m_intra - m_tok),
                         w_inter=jnp.exp(log_inter - m_tok), floor=jnp.exp(-m_tok),
                         n_prev_b=n_ref[b].astype(BF16)))

    early = {}
    for b, h in pairs:
        q = q_ref[b, :, hsl(h)]
        k = k_ref[b, :, hsl(h)]
        early[b, h] = dict(
            s_t=_dot_nt(k, q),
            cq_t=_dot_nt(ct_ref[b, h].astype(BF16), q),
            nq=_dot_nt(rows[b]["n_prev_b"], q)[h:h + 1, :],
            v_t=jnp.transpose(v_ref[b, :, hsl(h)].astype(F32)))

    n_upd = [jnp.zeros((8, DH_M), F32) for _ in range(nbm)]
    for b, h in pairs:
        r, e = rows[b], early[b, h]
        k = k_ref[b, :, hsl(h)]
        d_t = jnp.exp2(r["c2_cols"][:, h:h + 1] - r["cmax2"][h:h + 1, :])
        a_t = jnp.where(causal, d_t, 0.0) * e["s_t"]
        r_intra, w_inter = r["r_intra"][h:h + 1, :], r["w_inter"][h:h + 1, :]
        num_t = _dot(e["v_t"].astype(BF16), a_t.astype(BF16)) * r_intra + e["cq_t"] * w_inter
        den = jnp.sum(a_t, axis=0, keepdims=True) * r_intra + e["nq"] * w_inter
        h_t = num_t * (1.0 / jnp.maximum(jnp.abs(den), r["floor"][h:h + 1, :]))
        hn_t = h_t * lax.rsqrt(jnp.mean(h_t * h_t, axis=0, keepdims=True) + EPS)
        hn = jnp.transpose(hn_t) * gm_ref[:, hsl(h)]
        hm_ref[b, :, hsl(h)] = (_sigmoid(mo_ref[b, :, hsl(h)]) * hn).astype(BF16)

        vg_t = (e["v_t"] * r["g"][h:h + 1, :]).astype(BF16)
        ct_ref[b, h] = r["decay"][h:h + 1, :] * ct_ref[b, h] + _dot(vg_t, k)
        n_upd[b] = n_upd[b] + jnp.where(head_row == h, _dot(r["g_b"], k), 0.0)

    for b in range(nbm):
        n_ref[b] = rows[b]["decay"] * n_ref[b] + n_upd[b]
        m_ref[b] = jnp.broadcast_to(rows[b]["m_end"], (8, LANES))

    @pl.when(j == pl.num_programs(1) - 1)
    def _():
        transpose_states(ct_ref, c_ref)


def _mlstm(q, k, v, gates, mo, c0, n0, m0, g_mlstm_row, *, nbm, lc):
    NB, L, _ = q.shape
    grid = (NB // nbm, L // lc)
    row = lambda w: pl.BlockSpec((nbm, lc, w), lambda i, j: (i, j, 0))
    st = lambda s: pl.BlockSpec((nbm,) + s, lambda i, j: (i,) + (0,) * len(s))
    c_s, n_s = (NH_M, DH_M, DH_M), (8, DH_M)
    return pl.pallas_call(
        functools.partial(_mlstm_kernel, nbm=nbm, lc=lc),
        out_shape=[jax.ShapeDtypeStruct((NB, L, D_M), BF16),
                   jax.ShapeDtypeStruct((NB,) + c_s, F32),
                   jax.ShapeDtypeStruct((NB,) + n_s, F32),
                   jax.ShapeDtypeStruct((NB,) + n_s, F32)],
        grid=grid,
        in_specs=[row(D_M), row(D_M), row(D_M), row(LANES), row(D_M), st(c_s), st(n_s), st(n_s),
                  _const_spec((1, D_M))],
        out_specs=[row(D_M), st(c_s), st(n_s), st(n_s)],
        scratch_shapes=[pltpu.VMEM((nbm,) + c_s, F32)],
        compiler_params=_cparams(2),
        name="mlstm",
    )(q, k, v, gates, mo, c0, n0, m0, g_mlstm_row)


def _build_band_bias(ext_ref, tq, klen, put):
    p_len = ext_ref.shape[-1]
    q_pos = lax.broadcasted_iota(jnp.int32, (tq, klen), 0) + (klen - tq)
    k_pos = lax.broadcasted_iota(jnp.int32, (tq, klen), 1)
    back = q_pos // CHUNK - k_pos // CHUNK
    for h in range(NH_A):
        base = jnp.broadcast_to(ext_ref[h:h + 1, :] * LOG2E, (tq, p_len))
        toeplitz = pltpu.roll(base, 0, 1, stride=1, stride_axis=0)[:, :klen]
        put(h, jnp.where(back >= 0, jnp.where(back <= LEFT_CHUNKS, toeplitz, NEG), NEG))


def _lane_tiles(s):
    n = s.shape[-1]
    if n % LANES:
        return [s]
    return [s[:, i * LANES:(i + 1) * LANES] for i in range(n // LANES)]


def _row_reduce(tiles, op, lane_op):
    by_width = {}
    for t in tiles:
        by_width[t.shape[-1]] = t if t.shape[-1] not in by_width else op(by_width[t.shape[-1]], t)
    return functools.reduce(op, [lane_op(t, axis=-1, keepdims=True) for t in by_width.values()])


def _softmax_pv(ss, vs):
    mx = _row_reduce([t for s in ss for t in _lane_tiles(s)], jnp.maximum, jnp.max)
    es = [jnp.exp2(s - mx) for s in ss]
    l = _row_reduce([t for e in es for t in _lane_tiles(e)], jnp.add, jnp.sum)
    o = functools.reduce(jnp.add, [_dot(e.astype(BF16), v) for e, v in zip(es, vs)])
    return o / l


def _attn_pairs(jobs):
    first = lax.broadcasted_iota(jnp.int32, (1, LANES), 1) < DH_A
    zero = jnp.zeros((), BF16)

    def scores(q, segs, p):
        sl = slice(p * LANES, (p + 1) * LANES)
        qp = q(sl)
        q2 = jnp.concatenate([jnp.where(first, qp, zero), jnp.where(first, zero, qp)], axis=0)
        ss = []
        for k, _, bias, pen in segs:
            s = _dot_nt(q2, k(sl)) + bias(p)
            ss.append(s if pen is None else s + pen)
        return ss

    def finish(ss, segs, put, p):
        sl = slice(p * LANES, (p + 1) * LANES)
        r = ss[0].shape[0] // 2
        o2 = _softmax_pv(ss, [v(sl) for _, v, _, _ in segs])
        put(sl, jnp.where(first, o2[:r], o2[r:]))

    pending = None
    for q, segs, put in jobs:
        for p in range(NH_A // 2):
            ss = scores(q, segs, p)
            if pending is not None:
                finish(*pending)
            pending = (ss, segs, put, p)
    finish(*pending)


def _attn_prompt_kernel(q_ref, k0_ref, k1_ref, k2_ref, v0_ref, v1_ref, v2_ref, ext_ref, o_ref,
                        bias_ref):
    tq = q_ref.shape[1]
    hq = tq // 2
    j = pl.program_id(1)

    @pl.when((pl.program_id(0) == 0) & (j == 0))
    def _():
        def put_bias(h, tile):
            for half in range(2):
                bias_ref[h // 2, half, (h % 2) * hq:(h % 2 + 1) * hq, :] = tile[half * hq:(half + 1) * hq]
        _build_band_bias(ext_ref, tq, 3 * tq, put_bias)

    def run(pen0, pen1):
        jobs = []
        for half in range(2):
            lo = half * hq
            bias = lambda a, b, half=half: (lambda p: bias_ref[p, half, :, a:b])

            def put(sl, val, lo=lo):
                o_ref[0, lo:lo + hq, sl] = val.astype(o_ref.dtype)

            rows = lambda ref, a, b: (lambda sl: ref[0, a:b, sl])
            if half == 0:
                segs = [(rows(k0_ref, 0, tq), rows(v0_ref, 0, tq), bias(0, tq), pen0),
                        (rows(k1_ref, 0, tq), rows(v1_ref, 0, tq), bias(tq, 2 * tq), pen1),
                        (rows(k2_ref, 0, hq), rows(v2_ref, 0, hq), bias(2 * tq, 2 * tq + hq), None)]
            else:
                segs = [(rows(k0_ref, hq, tq), rows(v0_ref, hq, tq), bias(hq, tq), pen0),
                        (rows(k1_ref, 0, tq), rows(v1_ref, 0, tq), bias(tq, 2 * tq), pen1),
                        (rows(k2_ref, 0, tq), rows(v2_ref, 0, tq), bias(2 * tq, 3 * tq), None)]
            jobs.append((rows(q_ref, lo, lo + hq), segs, put))
        _attn_pairs(jobs)

    @pl.when(j >= 2)
    def _():
        run(None, None)

    @pl.when(j < 2)
    def _():
        run(jnp.where(j >= 2, 0.0, NEG).astype(F32), jnp.where(j >= 1, 0.0, NEG).astype(F32))


def _attn_prompt(aq, ak, av, ext):
    NB, L, _ = aq.shape
    tq = ATT_TQ
    grid = (NB, L // tq)
    row = pl.BlockSpec((1, tq, D_A), lambda b, j: (b, j, 0))
    back = lambda n: pl.BlockSpec((1, tq, D_A), lambda b, j: (b, jnp.maximum(j - n, 0), 0))
    return pl.pallas_call(
        _attn_prompt_kernel,
        out_shape=jax.ShapeDtypeStruct((NB, L, D_A), BF16),
        grid=grid,
        in_specs=[row, back(2), back(1), row, back(2), back(1), row, _const_spec(ext.shape)],
        out_specs=row,
        scratch_shapes=[pltpu.VMEM((NH_A // 2, 2, tq, 3 * tq), F32)],
        compiler_params=_cparams(2),
        name="attn_prompt",
    )(aq, ak, ak, ak, av, av, av, ext)


def _attn_sample_kernel(q_ref, k_ref, v_ref, ck_ref, cv_ref, ext_ref, o_ref, bias_ref, *, nb):
    tq = q_ref.shape[1]
    nc = ck_ref.shape[1]

    @pl.when(pl.program_id(0) == 0)
    def _():
        def put_bias(h, tile):
            bias_ref[h // 2, (h % 2) * tq:(h % 2 + 1) * tq, :] = tile
        _build_band_bias(ext_ref, tq, nc + tq, put_bias)

    jobs = []
    for b in range(nb):
        rows = lambda ref, b=b: (lambda sl: ref[b, :, sl])

        def put(sl, val, b=b):
            o_ref[b, :, sl] = val.astype(o_ref.dtype)

        segs = [(rows(ck_ref), rows(cv_ref), lambda p: bias_ref[p, :, 0:nc], None),
                (rows(k_ref), rows(v_ref), lambda p: bias_ref[p, :, nc:nc + tq], None)]
        jobs.append((rows(q_ref), segs, put))
    _attn_pairs(jobs)


def _attn_sample(aq, ak, av, cache_k, cache_v, ext, *, nb):
    NB, L, _ = aq.shape
    nc = cache_k.shape[1]
    row = pl.BlockSpec((nb, L, D_A), lambda i: (i, 0, 0))
    crow = pl.BlockSpec((nb, nc, D_A), lambda i: (i, 0, 0))
    return pl.pallas_call(
        functools.partial(_attn_sample_kernel, nb=nb),
        out_shape=jax.ShapeDtypeStruct((NB, L, D_A), BF16),
        grid=(NB // nb,),
        in_specs=[row, row, row, crow, crow, _const_spec(ext.shape)],
        out_specs=row,
        scratch_shapes=[pltpu.VMEM((NH_A // 2, 2 * L, nc + L), F32)],
        compiler_params=_cparams(1),
        name="attn_sample",
    )(aq, ak, av, cache_k, cache_v, ext)


def _rel_ext(rel_table, tq, klen):
    p_len = -(-(klen + tq) // LANES) * LANES
    n_lo = tq - CHUNK
    n_hi = p_len - tq - MAX_REL
    rep = lambda col, n: jnp.broadcast_to(col, (NH_A, n))
    by_dist = jnp.concatenate([rep(rel_table[:, :1], n_lo), rel_table, rep(rel_table[:, -1:], n_hi)],
                              axis=1)
    rev = by_dist[:, ::-1]
    n = p_len - klen
    return jnp.concatenate([rev[:, n:], rev[:, :n]], axis=1).astype(F32)


def _gate_lanes(g):
    z = lambda n: jnp.zeros((g.shape[0], n), g.dtype)
    return jnp.concatenate([g[:, :NH_M], z(8 - NH_M), g[:, NH_M:], z(LANES - 8 - NH_M)], axis=1)


def _layer(x, mod, conv_prev, state, cache, w, *, nb, tl, nbm, lc, final):
    NB, L, d = x.shape
    x1 = _ffn(x, mod, w["g0"], w["up1"], w["dn1"], w["g_final"], nb=nb, tl=tl, sub=0)
    (q, k, v, mo, gates, aq, ak, av, k_tail, v_tail, conv_new) = _proj(
        x1, mod, w["g1"], w["w_main"], w["w_gates"], w["b_gates"], w["conv_w"], w["conv_b"],
        conv_prev, nb=nb, tl=tl)
    c0, n0, m0 = state
    n0 = jnp.pad(n0, ((0, 0), (0, 8 - NH_M), (0, 0)))
    m0 = jnp.broadcast_to(jnp.pad(m0, ((0, 0), (0, 8 - NH_M)))[:, :, None], (NB, 8, LANES))
    hm, c_new, n_new, m_new = _mlstm(q, k, v, gates, mo, c0, n0, m0, w["g_mlstm"], nbm=nbm, lc=lc)
    if cache is None:
        att = _attn_prompt(aq, ak, av, w["ext_prompt"])
    else:
        att = _attn_sample(aq, ak, av, cache[0], cache[1], w["ext_sample"], nb=nb)
    y = _ffn(x1, mod, w["g2"], w["up2"], w["dn2"], w["g_final"], nb=nb, tl=tl, sub=2,
             mix=(hm, att, w["w_out"]), final=final)
    n_keep = k_tail.shape[1]
    states = (c_new, n_new[:, :NH_M, :], m_new[:, :NH_M, 0], conv_new,
              k_tail.reshape(NB, n_keep, NH_A, DH_A), v_tail.reshape(NB, n_keep, NH_A, DH_A))
    return y, states


def kernel(x_prompt, x_sample, state_mlstm_C, state_mlstm_n, state_mlstm_m, state_conv, cache_att_k, cache_att_v, c_prompt, c_sample, w_ada, b_ada, g_norm, w_ffn1_up, w_ffn1_down, w_ffn2_up, w_ffn2_down, w_in, conv_w, conv_b, b_gates, g_mlstm, rel_bias_table, w_out, g_final):
    depth = w_ada.shape[0]
    bp, seq, d = x_prompt.shape
    bs, dseq, _ = x_sample.shape
    xp, xs = x_prompt, x_sample
    st_p, st_s = [], []
    n_c = cache_att_k.shape[2]
    tl_p = min(ATT_WINDOW, seq)
    lc_p = min(MLSTM_CHUNK, seq)
    nb_s = max(1, min(bs, ATT_WINDOW // dseq))
    for l in range(depth):
        rows = bp + bs
        pad = (-rows) % 8
        c_all = jnp.concatenate([c_prompt, c_sample, jnp.zeros((pad, d), F32)], axis=0)
        mod = _adaln(c_all, w_ada[l], b_ada[l])
        mod_p = mod[:bp].reshape(bp, N_MOD, d)
        mod_s = mod[bp:rows].reshape(bs, N_MOD, d)
        off_g = 4 * D_M
        off_a = off_g + 2 * NH_M
        wl = w_in[l]
        rel = rel_bias_table[l]
        w = {
            "g0": g_norm[l, 0:1], "g1": g_norm[l, 1:2], "g2": g_norm[l, 2:3],
            "g_final": g_final.reshape(1, d),
            "up1": w_ffn1_up[l].astype(BF16), "dn1": w_ffn1_down[l].astype(BF16),
            "up2": w_ffn2_up[l].astype(BF16), "dn2": w_ffn2_down[l].astype(BF16),
            "w_main": jnp.concatenate([wl[:, :off_g], wl[:, off_a:]], axis=1).astype(BF16),
            "w_gates": _gate_lanes(wl[:, off_g:off_a]).astype(BF16),
            "b_gates": _gate_lanes(b_gates[l].reshape(1, 2 * NH_M)),
            "conv_w": conv_w[l], "conv_b": conv_b[l].reshape(1, 2 * D_M),
            "g_mlstm": g_mlstm[l].reshape(1, D_M),
            "w_out": w_out[l].astype(BF16),
            "ext_prompt": _rel_ext(rel, ATT_TQ, 3 * ATT_TQ),
            "ext_sample": _rel_ext(rel, dseq, n_c + dseq),
        }
        zero_state = (jnp.zeros((bp, NH_M, DH_M, DH_M), F32), jnp.zeros((bp, NH_M, DH_M), F32),
                      jnp.zeros((bp, NH_M), F32))
        xp, sp = _layer(xp, mod_p, jnp.zeros((bp, CONV_W - 1, 2 * D_M), F32), zero_state, None, w,
                        nb=1, tl=tl_p, nbm=bp, lc=lc_p, final=l == depth - 1)
        state_s = (state_mlstm_C[l], state_mlstm_n[l], state_mlstm_m[l])
        cache = (cache_att_k[l].reshape(bs, n_c, D_A).astype(BF16),
                 cache_att_v[l].reshape(bs, n_c, D_A).astype(BF16))
        xs, ss = _layer(xs, mod_s, state_conv[l], state_s, cache, w, nb=nb_s, tl=dseq,
                        nbm=min(bs, 4), lc=dseq, final=l == depth - 1)
        st_p.append(sp)
        st_s.append(ss)
    stk = lambda sts, i: jnp.stack([s[i] for s in sts])
    return ((xp, xs) + tuple(stk(st_p, i) for i in range(6)) + tuple(stk(st_s, i) for i in range(6)))
```

```python
import functools

import jax
import jax.numpy as jnp
from jax import lax
from jax.experimental import pallas as pl
from jax.experimental.pallas import tpu as pltpu

F32 = jnp.float32
BF16 = jnp.bfloat16

CHUNK = 64
NH_M = 4
DH_M = 128
D_M = NH_M * DH_M
NH_A = 8
DH_A = 64
D_A = NH_A * DH_A
CONV_W = 4
LEFT_CHUNKS = 8
ATT_WINDOW = LEFT_CHUNKS * CHUNK
MAX_REL = 2 * CHUNK
N_MOD = 9
EPS = 1e-6
NEG = -1e30
LOG2E = 1.4426950408889634

LANES = 128
V7X_VMEM_BYTES = 64 * 1024 * 1024
VMEM_LIMIT = V7X_VMEM_BYTES - 8 * 1024 * 1024

FF_CHUNK = 256
MLSTM_CHUNK = 256
ATT_TQ = 256


def _cparams(n_axes):
    return pltpu.CompilerParams(dimension_semantics=("arbitrary",) * n_axes,
                                vmem_limit_bytes=VMEM_LIMIT)


def _const_spec(shape):
    nd = len(shape)
    return pl.BlockSpec(shape, lambda *_: (0,) * nd, pipeline_mode=pl.Buffered(1))


def _dot(a, b):
    return jnp.dot(a, b, preferred_element_type=F32)


def _dot_nt(a, b):
    return lax.dot_general(a, b, (((1,), (1,)), ((), ())), preferred_element_type=F32)


def _dot_tn(a, b):
    return lax.dot_general(a, b, (((0,), (0,)), ((), ())), preferred_element_type=F32)


def _sigmoid(x):
    return 1.0 / (1.0 + jnp.exp(-x))


def _log_sigmoid(x):
    return jnp.minimum(x, 0.0) - jnp.log(1.0 + jnp.exp(-jnp.abs(x)))


def _norm_mod(x, g, shift, scale):
    ms = jnp.mean(x * x, axis=-1, keepdims=True)
    h = x * lax.rsqrt(ms + EPS) * g
    return h * (1.0 + scale) + shift


def _adaln_kernel(c_ref, w_ref, b_ref, o_ref):
    c = c_ref[...]
    o_ref[...] = _dot(c * _sigmoid(c), w_ref[...]) + b_ref[...]


def _adaln(c_all, w_ada, b_ada):
    rows, d = c_all.shape
    n = w_ada.shape[1]
    tn = d
    return pl.pallas_call(
        _adaln_kernel,
        out_shape=jax.ShapeDtypeStruct((rows, n), F32),
        grid=(n // tn,),
        in_specs=[pl.BlockSpec((rows, d), lambda j: (0, 0)),
                  pl.BlockSpec((d, tn), lambda j: (0, j)),
                  pl.BlockSpec((1, tn), lambda j: (0, j))],
        out_specs=pl.BlockSpec((rows, tn), lambda j: (0, j)),
        compiler_params=_cparams(1),
        name="adaln",
    )(c_all, w_ada, b_ada.reshape(1, n))


def _ffn_kernel(*refs, nb, tl, sub, with_mix, with_final):
    if with_mix:
        x_ref, hm_ref, att_ref, mod_ref, gn_ref, wout_ref, wup_ref, wdn_ref, gfin_ref, o_ref, hb_ref, act_ref = refs
    else:
        x_ref, mod_ref, gn_ref, wup_ref, wdn_ref, gfin_ref, o_ref, hb_ref, act_ref = refs
    m = nb * tl
    d = x_ref.shape[-1]
    dff = wdn_ref.shape[0]
    mod = mod_ref[...]
    shift, scale, gate = (mod[:, 3 * sub + i:3 * sub + i + 1, :] for i in range(3))

    if with_mix:
        mix = _dot(hm_ref[...].reshape(m, D_M), wout_ref[0:D_M, :])
        mix = mix + _dot(att_ref[...].reshape(m, D_A), wout_ref[D_M:, :])
        o_ref[...] = x_ref[...] + mod[:, 5:6, :] * mix.reshape(nb, tl, d)
    else:
        o_ref[...] = x_ref[...]

    h = _norm_mod(o_ref[...], gn_ref[...], shift, scale)
    hb_ref[...] = h.reshape(m, d).astype(BF16)
    for c in range(dff // FF_CHUNK):
        lo = c * FF_CHUNK
        g = _dot(hb_ref[...], wup_ref[:, lo:lo + FF_CHUNK])
        u = _dot(hb_ref[...], wup_ref[:, dff + lo:dff + lo + FF_CHUNK])
        act_ref[:, lo:lo + FF_CHUNK] = (g * _sigmoid(g) * u).astype(BF16)
    dn = _dot(act_ref[...], wdn_ref[...])
    y = o_ref[...] + 0.5 * gate * dn.reshape(nb, tl, d)
    if with_final:
        ms = jnp.mean(y * y, axis=-1, keepdims=True)
        y = y * lax.rsqrt(ms + EPS) * gfin_ref[...]
    o_ref[...] = y


def _ffn(x, mod, g_norm_row, w_up, w_dn, g_final, *, nb, tl, sub, mix=None, final=False):
    NB, L, d = x.shape
    dff = w_dn.shape[0]
    m = nb * tl
    grid = (NB // nb, L // tl)
    row = lambda w: pl.BlockSpec((nb, tl, w), lambda i, j: (i, j, 0))
    in_specs = [row(d)]
    args = [x]
    if mix is not None:
        hm, att, w_out = mix
        in_specs += [row(D_M), row(D_A)]
        args += [hm, att]
    in_specs += [pl.BlockSpec((nb, N_MOD, d), lambda i, j: (i, 0, 0)), _const_spec((1, d))]
    args += [mod, g_norm_row]
    if mix is not None:
        in_specs.append(_const_spec(w_out.shape))
        args.append(w_out)
    in_specs += [_const_spec(w_up.shape), _const_spec(w_dn.shape), _const_spec((1, d))]
    args += [w_up, w_dn, g_final]
    kern = functools.partial(_ffn_kernel, nb=nb, tl=tl, sub=sub, with_mix=mix is not None,
                             with_final=final)
    return pl.pallas_call(
        kern,
        out_shape=jax.ShapeDtypeStruct((NB, L, d), F32),
        grid=grid,
        in_specs=in_specs,
        out_specs=row(d),
        scratch_shapes=[pltpu.VMEM((m, d), BF16), pltpu.VMEM((m, dff), BF16)],
        compiler_params=_cparams(2),
        name="ffn%d" % sub,
    )(*args)


def _proj_kernel(x_ref, mod_ref, gn_ref, w_ref, wg_ref, bg_ref, cw_ref, cb_ref, cprev_ref,
                 q_ref, k_ref, v_ref, o_ref, gates_ref, aq_ref, ak_ref, av_ref,
                 kt_ref, vt_ref, cnew_ref, hb_ref, carry_ref, *, nb, tl):
    m = nb * tl
    d = x_ref.shape[-1]
    j = pl.program_id(1)
    mod = mod_ref[...]
    h = _norm_mod(x_ref[...], gn_ref[...], mod[:, 3:4, :], mod[:, 4:5, :])
    hb_ref[...] = h.reshape(m, d).astype(BF16)

    @pl.when(j == 0)
    def _():
        carry_ref[...] = jnp.zeros(carry_ref.shape, F32)
        carry_ref[:, 8 - (CONV_W - 1):8, :] = cprev_ref[...]

    def qk_pre(c):
        lo = c * FF_CHUNK
        return _dot(hb_ref[...], w_ref[:, lo:lo + FF_CHUNK]).reshape(nb, tl, FF_CHUNK)

    def conv(c, pre):
        lo = c * FF_CHUNK
        ext = jnp.concatenate([carry_ref[:, :, lo:lo + FF_CHUNK], pre], axis=1)
        acc = cb_ref[:, lo:lo + FF_CHUNK]
        for t in reversed(range(CONV_W)):
            s0 = 8 - (CONV_W - 1) + t
            acc = acc + cw_ref[t:t + 1, lo:lo + FF_CHUNK] * ext[:, s0:s0 + tl, :]
        y = acc * _sigmoid(acc)
        if lo < D_M:
            q_ref[:, :, lo:lo + FF_CHUNK] = y.astype(BF16)
        else:
            k_ref[:, :, lo - D_M:lo - D_M + FF_CHUNK] = (y * DH_M ** -0.5).astype(BF16)
        cnew_ref[:, :, lo:lo + FF_CHUNK] = pre[:, tl - (CONV_W - 1):tl, :]
        carry_ref[:, :, lo:lo + FF_CHUNK] = pre[:, tl - 8:tl, :]

    def cols(c0):
        return _dot(hb_ref[...], w_ref[:, c0:c0 + D_M]).reshape(nb, tl, D_M)

    pre = qk_pre(0)
    v_ref[...] = cols(2 * D_M).astype(BF16)
    conv(0, pre)
    pre = qk_pre(1)
    o_ref[...] = cols(3 * D_M)
    conv(1, pre)
    pre = qk_pre(2)
    aq_ref[...] = (cols(4 * D_M) * (DH_A ** -0.5 * LOG2E)).astype(BF16)
    conv(2, pre)
    pre = qk_pre(3)
    ak = cols(4 * D_M + D_A)
    ak_ref[...] = ak.astype(BF16)
    kt_ref[...] = ak
    conv(3, pre)
    av = cols(4 * D_M + 2 * D_A)
    av_ref[...] = av.astype(BF16)
    vt_ref[...] = av
    gates_ref[...] = (_dot(hb_ref[...], wg_ref[...]) + bg_ref[...]).reshape(nb, tl, LANES)


def _proj(x, mod, g_norm_row, w_main, w_gates, b_gates, conv_w, conv_b, conv_prev, *, nb, tl):
    NB, L, d = x.shape
    assert tl == min(ATT_WINDOW, L)
    m = nb * tl
    grid = (NB // nb, L // tl)
    row = lambda w: pl.BlockSpec((nb, tl, w), lambda i, j: (i, j, 0))
    per_b = lambda r, w: pl.BlockSpec((nb, r, w), lambda i, j: (i, 0, 0))
    bshape = lambda w, dt: jax.ShapeDtypeStruct((NB, L, w), dt)
    in_specs = [row(d), per_b(N_MOD, d), _const_spec((1, d)), _const_spec(w_main.shape),
                _const_spec(w_gates.shape), _const_spec((1, LANES)), _const_spec(conv_w.shape),
                _const_spec((1, 2 * D_M)), per_b(CONV_W - 1, 2 * D_M)]
    out_shape = [bshape(D_M, BF16), bshape(D_M, BF16), bshape(D_M, BF16), bshape(D_M, F32),
                 bshape(LANES, F32), bshape(D_A, BF16), bshape(D_A, BF16), bshape(D_A, BF16),
                 jax.ShapeDtypeStruct((NB, tl, D_A), F32), jax.ShapeDtypeStruct((NB, tl, D_A), F32),
                 jax.ShapeDtypeStruct((NB, CONV_W - 1, 2 * D_M), F32)]
    out_specs = [row(D_M), row(D_M), row(D_M), row(D_M), row(LANES), row(D_A), row(D_A), row(D_A),
                 per_b(tl, D_A), per_b(tl, D_A), per_b(CONV_W - 1, 2 * D_M)]
    return pl.pallas_call(
        functools.partial(_proj_kernel, nb=nb, tl=tl),
        out_shape=out_shape,
        grid=grid,
        in_specs=in_specs,
        out_specs=out_specs,
        scratch_shapes=[pltpu.VMEM((m, d), BF16), pltpu.VMEM((nb, 8, 2 * D_M), F32)],
        compiler_params=_cparams(2),
        name="proj",
    )(x, mod, g_norm_row, w_main, w_gates, b_gates, conv_w, conv_b, conv_prev)


def _prefix_max_lanes(x):
    n = x.shape[-1]
    lane = lax.broadcasted_iota(jnp.int32, x.shape, 1)
    shift = 1
    while shift < n:
        x = jnp.maximum(x, jnp.where(lane >= shift, pltpu.roll(x, shift, 1), NEG))
        shift *= 2
    return x


def _split_bf16(x):
    hi = x.astype(BF16)
    r = x - hi.astype(F32)
    mid = r.astype(BF16)
    return hi, mid, (r - mid.astype(F32)).astype(BF16)


def _mlstm_kernel(q_ref, k_ref, v_ref, gates_ref, mo_ref, c0_ref, n0_ref, m0_ref, gm_ref,
                  hm_ref, c_ref, n_ref, m_ref, ct_ref, *, nbm, lc):
    j = pl.program_id(1)

    def transpose_states(src_ref, dst_ref):
        def one(i, carry):
            dst_ref[i // NH_M, i % NH_M] = jnp.transpose(src_ref[i // NH_M, i % NH_M])
            return carry
        lax.fori_loop(0, nbm * NH_M, one, 0)

    @pl.when(j == 0)
    def _():
        transpose_states(c0_ref, ct_ref)
        n_ref[...] = n0_ref[...]
        m_ref[...] = m0_ref[...]

    s_i = lax.broadcasted_iota(jnp.int32, (lc, lc), 0)
    t_i = lax.broadcasted_iota(jnp.int32, (lc, lc), 1)
    causal = s_i <= t_i
    upper = jnp.where(causal, 1.0, 0.0).astype(BF16)
    head_row = lax.broadcasted_iota(jnp.int32, (8, 1), 0)

    pairs = [(b, h) for b in range(nbm) for h in range(NH_M)]
    hsl = lambda h: slice(h * DH_M, (h + 1) * DH_M)

    rows = []
    for b in range(nbm):
        gates_t = jnp.transpose(gates_ref[b])
        ig_t = gates_t[0:8]
        lf_t = _log_sigmoid(gates_t[8:16])
        b_t = sum(_dot(part, upper) for part in _split_bf16(lf_t))
        c_t = ig_t - b_t
        c2_t = c_t * LOG2E
        c2_cols = jnp.transpose(jnp.concatenate([c2_t, jnp.zeros((LANES - 8, lc), F32)], axis=0))
        if lc % LANES == 0:
            cmax2_t = _prefix_max_lanes(c2_t)
        else:
            cmax2_t = jnp.concatenate(
                [jnp.max(jnp.where(causal, c2_cols[:, h:h + 1], NEG), axis=0, keepdims=True)
                 for h in range(8)], axis=0)
        b_end = b_t[:, lc - 1:lc]
        m_prev = m_ref[b][:, 0:1]
        log_g = b_end + c_t
        m_end = jnp.maximum(b_end + m_prev, jnp.max(log_g, axis=-1, keepdims=True))
        decay = jnp.exp(b_end + m_prev - m_end)
        g_rows = jnp.exp(log_g - m_end)
        log_inter = b_t + m_prev
        m_intra = b_t + cmax2_t * (1.0 / LOG2E)
        m_tok = jnp.maximum(log_inter, m_intra)
        rows.append(dict(decay=decay, g=g_rows, g_b=g_rows.astype(BF16), m_end=m_end,
                         c2_cols=c2_cols, cmax2=cmax2_t, r_intra=jnp.exp(m_intra - m_tok),
                         w_inter=jnp.exp(log_inter - m_tok), floor=jnp.exp(-m_tok),
                         n_prev_b=n_ref[b].astype(BF16)))

    early = {}
    for b, h in pairs:
        q = q_ref[b, :, hsl(h)]
        k = k_ref[b, :, hsl(h)]
        early[b, h] = dict(
            s_t=_dot_nt(k, q),
            cq_t=_dot_nt(ct_ref[b, h].astype(BF16), q),
            nq=_dot_nt(rows[b]["n_prev_b"], q)[h:h + 1, :],
            v_t=jnp.transpose(v_ref[b, :, hsl(h)].astype(F32)))

    n_upd = [jnp.zeros((8, DH_M), F32) for _ in range(nbm)]
    for b, h in pairs:
        r, e = rows[b], early[b, h]
        k = k_ref[b, :, hsl(h)]
        d_t = jnp.exp2(r["c2_cols"][:, h:h + 1] - r["cmax2"][h:h + 1, :])
        a_t = jnp.where(causal, d_t, 0.0) * e["s_t"]
        r_intra, w_inter = r["r_intra"][h:h + 1, :], r["w_inter"][h:h + 1, :]
        num_t = _dot(e["v_t"].astype(BF16), a_t.astype(BF16)) * r_intra + e["cq_t"] * w_inter
        den = jnp.sum(a_t, axis=0, keepdims=True) * r_intra + e["nq"] * w_inter
        h_t = num_t * (1.0 / jnp.maximum(jnp.abs(den), r["floor"][h:h + 1, :]))
        hn_t = h_t * lax.rsqrt(jnp.mean(h_t * h_t, axis=0, keepdims=True) + EPS)
        hn = jnp.transpose(hn_t) * gm_ref[:, hsl(h)]
        hm_ref[b, :, hsl(h)] = (_sigmoid(mo_ref[b, :, hsl(h)]) * hn).astype(BF16)

        vg_t = (e["v_t"] * r["g"][h:h + 1, :]).astype(BF16)
        ct_ref[b, h] = r["decay"][h:h + 1, :] * ct_ref[b, h] + _dot(vg_t, k)
        n_upd[b] = n_upd[b] + jnp.where(head_row == h, _dot(r["g_b"], k), 0.0)

    for b in range(nbm):
        n_ref[b] = rows[b]["decay"] * n_ref[b] + n_upd[b]
        m_ref[b] = jnp.broadcast_to(rows[b]["m_end"], (8, LANES))

    @pl.when(j == pl.num_programs(1) - 1)
    def _():
        transpose_states(ct_ref, c_ref)


def _mlstm(q, k, v, gates, mo, c0, n0, m0, g_mlstm_row, *, nbm, lc):
    NB, L, _ = q.shape
    grid = (NB // nbm, L // lc)
    row = lambda w: pl.BlockSpec((nbm, lc, w), lambda i, j: (i, j, 0))
    st = lambda s: pl.BlockSpec((nbm,) + s, lambda i, j: (i,) + (0,) * len(s))
    c_s, n_s = (NH_M, DH_M, DH_M), (8, DH_M)
    return pl.pallas_call(
        functools.partial(_mlstm_kernel, nbm=nbm, lc=lc),
        out_shape=[jax.ShapeDtypeStruct((NB, L, D_M), BF16),
                   jax.ShapeDtypeStruct((NB,) + c_s, F32),
                   jax.ShapeDtypeStruct((NB,) + n_s, F32),
                   jax.ShapeDtypeStruct((NB,) + n_s, F32)],
        grid=grid,
        in_specs=[row(D_M), row(D_M), row(D_M), row(LANES), row(D_M), st(c_s), st(n_s), st(n_s),
                  _const_spec((1, D_M))],
        out_specs=[row(D_M), st(c_s), st(n_s), st(n_s)],
        scratch_shapes=[pltpu.VMEM((nbm,) + c_s, F32)],
        compiler_params=_cparams(2),
        name="mlstm",
    )(q, k, v, gates, mo, c0, n0, m0, g_mlstm_row)


def _build_band_bias(ext_ref, tq, klen, put):
    p_len = ext_ref.shape[-1]
    q_pos = lax.broadcasted_iota(jnp.int32, (tq, klen), 0) + (klen - tq)
    k_pos = lax.broadcasted_iota(jnp.int32, (tq, klen), 1)
    back = q_pos // CHUNK - k_pos // CHUNK
    for h in range(NH_A):
        base = jnp.broadcast_to(ext_ref[h:h + 1, :] * LOG2E, (tq, p_len))
        toeplitz = pltpu.roll(base, 0, 1, stride=1, stride_axis=0)[:, :klen]
        put(h, jnp.where(back >= 0, jnp.where(back <= LEFT_CHUNKS, toeplitz, NEG), NEG))


def _lane_tiles(s):
    n = s.shape[-1]
    if n % LANES:
        return [s]
    return [s[:, i * LANES:(i + 1) * LANES] for i in range(n // LANES)]


def _row_reduce(tiles, op, lane_op):
    by_width = {}
    for t in tiles:
        by_width[t.shape[-1]] = t if t.shape[-1] not in by_width else op(by_width[t.shape[-1]], t)
    return functools.reduce(op, [lane_op(t, axis=-1, keepdims=True) for t in by_width.values()])


def _softmax_pv(ss, vs):
    mx = _row_reduce([t for s in ss for t in _lane_tiles(s)], jnp.maximum, jnp.max)
    es = [jnp.exp2(s - mx) for s in ss]
    l = _row_reduce([t for e in es for t in _lane_tiles(e)], jnp.add, jnp.sum)
    o = _dot(jnp.concatenate([e.astype(BF16) for e in es], axis=1), jnp.concatenate(vs, axis=0))
    return o / l


def _attn_pairs(jobs):
    first = lax.broadcasted_iota(jnp.int32, (1, LANES), 1) < DH_A
    zero = jnp.zeros((), BF16)

    def scores(q, segs, p):
        sl = slice(p * LANES, (p + 1) * LANES)
        qp = q(sl)
        q2 = jnp.concatenate([jnp.where(first, qp, zero), jnp.where(first, zero, qp)], axis=0)
        ss = []
        for k, _, bias, pen in segs:
            s = _dot_nt(q2, k(sl)) + bias(p)
            ss.append(s if pen is None else s + pen)
        return ss

    def finish(ss, segs, put, p):
        sl = slice(p * LANES, (p + 1) * LANES)
        r = ss[0].shape[0] // 2
        o2 = _softmax_pv(ss, [v(sl) for _, v, _, _ in segs])
        put(sl, jnp.where(first, o2[:r], o2[r:]))

    pending = None
    for q, segs, put in jobs:
        for p in range(NH_A // 2):
            ss = scores(q, segs, p)
            if pending is not None:
                finish(*pending)
            pending = (ss, segs, put, p)
    finish(*pending)


def _attn_prompt_kernel(q_ref, k0_ref, k1_ref, k2_ref, v0_ref, v1_ref, v2_ref, ext_ref, o_ref,
                        bias_ref):
    tq = q_ref.shape[1]
    hq = tq // 2
    j = pl.program_id(1)

    @pl.when((pl.program_id(0) == 0) & (j == 0))
    def _():
        def put_bias(h, tile):
            for half in range(2):
                bias_ref[h // 2, half, (h % 2) * hq:(h % 2 + 1) * hq, :] = tile[half * hq:(half + 1) * hq]
        _build_band_bias(ext_ref, tq, 3 * tq, put_bias)

    def run(pen0, pen1):
        jobs = []
        for half in range(2):
            lo = half * hq
            bias = lambda a, b, half=half: (lambda p: bias_ref[p, half, :, a:b])

            def put(sl, val, lo=lo):
                o_ref[0, lo:lo + hq, sl] = val.astype(o_ref.dtype)

            rows = lambda ref, a, b: (lambda sl: ref[0, a:b, sl])
            if half == 0:
                segs = [(rows(k0_ref, 0, tq), rows(v0_ref, 0, tq), bias(0, tq), pen0),
                        (rows(k1_ref, 0, tq), rows(v1_ref, 0, tq), bias(tq, 2 * tq), pen1),
                        (rows(k2_ref, 0, hq), rows(v2_ref, 0, hq), bias(2 * tq, 2 * tq + hq), None)]
            else:
                segs = [(rows(k0_ref, hq, tq), rows(v0_ref, hq, tq), bias(hq, tq), pen0),
                        (rows(k1_ref, 0, tq), rows(v1_ref, 0, tq), bias(tq, 2 * tq), pen1),
                        (rows(k2_ref, 0, tq), rows(v2_ref, 0, tq), bias(2 * tq, 3 * tq), None)]
            jobs.append((rows(q_ref, lo, lo + hq), segs, put))
        _attn_pairs(jobs)

    @pl.when(j >= 2)
    def _():
        run(None, None)

    @pl.when(j < 2)
    def _():
        run(jnp.where(j >= 2, 0.0, NEG).astype(F32), jnp.where(j >= 1, 0.0, NEG).astype(F32))


def _attn_prompt(aq, ak, av, ext):
    NB, L, _ = aq.shape
    tq = ATT_TQ
    grid = (NB, L // tq)
    row = pl.BlockSpec((1, tq, D_A), lambda b, j: (b, j, 0))
    back = lambda n: pl.BlockSpec((1, tq, D_A), lambda b, j: (b, jnp.maximum(j - n, 0), 0))
    return pl.pallas_call(
        _attn_prompt_kernel,
        out_shape=jax.ShapeDtypeStruct((NB, L, D_A), BF16),
        grid=grid,
        in_specs=[row, back(2), back(1), row, back(2), back(1), row, _const_spec(ext.shape)],
        out_specs=row,
        scratch_shapes=[pltpu.VMEM((NH_A // 2, 2, tq, 3 * tq), F32)],
        compiler_params=_cparams(2),
        name="attn_prompt",
    )(aq, ak, ak, ak, av, av, av, ext)


def _attn_sample_kernel(q_ref, k_ref, v_ref, ckt_ref, cvt_ref, ext_ref, o_ref, bias_ref, *, nb):
    tq = q_ref.shape[1]
    nc = ckt_ref.shape[-1]

    @pl.when(pl.program_id(0) == 0)
    def _():
        def put_bias(h, tile):
            bias_ref[h] = tile
        _build_band_bias(ext_ref, tq, nc + tq, put_bias)

    def scores(b, h):
        hs = slice(h * DH_A, (h + 1) * DH_A)
        qh = q_ref[b, :, hs]
        return [_dot(qh, ckt_ref[b, h].astype(BF16)) + bias_ref[h, :, 0:nc],
                _dot_nt(qh, k_ref[b, :, hs]) + bias_ref[h, :, nc:nc + tq]]

    def finish(b, h, ss):
        hs = slice(h * DH_A, (h + 1) * DH_A)
        mx = jnp.maximum(*[jnp.max(s, axis=-1, keepdims=True) for s in ss])
        e_c, e_o = [jnp.exp2(s - mx) for s in ss]
        l = jnp.sum(e_c, axis=-1, keepdims=True) + jnp.sum(e_o, axis=-1, keepdims=True)
        o = (_dot_nt(e_c.astype(BF16), cvt_ref[b, h].astype(BF16))
             + _dot(e_o.astype(BF16), v_ref[b, :, hs]))
        return o / l

    for b in range(nb):
        outs = []
        pending = None
        for h in range(NH_A):
            ss = scores(b, h)
            if pending is not None:
                outs.append(finish(b, *pending))
            pending = (h, ss)
        outs.append(finish(b, *pending))
        o_ref[b] = jnp.concatenate(outs, axis=-1).astype(o_ref.dtype)


def _attn_sample(aq, ak, av, cache_kt, cache_vt, ext, *, nb):
    NB, L, _ = aq.shape
    nc = cache_kt.shape[-1]
    row = pl.BlockSpec((nb, L, D_A), lambda i: (i, 0, 0))
    crow = pl.BlockSpec((nb, NH_A, DH_A, nc), lambda i: (i, 0, 0, 0))
    return pl.pallas_call(
        functools.partial(_attn_sample_kernel, nb=nb),
        out_shape=jax.ShapeDtypeStruct((NB, L, D_A), BF16),
        grid=(NB // nb,),
        in_specs=[row, row, row, crow, crow, _const_spec(ext.shape)],
        out_specs=row,
        scratch_shapes=[pltpu.VMEM((NH_A, L, nc + L), F32)],
        compiler_params=_cparams(1),
        name="attn_sample",
    )(aq, ak, av, cache_kt, cache_vt, ext)


def _rel_ext(rel_table, tq, klen):
    p_len = -(-(klen + tq) // LANES) * LANES
    n_lo = tq - CHUNK
    n_hi = p_len - tq - MAX_REL
    rep = lambda col, n: jnp.broadcast_to(col, (NH_A, n))
    by_dist = jnp.concatenate([rep(rel_table[:, :1], n_lo), rel_table, rep(rel_table[:, -1:], n_hi)],
                              axis=1)
    rev = by_dist[:, ::-1]
    n = p_len - klen
    return jnp.concatenate([rev[:, n:], rev[:, :n]], axis=1).astype(F32)


def _gate_lanes(g):
    z = lambda n: jnp.zeros((g.shape[0], n), g.dtype)
    return jnp.concatenate([g[:, :NH_M], z(8 - NH_M), g[:, NH_M:], z(LANES - 8 - NH_M)], axis=1)


def _layer(x, mod, conv_prev, state, cache, w, *, nb, tl, nbm, lc, final):
    NB, L, d = x.shape
    x1 = _ffn(x, mod, w["g0"], w["up1"], w["dn1"], w["g_final"], nb=nb, tl=tl, sub=0)
    (q, k, v, mo, gates, aq, ak, av, k_tail, v_tail, conv_new) = _proj(
        x1, mod, w["g1"], w["w_main"], w["w_gates"], w["b_gates"], w["conv_w"], w["conv_b"],
        conv_prev, nb=nb, tl=tl)
    c0, n0, m0 = state
    n0 = jnp.pad(n0, ((0, 0), (0, 8 - NH_M), (0, 0)))
    m0 = jnp.broadcast_to(jnp.pad(m0, ((0, 0), (0, 8 - NH_M)))[:, :, None], (NB, 8, LANES))
    hm, c_new, n_new, m_new = _mlstm(q, k, v, gates, mo, c0, n0, m0, w["g_mlstm"], nbm=nbm, lc=lc)
    if cache is None:
        att = _attn_prompt(aq, ak, av, w["ext_prompt"])
    else:
        att = _attn_sample(aq, ak, av, cache[0], cache[1], w["ext_sample"], nb=min(NB, 4))
    y = _ffn(x1, mod, w["g2"], w["up2"], w["dn2"], w["g_final"], nb=nb, tl=tl, sub=2,
             mix=(hm, att, w["w_out"]), final=final)
    n_keep = k_tail.shape[1]
    states = (c_new, n_new[:, :NH_M, :], m_new[:, :NH_M, 0], conv_new,
              k_tail.reshape(NB, n_keep, NH_A, DH_A), v_tail.reshape(NB, n_keep, NH_A, DH_A))
    return y, states


def kernel(x_prompt, x_sample, state_mlstm_C, state_mlstm_n, state_mlstm_m, state_conv, cache_att_k, cache_att_v, c_prompt, c_sample, w_ada, b_ada, g_norm, w_ffn1_up, w_ffn1_down, w_ffn2_up, w_ffn2_down, w_in, conv_w, conv_b, b_gates, g_mlstm, rel_bias_table, w_out, g_final):
    depth = w_ada.shape[0]
    bp, seq, d = x_prompt.shape
    bs, dseq, _ = x_sample.shape
    xp, xs = x_prompt, x_sample
    st_p, st_s = [], []
    n_c = cache_att_k.shape[2]
    tl_p = min(ATT_WINDOW, seq)
    lc_p = min(MLSTM_CHUNK, seq)
    nb_s = max(1, min(bs, ATT_WINDOW // dseq))
    for l in range(depth):
        rows = bp + bs
        pad = (-rows) % 8
        c_all = jnp.concatenate([c_prompt, c_sample, jnp.zeros((pad, d), F32)], axis=0)
        mod = _adaln(c_all, w_ada[l], b_ada[l])
        mod_p = mod[:bp].reshape(bp, N_MOD, d)
        mod_s = mod[bp:rows].reshape(bs, N_MOD, d)
        off_g = 4 * D_M
        off_a = off_g + 2 * NH_M
        wl = w_in[l]
        rel = rel_bias_table[l]
        w = {
            "g0": g_norm[l, 0:1], "g1": g_norm[l, 1:2], "g2": g_norm[l, 2:3],
            "g_final": g_final.reshape(1, d),
            "up1": w_ffn1_up[l].astype(BF16), "dn1": w_ffn1_down[l].astype(BF16),
            "up2": w_ffn2_up[l].astype(BF16), "dn2": w_ffn2_down[l].astype(BF16),
            "w_main": jnp.concatenate([wl[:, :off_g], wl[:, off_a:]], axis=1).astype(BF16),
            "w_gates": _gate_lanes(wl[:, off_g:off_a]).astype(BF16),
            "b_gates": _gate_lanes(b_gates[l].reshape(1, 2 * NH_M)),
            "conv_w": conv_w[l], "conv_b": conv_b[l].reshape(1, 2 * D_M),
            "g_mlstm": g_mlstm[l].reshape(1, D_M),
            "w_out": w_out[l].astype(BF16),
            "ext_prompt": _rel_ext(rel, ATT_TQ, 3 * ATT_TQ),
            "ext_sample": _rel_ext(rel, dseq, n_c + dseq),
        }
        zero_state = (jnp.zeros((bp, NH_M, DH_M, DH_M), F32), jnp.zeros((bp, NH_M, DH_M), F32),
                      jnp.zeros((bp, NH_M), F32))
        xp, sp = _layer(xp, mod_p, jnp.zeros((bp, CONV_W - 1, 2 * D_M), F32), zero_state, None, w,
                        nb=1, tl=tl_p, nbm=bp, lc=lc_p, final=l == depth - 1)
        state_s = (state_mlstm_C[l], state_mlstm_n[l], state_mlstm_m[l])
        cache = (jnp.transpose(cache_att_k[l], (0, 2, 3, 1)), jnp.transpose(cache_att_v[l], (0, 2, 3, 1)))
        xs, ss = _layer(xs, mod_s, state_conv[l], state_s, cache, w, nb=nb_s, tl=dseq,
                        nbm=min(bs, 4), lc=dseq, final=l == depth - 1)
        st_p.append(sp)
        st_s.append(ss)
    stk = lambda sts, i: jnp.stack([s[i] for s in sts])
    return ((xp, xs) + tuple(stk(st_p, i) for i in range(6)) + tuple(stk(st_s, i) for i in range(6)))
```

```python
import functools

import jax
import jax.numpy as jnp
from jax import lax
from jax.experimental import pallas as pl
from jax.experimental.pallas import tpu as pltpu

F32 = jnp.float32
BF16 = jnp.bfloat16

CHUNK = 64
NH_M = 4
DH_M = 128
D_M = NH_M * DH_M
NH_A = 8
DH_A = 64
D_A = NH_A * DH_A
CONV_W = 4
LEFT_CHUNKS = 8
ATT_WINDOW = LEFT_CHUNKS * CHUNK
MAX_REL = 2 * CHUNK
N_MOD = 9
EPS = 1e-6
NEG = -1e30
LOG2E = 1.4426950408889634

LANES = 128
V7X_VMEM_BYTES = 64 * 1024 * 1024
VMEM_LIMIT = V7X_VMEM_BYTES - 8 * 1024 * 1024

FF_CHUNK = 256
MLSTM_CHUNK = 256
ATT_TQ = 256


def _cparams(n_axes):
    return pltpu.CompilerParams(dimension_semantics=("arbitrary",) * n_axes,
                                vmem_limit_bytes=VMEM_LIMIT)


def _const_spec(shape):
    nd = len(shape)
    return pl.BlockSpec(shape, lambda *_: (0,) * nd, pipeline_mode=pl.Buffered(1))


def _dot(a, b):
    return jnp.dot(a, b, preferred_element_type=F32)


def _dot_nt(a, b):
    return lax.dot_general(a, b, (((1,), (1,)), ((), ())), preferred_element_type=F32)


def _dot_tn(a, b):
    return lax.dot_general(a, b, (((0,), (0,)), ((), ())), preferred_element_type=F32)


def _sigmoid(x):
    return 1.0 / (1.0 + jnp.exp(-x))


def _log_sigmoid(x):
    return jnp.minimum(x, 0.0) - jnp.log(1.0 + jnp.exp(-jnp.abs(x)))


def _norm_mod(x, g, shift, scale):
    ms = jnp.mean(x * x, axis=-1, keepdims=True)
    h = x * lax.rsqrt(ms + EPS) * g
    return h * (1.0 + scale) + shift


def _adaln_kernel(c_ref, w_ref, b_ref, o_ref):
    c = c_ref[...]
    o_ref[...] = _dot(c * _sigmoid(c), w_ref[...]) + b_ref[...]


def _adaln(c_all, w_ada, b_ada):
    rows, d = c_all.shape
    n = w_ada.shape[1]
    tn = d
    return pl.pallas_call(
        _adaln_kernel,
        out_shape=jax.ShapeDtypeStruct((rows, n), F32),
        grid=(n // tn,),
        in_specs=[pl.BlockSpec((rows, d), lambda j: (0, 0)),
                  pl.BlockSpec((d, tn), lambda j: (0, j)),
                  pl.BlockSpec((1, tn), lambda j: (0, j))],
        out_specs=pl.BlockSpec((rows, tn), lambda j: (0, j)),
        compiler_params=_cparams(1),
        name="adaln",
    )(c_all, w_ada, b_ada.reshape(1, n))


def _ffn_kernel(*refs, nb, tl, sub, with_mix, with_final):
    if with_mix:
        x_ref, hm_ref, att_ref, mod_ref, gn_ref, wout_ref, wup_ref, wdn_ref, gfin_ref, o_ref, hb_ref, act_ref = refs
    else:
        x_ref, mod_ref, gn_ref, wup_ref, wdn_ref, gfin_ref, o_ref, hb_ref, act_ref = refs
    m = nb * tl
    d = x_ref.shape[-1]
    dff = wdn_ref.shape[0]
    mod = mod_ref[...]
    shift, scale, gate = (mod[:, 3 * sub + i:3 * sub + i + 1, :] for i in range(3))

    if with_mix:
        mix = _dot(hm_ref[...].reshape(m, D_M), wout_ref[0:D_M, :])
        mix = mix + _dot(att_ref[...].reshape(m, D_A), wout_ref[D_M:, :])
        o_ref[...] = x_ref[...] + mod[:, 5:6, :] * mix.reshape(nb, tl, d)
    else:
        o_ref[...] = x_ref[...]

    h = _norm_mod(o_ref[...], gn_ref[...], shift, scale)
    hb_ref[...] = h.reshape(m, d).astype(BF16)
    for c in range(dff // FF_CHUNK):
        lo = c * FF_CHUNK
        g = _dot(hb_ref[...], wup_ref[:, lo:lo + FF_CHUNK])
        u = _dot(hb_ref[...], wup_ref[:, dff + lo:dff + lo + FF_CHUNK])
        act_ref[:, lo:lo + FF_CHUNK] = (g * _sigmoid(g) * u).astype(BF16)
    dn = _dot(act_ref[...], wdn_ref[...])
    y = o_ref[...] + 0.5 * gate * dn.reshape(nb, tl, d)
    if with_final:
        ms = jnp.mean(y * y, axis=-1, keepdims=True)
        y = y * lax.rsqrt(ms + EPS) * gfin_ref[...]
    o_ref[...] = y


def _ffn(x, mod, g_norm_row, w_up, w_dn, g_final, *, nb, tl, sub, mix=None, final=False):
    NB, L, d = x.shape
    dff = w_dn.shape[0]
    m = nb * tl
    grid = (NB // nb, L // tl)
    row = lambda w: pl.BlockSpec((nb, tl, w), lambda i, j: (i, j, 0))
    in_specs = [row(d)]
    args = [x]
    if mix is not None:
        hm, att, w_out = mix
        in_specs += [row(D_M), row(D_A)]
        args += [hm, att]
    in_specs += [pl.BlockSpec((nb, N_MOD, d), lambda i, j: (i, 0, 0)), _const_spec((1, d))]
    args += [mod, g_norm_row]
    if mix is not None:
        in_specs.append(_const_spec(w_out.shape))
        args.append(w_out)
    in_specs += [_const_spec(w_up.shape), _const_spec(w_dn.shape), _const_spec((1, d))]
    args += [w_up, w_dn, g_final]
    kern = functools.partial(_ffn_kernel, nb=nb, tl=tl, sub=sub, with_mix=mix is not None,
                             with_final=final)
    return pl.pallas_call(
        kern,
        out_shape=jax.ShapeDtypeStruct((NB, L, d), F32),
        grid=grid,
        in_specs=in_specs,
        out_specs=row(d),
        scratch_shapes=[pltpu.VMEM((m, d), BF16), pltpu.VMEM((m, dff), BF16)],
        compiler_params=_cparams(2),
        name="ffn%d" % sub,
    )(*args)


def _proj_kernel(x_ref, mod_ref, gn_ref, w_ref, wg_ref, bg_ref, cw_ref, cb_ref, cprev_ref,
                 q_ref, k_ref, v_ref, o_ref, gates_ref, aq_ref, ak_ref, av_ref,
                 kt_ref, vt_ref, cnew_ref, hb_ref, carry_ref, *, nb, tl):
    m = nb * tl
    d = x_ref.shape[-1]
    j = pl.program_id(1)
    mod = mod_ref[...]
    h = _norm_mod(x_ref[...], gn_ref[...], mod[:, 3:4, :], mod[:, 4:5, :])
    hb_ref[...] = h.reshape(m, d).astype(BF16)

    @pl.when(j == 0)
    def _():
        carry_ref[...] = jnp.zeros(carry_ref.shape, F32)
        carry_ref[:, 8 - (CONV_W - 1):8, :] = cprev_ref[...]

    def qk_pre(c):
        lo = c * FF_CHUNK
        return _dot(hb_ref[...], w_ref[:, lo:lo + FF_CHUNK]).reshape(nb, tl, FF_CHUNK)

    def conv(c, pre):
        lo = c * FF_CHUNK
        ext = jnp.concatenate([carry_ref[:, :, lo:lo + FF_CHUNK], pre], axis=1)
        acc = cb_ref[:, lo:lo + FF_CHUNK]
        for t in reversed(range(CONV_W)):
            s0 = 8 - (CONV_W - 1) + t
            acc = acc + cw_ref[t:t + 1, lo:lo + FF_CHUNK] * ext[:, s0:s0 + tl, :]
        y = acc * _sigmoid(acc)
        if lo < D_M:
            q_ref[:, :, lo:lo + FF_CHUNK] = y.astype(BF16)
        else:
            k_ref[:, :, lo - D_M:lo - D_M + FF_CHUNK] = (y * DH_M ** -0.5).astype(BF16)
        cnew_ref[:, :, lo:lo + FF_CHUNK] = pre[:, tl - (CONV_W - 1):tl, :]
        carry_ref[:, :, lo:lo + FF_CHUNK] = pre[:, tl - 8:tl, :]

    def cols(c0):
        return _dot(hb_ref[...], w_ref[:, c0:c0 + D_M]).reshape(nb, tl, D_M)

    pre = qk_pre(0)
    v_ref[...] = cols(2 * D_M).astype(BF16)
    conv(0, pre)
    pre = qk_pre(1)
    o_ref[...] = cols(3 * D_M)
    conv(1, pre)
    pre = qk_pre(2)
    aq_ref[...] = (cols(4 * D_M) * (DH_A ** -0.5 * LOG2E)).astype(BF16)
    conv(2, pre)
    pre = qk_pre(3)
    ak = cols(4 * D_M + D_A)
    ak_ref[...] = ak.astype(BF16)
    kt_ref[...] = ak
    conv(3, pre)
    av = cols(4 * D_M + 2 * D_A)
    av_ref[...] = av.astype(BF16)
    vt_ref[...] = av
    gates_ref[...] = (_dot(hb_ref[...], wg_ref[...]) + bg_ref[...]).reshape(nb, tl, LANES)


def _proj(x, mod, g_norm_row, w_main, w_gates, b_gates, conv_w, conv_b, conv_prev, *, nb, tl):
    NB, L, d = x.shape
    assert tl == min(ATT_WINDOW, L)
    m = nb * tl
    grid = (NB // nb, L // tl)
    row = lambda w: pl.BlockSpec((nb, tl, w), lambda i, j: (i, j, 0))
    per_b = lambda r, w: pl.BlockSpec((nb, r, w), lambda i, j: (i, 0, 0))
    bshape = lambda w, dt: jax.ShapeDtypeStruct((NB, L, w), dt)
    in_specs = [row(d), per_b(N_MOD, d), _const_spec((1, d)), _const_spec(w_main.shape),
                _const_spec(w_gates.shape), _const_spec((1, LANES)), _const_spec(conv_w.shape),
                _const_spec((1, 2 * D_M)), per_b(CONV_W - 1, 2 * D_M)]
    out_shape = [bshape(D_M, BF16), bshape(D_M, BF16), bshape(D_M, BF16), bshape(D_M, F32),
                 bshape(LANES, F32), bshape(D_A, BF16), bshape(D_A, BF16), bshape(D_A, BF16),
                 jax.ShapeDtypeStruct((NB, tl, D_A), F32), jax.ShapeDtypeStruct((NB, tl, D_A), F32),
                 jax.ShapeDtypeStruct((NB, CONV_W - 1, 2 * D_M), F32)]
    out_specs = [row(D_M), row(D_M), row(D_M), row(D_M), row(LANES), row(D_A), row(D_A), row(D_A),
                 per_b(tl, D_A), per_b(tl, D_A), per_b(CONV_W - 1, 2 * D_M)]
    return pl.pallas_call(
        functools.partial(_proj_kernel, nb=nb, tl=tl),
        out_shape=out_shape,
        grid=grid,
        in_specs=in_specs,
        out_specs=out_specs,
        scratch_shapes=[pltpu.VMEM((m, d), BF16), pltpu.VMEM((nb, 8, 2 * D_M), F32)],
        compiler_params=_cparams(2),
        name="proj",
    )(x, mod, g_norm_row, w_main, w_gates, b_gates, conv_w, conv_b, conv_prev)


def _prefix_max_lanes(x):
    n = x.shape[-1]
    lane = lax.broadcasted_iota(jnp.int32, x.shape, 1)
    shift = 1
    while shift < n:
        x = jnp.maximum(x, jnp.where(lane >= shift, pltpu.roll(x, shift, 1), NEG))
        shift *= 2
    return x


def _split_bf16(x):
    hi = x.astype(BF16)
    r = x - hi.astype(F32)
    mid = r.astype(BF16)
    return hi, mid, (r - mid.astype(F32)).astype(BF16)


def _mlstm_kernel(q_ref, k_ref, v_ref, gates_ref, mo_ref, c0_ref, n0_ref, m0_ref, gm_ref,
                  hm_ref, c_ref, n_ref, m_ref, ct_ref, *, nbm, lc, single_chunk):
    j = pl.program_id(1)

    def transpose_states(src_ref, dst_ref):
        def one(i, carry):
            dst_ref[i // NH_M, i % NH_M] = jnp.transpose(src_ref[i // NH_M, i % NH_M])
            return carry
        lax.fori_loop(0, nbm * NH_M, one, 0)

    if single_chunk:
        n_ref[...] = n0_ref[...]
        m_ref[...] = m0_ref[...]
        ct_get = lambda b, h: jnp.transpose(c0_ref[b, h])

        def ct_put(b, h, val):
            c_ref[b, h] = jnp.transpose(val)
    else:
        @pl.when(j == 0)
        def _():
            transpose_states(c0_ref, ct_ref)
            n_ref[...] = n0_ref[...]
            m_ref[...] = m0_ref[...]
        ct_get = lambda b, h: ct_ref[b, h]

        def ct_put(b, h, val):
            ct_ref[b, h] = val

    s_i = lax.broadcasted_iota(jnp.int32, (lc, lc), 0)
    t_i = lax.broadcasted_iota(jnp.int32, (lc, lc), 1)
    causal = s_i <= t_i
    upper = jnp.where(causal, 1.0, 0.0).astype(BF16)
    head_row = lax.broadcasted_iota(jnp.int32, (8, 1), 0)

    pairs = [(b, h) for b in range(nbm) for h in range(NH_M)]
    hsl = lambda h: slice(h * DH_M, (h + 1) * DH_M)

    rows = []
    for b in range(nbm):
        gates_t = jnp.transpose(gates_ref[b])
        ig_t = gates_t[0:8]
        lf_t = _log_sigmoid(gates_t[8:16])
        b_t = sum(_dot(part, upper) for part in _split_bf16(lf_t))
        c_t = ig_t - b_t
        b_end = b_t[:, lc - 1:lc]
        m_prev = m_ref[b][:, 0:1]
        log_g = b_end + c_t
        m_end = jnp.maximum(b_end + m_prev, jnp.max(log_g, axis=-1, keepdims=True))
        decay = jnp.exp(b_end + m_prev - m_end)
        g_rows = jnp.exp(log_g - m_end)
        log_inter = b_t + m_prev
        c2_t = c_t * LOG2E
        c2_cols = jnp.transpose(jnp.concatenate([c2_t, jnp.zeros((LANES - 8, lc), F32)], axis=0))
        if lc % LANES == 0:
            cmax2_t = _prefix_max_lanes(c2_t)
        else:
            cmax2_t = jnp.concatenate(
                [jnp.max(jnp.where(causal, c2_cols[:, h:h + 1], NEG), axis=0, keepdims=True)
                 for h in range(8)], axis=0)
        m_intra = b_t + cmax2_t * (1.0 / LOG2E)
        m_tok = jnp.maximum(log_inter, m_intra)
        rows.append(dict(decay=decay, g=g_rows, g_b=g_rows.astype(BF16), m_end=m_end,
                         c2_cols=c2_cols, cmax2=cmax2_t, r_intra=jnp.exp(m_intra - m_tok),
                         w_inter=jnp.exp(log_inter - m_tok), floor=jnp.exp(-m_tok),
                         n_prev_b=n_ref[b].astype(BF16)))

    early = {}
    for b, h in pairs:
        q = q_ref[b, :, hsl(h)]
        k = k_ref[b, :, hsl(h)]
        ct_prev = ct_get(b, h)
        early[b, h] = dict(
            ct=ct_prev if single_chunk else None,
            s_t=_dot_nt(k, q),
            cq_t=_dot_nt(ct_prev.astype(BF16), q),
            nq=_dot_nt(rows[b]["n_prev_b"], q)[h:h + 1, :],
            v_t=jnp.transpose(v_ref[b, :, hsl(h)].astype(F32)))

    n_upd = [jnp.zeros((8, DH_M), F32) for _ in range(nbm)]
    for b, h in pairs:
        r, e = rows[b], early[b, h]
        k = k_ref[b, :, hsl(h)]
        d_t = jnp.exp2(r["c2_cols"][:, h:h + 1] - r["cmax2"][h:h + 1, :])
        a_t = jnp.where(causal, d_t, 0.0) * e["s_t"]
        r_intra, w_inter = r["r_intra"][h:h + 1, :], r["w_inter"][h:h + 1, :]
        num_t = _dot(e["v_t"].astype(BF16), a_t.astype(BF16)) * r_intra + e["cq_t"] * w_inter
        den = jnp.sum(a_t, axis=0, keepdims=True) * r_intra + e["nq"] * w_inter
        h_t = num_t * (1.0 / jnp.maximum(jnp.abs(den), r["floor"][h:h + 1, :]))
        hn_t = h_t * lax.rsqrt(jnp.mean(h_t * h_t, axis=0, keepdims=True) + EPS)
        hn = jnp.transpose(hn_t) * gm_ref[:, hsl(h)]
        hm_ref[b, :, hsl(h)] = (_sigmoid(mo_ref[b, :, hsl(h)]) * hn).astype(BF16)

        vg_t = (e["v_t"] * r["g"][h:h + 1, :]).astype(BF16)
        ct_prev = e["ct"] if single_chunk else ct_get(b, h)
        ct_put(b, h, r["decay"][h:h + 1, :] * ct_prev + _dot(vg_t, k))
        n_upd[b] = n_upd[b] + jnp.where(head_row == h, _dot(r["g_b"], k), 0.0)

    for b in range(nbm):
        n_ref[b] = rows[b]["decay"] * n_ref[b] + n_upd[b]
        m_ref[b] = jnp.broadcast_to(rows[b]["m_end"], (8, LANES))

    if not single_chunk:
        @pl.when(j == pl.num_programs(1) - 1)
        def _():
            transpose_states(ct_ref, c_ref)


def _mlstm(q, k, v, gates, mo, c0, n0, m0, g_mlstm_row, *, nbm, lc):
    NB, L, _ = q.shape
    grid = (NB // nbm, L // lc)
    row = lambda w: pl.BlockSpec((nbm, lc, w), lambda i, j: (i, j, 0))
    st = lambda s: pl.BlockSpec((nbm,) + s, lambda i, j: (i,) + (0,) * len(s))
    c_s, n_s = (NH_M, DH_M, DH_M), (8, DH_M)
    return pl.pallas_call(
        functools.partial(_mlstm_kernel, nbm=nbm, lc=lc, single_chunk=L == lc),
        out_shape=[jax.ShapeDtypeStruct((NB, L, D_M), BF16),
                   jax.ShapeDtypeStruct((NB,) + c_s, F32),
                   jax.ShapeDtypeStruct((NB,) + n_s, F32),
                   jax.ShapeDtypeStruct((NB,) + n_s, F32)],
        grid=grid,
        in_specs=[row(D_M), row(D_M), row(D_M), row(LANES), row(D_M), st(c_s), st(n_s), st(n_s),
                  _const_spec((1, D_M))],
        out_specs=[row(D_M), st(c_s), st(n_s), st(n_s)],
        scratch_shapes=[pltpu.VMEM((nbm,) + c_s, F32)],
        compiler_params=_cparams(2),
        name="mlstm",
    )(q, k, v, gates, mo, c0, n0, m0, g_mlstm_row)


def _build_band_bias(ext_ref, tq, klen, put):
    p_len = ext_ref.shape[-1]
    q_pos = lax.broadcasted_iota(jnp.int32, (tq, klen), 0) + (klen - tq)
    k_pos = lax.broadcasted_iota(jnp.int32, (tq, klen), 1)
    back = q_pos // CHUNK - k_pos // CHUNK
    for h in range(NH_A):
        base = jnp.broadcast_to(ext_ref[h:h + 1, :] * LOG2E, (tq, p_len))
        toeplitz = pltpu.roll(base, 0, 1, stride=1, stride_axis=0)[:, :klen]
        put(h, jnp.where(back >= 0, jnp.where(back <= LEFT_CHUNKS, toeplitz, NEG), NEG))


def _lane_tiles(s):
    n = s.shape[-1]
    if n % LANES:
        return [s]
    return [s[:, i * LANES:(i + 1) * LANES] for i in range(n // LANES)]


def _row_reduce(tiles, op, lane_op):
    by_width = {}
    for t in tiles:
        by_width[t.shape[-1]] = t if t.shape[-1] not in by_width else op(by_width[t.shape[-1]], t)
    return functools.reduce(op, [lane_op(t, axis=-1, keepdims=True) for t in by_width.values()])


def _softmax_pv(ss, vs):
    mx = _row_reduce([t for s in ss for t in _lane_tiles(s)], jnp.maximum, jnp.max)
    es = [jnp.exp2(s - mx) for s in ss]
    l = _row_reduce([t for e in es for t in _lane_tiles(e)], jnp.add, jnp.sum)
    o = _dot(jnp.concatenate([e.astype(BF16) for e in es], axis=1), jnp.concatenate(vs, axis=0))
    return o / l


def _attn_pairs(jobs):
    first = lax.broadcasted_iota(jnp.int32, (1, LANES), 1) < DH_A
    zero = jnp.zeros((), BF16)

    def scores(q, segs, p):
        sl = slice(p * LANES, (p + 1) * LANES)
        qp = q(sl)
        q2 = jnp.concatenate([jnp.where(first, qp, zero), jnp.where(first, zero, qp)], axis=0)
        ss = []
        for k, _, bias, pen in segs:
            s = _dot_nt(q2, k(sl)) + bias(p)
            ss.append(s if pen is None else s + pen)
        return ss

    def finish(ss, segs, put, p):
        sl = slice(p * LANES, (p + 1) * LANES)
        r = ss[0].shape[0] // 2
        o2 = _softmax_pv(ss, [v(sl) for _, v, _, _ in segs])
        put(sl, jnp.where(first, o2[:r], o2[r:]))

    pending = None
    for q, segs, put in jobs:
        for p in range(NH_A // 2):
            ss = scores(q, segs, p)
            if pending is not None:
                finish(*pending)
            pending = (ss, segs, put, p)
    finish(*pending)


def _attn_prompt_kernel(q_ref, k0_ref, k1_ref, k2_ref, v0_ref, v1_ref, v2_ref, ext_ref, o_ref,
                        bias_ref):
    tq = q_ref.shape[1]
    hq = tq // 2
    j = pl.program_id(1)

    @pl.when((pl.program_id(0) == 0) & (j == 0))
    def _():
        def put_bias(h, tile):
            for half in range(2):
                bias_ref[h // 2, half, (h % 2) * hq:(h % 2 + 1) * hq, :] = tile[half * hq:(half + 1) * hq]
        _build_band_bias(ext_ref, tq, 3 * tq, put_bias)

    def run(pen0, pen1):
        jobs = []
        for half in range(2):
            lo = half * hq
            bias = lambda a, b, half=half: (lambda p: bias_ref[p, half, :, a:b])

            def put(sl, val, lo=lo):
                o_ref[0, lo:lo + hq, sl] = val.astype(o_ref.dtype)

            rows = lambda ref, a, b: (lambda sl: ref[0, a:b, sl])
            if half == 0:
                segs = [(rows(k0_ref, 0, tq), rows(v0_ref, 0, tq), bias(0, tq), pen0),
                        (rows(k1_ref, 0, tq), rows(v1_ref, 0, tq), bias(tq, 2 * tq), pen1),
                        (rows(k2_ref, 0, hq), rows(v2_ref, 0, hq), bias(2 * tq, 2 * tq + hq), None)]
            else:
                segs = [(rows(k0_ref, hq, tq), rows(v0_ref, hq, tq), bias(hq, tq), pen0),
                        (rows(k1_ref, 0, tq), rows(v1_ref, 0, tq), bias(tq, 2 * tq), pen1),
                        (rows(k2_ref, 0, tq), rows(v2_ref, 0, tq), bias(2 * tq, 3 * tq), None)]
            jobs.append((rows(q_ref, lo, lo + hq), segs, put))
        _attn_pairs(jobs)

    @pl.when(j >= 2)
    def _():
        run(None, None)

    @pl.when(j < 2)
    def _():
        run(jnp.where(j >= 2, 0.0, NEG).astype(F32), jnp.where(j >= 1, 0.0, NEG).astype(F32))


def _attn_prompt(aq, ak, av, ext):
    NB, L, _ = aq.shape
    tq = ATT_TQ
    grid = (NB, L // tq)
    row = pl.BlockSpec((1, tq, D_A), lambda b, j: (b, j, 0))
    back = lambda n: pl.BlockSpec((1, tq, D_A), lambda b, j: (b, jnp.maximum(j - n, 0), 0))
    return pl.pallas_call(
        _attn_prompt_kernel,
        out_shape=jax.ShapeDtypeStruct((NB, L, D_A), BF16),
        grid=grid,
        in_specs=[row, back(2), back(1), row, back(2), back(1), row, _const_spec(ext.shape)],
        out_specs=row,
        scratch_shapes=[pltpu.VMEM((NH_A // 2, 2, tq, 3 * tq), F32)],
        compiler_params=_cparams(2),
        name="attn_prompt",
    )(aq, ak, ak, ak, av, av, av, ext)


def _attn_sample_kernel(q_ref, k_ref, v_ref, ckt_ref, cvt_ref, ext_ref, o_ref, bias_ref, *, nb):
    tq = q_ref.shape[1]
    nc = ckt_ref.shape[-1]

    @pl.when(pl.program_id(0) == 0)
    def _():
        def put_bias(h, tile):
            bias_ref[h] = tile
        _build_band_bias(ext_ref, tq, nc + tq, put_bias)

    def scores(b, h):
        hs = slice(h * DH_A, (h + 1) * DH_A)
        qh = q_ref[b, :, hs]
        return [_dot(qh, ckt_ref[b, h].astype(BF16)) + bias_ref[h, :, 0:nc],
                _dot_nt(qh, k_ref[b, :, hs]) + bias_ref[h, :, nc:nc + tq]]

    def finish(b, h, ss):
        hs = slice(h * DH_A, (h + 1) * DH_A)
        mx = jnp.maximum(*[jnp.max(s, axis=-1, keepdims=True) for s in ss])
        e_c, e_o = [jnp.exp2(s - mx) for s in ss]
        l = jnp.sum(e_c, axis=-1, keepdims=True) + jnp.sum(e_o, axis=-1, keepdims=True)
        o = (_dot_nt(e_c.astype(BF16), cvt_ref[b, h].astype(BF16))
             + _dot(e_o.astype(BF16), v_ref[b, :, hs]))
        return o / l

    for b in range(nb):
        outs = []
        pending = None
        for h in range(NH_A):
            ss = scores(b, h)
            if pending is not None:
                outs.append(finish(b, *pending))
            pending = (h, ss)
        outs.append(finish(b, *pending))
        o_ref[b] = jnp.concatenate(outs, axis=-1).astype(o_ref.dtype)


def _attn_sample(aq, ak, av, cache_kt, cache_vt, ext, *, nb):
    NB, L, _ = aq.shape
    nc = cache_kt.shape[-1]
    row = pl.BlockSpec((nb, L, D_A), lambda i: (i, 0, 0))
    crow = pl.BlockSpec((nb, NH_A, DH_A, nc), lambda i: (i, 0, 0, 0))
    return pl.pallas_call(
        functools.partial(_attn_sample_kernel, nb=nb),
        out_shape=jax.ShapeDtypeStruct((NB, L, D_A), BF16),
        grid=(NB // nb,),
        in_specs=[row, row, row, crow, crow, _const_spec(ext.shape)],
        out_specs=row,
        scratch_shapes=[pltpu.VMEM((NH_A, L, nc + L), F32)],
        compiler_params=_cparams(1),
        name="attn_sample",
    )(aq, ak, av, cache_kt, cache_vt, ext)


def _rel_ext(rel_table, tq, klen):
    p_len = -(-(klen + tq) // LANES) * LANES
    n_lo = tq - CHUNK
    n_hi = p_len - tq - MAX_REL
    rep = lambda col, n: jnp.broadcast_to(col, (NH_A, n))
    by_dist = jnp.concatenate([rep(rel_table[:, :1], n_lo), rel_table, rep(rel_table[:, -1:], n_hi)],
                              axis=1)
    rev = by_dist[:, ::-1]
    n = p_len - klen
    return jnp.concatenate([rev[:, n:], rev[:, :n]], axis=1).astype(F32)


def _gate_lanes(g):
    z = lambda n: jnp.zeros((g.shape[0], n), g.dtype)
    return jnp.concatenate([g[:, :NH_M], z(8 - NH_M), g[:, NH_M:], z(LANES - 8 - NH_M)], axis=1)


def _layer(x, mod, conv_prev, state, cache, w, *, nb, tl, nbm, lc, final):
    NB, L, d = x.shape
    x1 = _ffn(x, mod, w["g0"], w["up1"], w["dn1"], w["g_final"], nb=nb, tl=tl, sub=0)
    (q, k, v, mo, gates, aq, ak, av, k_tail, v_tail, conv_new) = _proj(
        x1, mod, w["g1"], w["w_main"], w["w_gates"], w["b_gates"], w["conv_w"], w["conv_b"],
        conv_prev, nb=nb, tl=tl)
    c0, n0, m0 = state
    n0 = jnp.pad(n0, ((0, 0), (0, 8 - NH_M), (0, 0)))
    m0 = jnp.broadcast_to(jnp.pad(m0, ((0, 0), (0, 8 - NH_M)))[:, :, None], (NB, 8, LANES))
    hm, c_new, n_new, m_new = _mlstm(q, k, v, gates, mo, c0, n0, m0, w["g_mlstm"], nbm=nbm, lc=lc)
    if cache is None:
        att = _attn_prompt(aq, ak, av, w["ext_prompt"])
    else:
        att = _attn_sample(aq, ak, av, cache[0], cache[1], w["ext_sample"], nb=min(NB, 4))
    y = _ffn(x1, mod, w["g2"], w["up2"], w["dn2"], w["g_final"], nb=nb, tl=tl, sub=2,
             mix=(hm, att, w["w_out"]), final=final)
    n_keep = k_tail.shape[1]
    states = (c_new, n_new[:, :NH_M, :], m_new[:, :NH_M, 0], conv_new,
              k_tail.reshape(NB, n_keep, NH_A, DH_A), v_tail.reshape(NB, n_keep, NH_A, DH_A))
    return y, states


def kernel(x_prompt, x_sample, state_mlstm_C, state_mlstm_n, state_mlstm_m, state_conv, cache_att_k, cache_att_v, c_prompt, c_sample, w_ada, b_ada, g_norm, w_ffn1_up, w_ffn1_down, w_ffn2_up, w_ffn2_down, w_in, conv_w, conv_b, b_gates, g_mlstm, rel_bias_table, w_out, g_final):
    depth = w_ada.shape[0]
    bp, seq, d = x_prompt.shape
    bs, dseq, _ = x_sample.shape
    xp, xs = x_prompt, x_sample
    st_p, st_s = [], []
    n_c = cache_att_k.shape[2]
    tl_p = min(ATT_WINDOW, seq)
    lc_p = min(MLSTM_CHUNK, seq)
    nb_s = max(1, min(bs, ATT_WINDOW // dseq))
    for l in range(depth):
        rows = bp + bs
        pad = (-rows) % 8
        c_all = jnp.concatenate([c_prompt, c_sample, jnp.zeros((pad, d), F32)], axis=0)
        mod = _adaln(c_all, w_ada[l], b_ada[l])
        mod_p = mod[:bp].reshape(bp, N_MOD, d)
        mod_s = mod[bp:rows].reshape(bs, N_MOD, d)
        off_g = 4 * D_M
        off_a = off_g + 2 * NH_M
        wl = w_in[l]
        rel = rel_bias_table[l]
        w = {
            "g0": g_norm[l, 0:1], "g1": g_norm[l, 1:2], "g2": g_norm[l, 2:3],
            "g_final": g_final.reshape(1, d),
            "up1": w_ffn1_up[l].astype(BF16), "dn1": w_ffn1_down[l].astype(BF16),
            "up2": w_ffn2_up[l].astype(BF16), "dn2": w_ffn2_down[l].astype(BF16),
            "w_main": jnp.concatenate([wl[:, :off_g], wl[:, off_a:]], axis=1).astype(BF16),
            "w_gates": _gate_lanes(wl[:, off_g:off_a]).astype(BF16),
            "b_gates": _gate_lanes(b_gates[l].reshape(1, 2 * NH_M)),
            "conv_w": conv_w[l], "conv_b": conv_b[l].reshape(1, 2 * D_M),
            "g_mlstm": g_mlstm[l].reshape(1, D_M),
            "w_out": w_out[l].astype(BF16),
            "ext_prompt": _rel_ext(rel, ATT_TQ, 3 * ATT_TQ),
            "ext_sample": _rel_ext(rel, dseq, n_c + dseq),
        }
        zero_state = (jnp.zeros((bp, NH_M, DH_M, DH_M), F32), jnp.zeros((bp, NH_M, DH_M), F32),
                      jnp.zeros((bp, NH_M), F32))
        xp, sp = _layer(xp, mod_p, jnp.zeros((bp, CONV_W - 1, 2 * D_M), F32), zero_state, None, w,
                        nb=1, tl=tl_p, nbm=bp, lc=lc_p, final=l == depth - 1)
        state_s = (state_mlstm_C[l], state_mlstm_n[l], state_mlstm_m[l])
        cache = (jnp.transpose(cache_att_k[l], (0, 2, 3, 1)), jnp.transpose(cache_att_v[l], (0, 2, 3, 1)))
        xs, ss = _layer(xs, mod_s, state_conv[l], state_s, cache, w, nb=nb_s, tl=dseq,
                        nbm=min(bs, 4), lc=dseq, final=l == depth - 1)
        st_p.append(sp)
        st_s.append(ss)
    stk = lambda sts, i: jnp.stack([s[i] for s in sts])
    return ((xp, xs) + tuple(stk(st_p, i) for i in range(6)) + tuple(stk(st_s, i) for i in range(6)))
```

```python
import functools

import jax
import jax.numpy as jnp
from jax import lax
from jax.experimental import pallas as pl
from jax.experimental.pallas import tpu as pltpu

F32 = jnp.float32
BF16 = jnp.bfloat16

CHUNK = 64
NH_M = 4
DH_M = 128
D_M = NH_M * DH_M
NH_A = 8
DH_A = 64
D_A = NH_A * DH_A
CONV_W = 4
LEFT_CHUNKS = 8
ATT_WINDOW = LEFT_CHUNKS * CHUNK
MAX_REL = 2 * CHUNK
N_MOD = 9
EPS = 1e-6
NEG = -1e30
LOG2E = 1.4426950408889634

LANES = 128
V7X_VMEM_BYTES = 64 * 1024 * 1024
VMEM_LIMIT = V7X_VMEM_BYTES - 8 * 1024 * 1024

FF_CHUNK = 256
MLSTM_CHUNK = 256
ATT_SUB = 2 * CHUNK


def _cparams(n_axes):
    return pltpu.CompilerParams(dimension_semantics=("arbitrary",) * n_axes,
                                vmem_limit_bytes=VMEM_LIMIT)


def _const_spec(shape):
    nd = len(shape)
    return pl.BlockSpec(shape, lambda *_: (0,) * nd, pipeline_mode=pl.Buffered(1))


def _dot(a, b):
    return jnp.dot(a, b, preferred_element_type=F32)


def _dot_nt(a, b):
    return lax.dot_general(a, b, (((1,), (1,)), ((), ())), preferred_element_type=F32)


def _dot_tn(a, b):
    return lax.dot_general(a, b, (((0,), (0,)), ((), ())), preferred_element_type=F32)


def _sigmoid(x):
    return 1.0 / (1.0 + jnp.exp(-x))


def _log_sigmoid(x):
    return jnp.minimum(x, 0.0) - jnp.log(1.0 + jnp.exp(-jnp.abs(x)))


def _norm_mod(x, g, shift, scale):
    ms = jnp.mean(x * x, axis=-1, keepdims=True)
    h = x * lax.rsqrt(ms + EPS) * g
    return h * (1.0 + scale) + shift


def _adaln_kernel(c_ref, w_ref, b_ref, o_ref):
    c = c_ref[...]
    o_ref[...] = _dot(c * _sigmoid(c), w_ref[...]) + b_ref[...]


def _adaln(c_all, w_ada, b_ada):
    rows, d = c_all.shape
    n = w_ada.shape[1]
    tn = d
    return pl.pallas_call(
        _adaln_kernel,
        out_shape=jax.ShapeDtypeStruct((rows, n), F32),
        grid=(n // tn,),
        in_specs=[pl.BlockSpec((rows, d), lambda j: (0, 0)),
                  pl.BlockSpec((d, tn), lambda j: (0, j)),
                  pl.BlockSpec((1, tn), lambda j: (0, j))],
        out_specs=pl.BlockSpec((rows, tn), lambda j: (0, j)),
        compiler_params=_cparams(1),
        name="adaln",
    )(c_all, w_ada, b_ada.reshape(1, n))


def _ffn_kernel(*refs, nb, tl, sub, with_mix, with_final):
    if with_mix:
        x_ref, hm_ref, att_ref, mod_ref, gn_ref, wout_ref, wup_ref, wdn_ref, gfin_ref, o_ref, hb_ref, act_ref = refs
    else:
        x_ref, mod_ref, gn_ref, wup_ref, wdn_ref, gfin_ref, o_ref, hb_ref, act_ref = refs
    m = nb * tl
    d = x_ref.shape[-1]
    dff = wdn_ref.shape[0]
    mod = mod_ref[...]
    shift, scale, gate = (mod[:, 3 * sub + i:3 * sub + i + 1, :] for i in range(3))

    if with_mix:
        mix = _dot(hm_ref[...].reshape(m, D_M), wout_ref[0:D_M, :])
        mix = mix + _dot(att_ref[...].reshape(m, D_A), wout_ref[D_M:, :])
        o_ref[...] = x_ref[...] + mod[:, 5:6, :] * mix.reshape(nb, tl, d)
    else:
        o_ref[...] = x_ref[...]

    h = _norm_mod(o_ref[...], gn_ref[...], shift, scale)
    hb_ref[...] = h.reshape(m, d).astype(BF16)
    for c in range(dff // FF_CHUNK):
        lo = c * FF_CHUNK
        g = _dot(hb_ref[...], wup_ref[:, lo:lo + FF_CHUNK])
        u = _dot(hb_ref[...], wup_ref[:, dff + lo:dff + lo + FF_CHUNK])
        act_ref[:, lo:lo + FF_CHUNK] = (g * _sigmoid(g) * u).astype(BF16)
    dn = _dot(act_ref[...], wdn_ref[...])
    y = o_ref[...] + 0.5 * gate * dn.reshape(nb, tl, d)
    if with_final:
        ms = jnp.mean(y * y, axis=-1, keepdims=True)
        y = y * lax.rsqrt(ms + EPS) * gfin_ref[...]
    o_ref[...] = y


def _ffn(x, mod, g_norm_row, w_up, w_dn, g_final, *, nb, tl, sub, mix=None, final=False):
    NB, L, d = x.shape
    dff = w_dn.shape[0]
    m = nb * tl
    grid = (NB // nb, L // tl)
    row = lambda w: pl.BlockSpec((nb, tl, w), lambda i, j: (i, j, 0))
    in_specs = [row(d)]
    args = [x]
    if mix is not None:
        hm, att, w_out = mix
        in_specs += [row(D_M), row(D_A)]
        args += [hm, att]
    in_specs += [pl.BlockSpec((nb, N_MOD, d), lambda i, j: (i, 0, 0)), _const_spec((1, d))]
    args += [mod, g_norm_row]
    if mix is not None:
        in_specs.append(_const_spec(w_out.shape))
        args.append(w_out)
    in_specs += [_const_spec(w_up.shape), _const_spec(w_dn.shape), _const_spec((1, d))]
    args += [w_up, w_dn, g_final]
    kern = functools.partial(_ffn_kernel, nb=nb, tl=tl, sub=sub, with_mix=mix is not None,
                             with_final=final)
    return pl.pallas_call(
        kern,
        out_shape=jax.ShapeDtypeStruct((NB, L, d), F32),
        grid=grid,
        in_specs=in_specs,
        out_specs=row(d),
        scratch_shapes=[pltpu.VMEM((m, d), BF16), pltpu.VMEM((m, dff), BF16)],
        compiler_params=_cparams(2),
        name="ffn%d" % sub,
    )(*args)


def _proj_kernel(x_ref, mod_ref, gn_ref, w_ref, wg_ref, bg_ref, cw_ref, cb_ref, cprev_ref,
                 q_ref, k_ref, v_ref, o_ref, gates_ref, aq_ref, ak_ref, av_ref,
                 kt_ref, vt_ref, cnew_ref, hb_ref, carry_ref, *, nb, tl):
    m = nb * tl
    d = x_ref.shape[-1]
    j = pl.program_id(1)
    mod = mod_ref[...]
    h = _norm_mod(x_ref[...], gn_ref[...], mod[:, 3:4, :], mod[:, 4:5, :])
    hb_ref[...] = h.reshape(m, d).astype(BF16)

    @pl.when(j == 0)
    def _():
        carry_ref[...] = jnp.zeros(carry_ref.shape, F32)
        carry_ref[:, 8 - (CONV_W - 1):8, :] = cprev_ref[...]

    def qk_pre(c):
        lo = c * FF_CHUNK
        return _dot(hb_ref[...], w_ref[:, lo:lo + FF_CHUNK]).reshape(nb, tl, FF_CHUNK)

    def conv(c, pre):
        lo = c * FF_CHUNK
        ext = jnp.concatenate([carry_ref[:, :, lo:lo + FF_CHUNK], pre], axis=1)
        acc = cb_ref[:, lo:lo + FF_CHUNK]
        for t in reversed(range(CONV_W)):
            s0 = 8 - (CONV_W - 1) + t
            acc = acc + cw_ref[t:t + 1, lo:lo + FF_CHUNK] * ext[:, s0:s0 + tl, :]
        y = acc * _sigmoid(acc)
        if lo < D_M:
            q_ref[:, :, lo:lo + FF_CHUNK] = y.astype(BF16)
        else:
            k_ref[:, :, lo - D_M:lo - D_M + FF_CHUNK] = (y * DH_M ** -0.5).astype(BF16)
        cnew_ref[:, :, lo:lo + FF_CHUNK] = pre[:, tl - (CONV_W - 1):tl, :]
        carry_ref[:, :, lo:lo + FF_CHUNK] = pre[:, tl - 8:tl, :]

    def cols(c0):
        return _dot(hb_ref[...], w_ref[:, c0:c0 + D_M]).reshape(nb, tl, D_M)

    pre = qk_pre(0)
    v_ref[...] = cols(2 * D_M).astype(BF16)
    conv(0, pre)
    pre = qk_pre(1)
    o_ref[...] = cols(3 * D_M)
    conv(1, pre)
    pre = qk_pre(2)
    aq_ref[...] = (cols(4 * D_M) * (DH_A ** -0.5 * LOG2E)).astype(BF16)
    conv(2, pre)
    pre = qk_pre(3)
    ak = cols(4 * D_M + D_A)
    ak_ref[...] = ak.astype(BF16)
    kt_ref[...] = ak
    conv(3, pre)
    av = cols(4 * D_M + 2 * D_A)
    av_ref[...] = av.astype(BF16)
    vt_ref[...] = av
    gates_ref[...] = (_dot(hb_ref[...], wg_ref[...]) + bg_ref[...]).reshape(nb, tl, LANES)


def _proj(x, mod, g_norm_row, w_main, w_gates, b_gates, conv_w, conv_b, conv_prev, *, nb, tl):
    NB, L, d = x.shape
    assert tl == min(ATT_WINDOW, L)
    m = nb * tl
    grid = (NB // nb, L // tl)
    row = lambda w: pl.BlockSpec((nb, tl, w), lambda i, j: (i, j, 0))
    per_b = lambda r, w: pl.BlockSpec((nb, r, w), lambda i, j: (i, 0, 0))
    bshape = lambda w, dt: jax.ShapeDtypeStruct((NB, L, w), dt)
    in_specs = [row(d), per_b(N_MOD, d), _const_spec((1, d)), _const_spec(w_main.shape),
                _const_spec(w_gates.shape), _const_spec((1, LANES)), _const_spec(conv_w.shape),
                _const_spec((1, 2 * D_M)), per_b(CONV_W - 1, 2 * D_M)]
    out_shape = [bshape(D_M, BF16), bshape(D_M, BF16), bshape(D_M, BF16), bshape(D_M, F32),
                 bshape(LANES, F32), bshape(D_A, BF16), bshape(D_A, BF16), bshape(D_A, BF16),
                 jax.ShapeDtypeStruct((NB, tl, D_A), F32), jax.ShapeDtypeStruct((NB, tl, D_A), F32),
                 jax.ShapeDtypeStruct((NB, CONV_W - 1, 2 * D_M), F32)]
    out_specs = [row(D_M), row(D_M), row(D_M), row(D_M), row(LANES), row(D_A), row(D_A), row(D_A),
                 per_b(tl, D_A), per_b(tl, D_A), per_b(CONV_W - 1, 2 * D_M)]
    return pl.pallas_call(
        functools.partial(_proj_kernel, nb=nb, tl=tl),
        out_shape=out_shape,
        grid=grid,
        in_specs=in_specs,
        out_specs=out_specs,
        scratch_shapes=[pltpu.VMEM((m, d), BF16), pltpu.VMEM((nb, 8, 2 * D_M), F32)],
        compiler_params=_cparams(2),
        name="proj",
    )(x, mod, g_norm_row, w_main, w_gates, b_gates, conv_w, conv_b, conv_prev)


def _prefix_max_lanes(x):
    n = x.shape[-1]
    lane = lax.broadcasted_iota(jnp.int32, x.shape, 1)
    shift = 1
    while shift < n:
        x = jnp.maximum(x, jnp.where(lane >= shift, pltpu.roll(x, shift, 1), NEG))
        shift *= 2
    return x


def _split_bf16(x):
    hi = x.astype(BF16)
    r = x - hi.astype(F32)
    mid = r.astype(BF16)
    return hi, mid, (r - mid.astype(F32)).astype(BF16)


def _mlstm_kernel(q_ref, k_ref, v_ref, gates_ref, mo_ref, c0_ref, n0_ref, m0_ref, gm_ref,
                  hm_ref, c_ref, n_ref, m_ref, ct_ref, *, nbm, lc, single_chunk):
    j = pl.program_id(1)

    def transpose_states(src_ref, dst_ref):
        def one(i, carry):
            dst_ref[i // NH_M, i % NH_M] = jnp.transpose(src_ref[i // NH_M, i % NH_M])
            return carry
        lax.fori_loop(0, nbm * NH_M, one, 0)

    if single_chunk:
        n_ref[...] = n0_ref[...]
        m_ref[...] = m0_ref[...]
        ct_get = lambda b, h: jnp.transpose(c0_ref[b, h])

        def ct_put(b, h, val):
            c_ref[b, h] = jnp.transpose(val)
    else:
        @pl.when(j == 0)
        def _():
            transpose_states(c0_ref, ct_ref)
            n_ref[...] = n0_ref[...]
            m_ref[...] = m0_ref[...]
        ct_get = lambda b, h: ct_ref[b, h]

        def ct_put(b, h, val):
            ct_ref[b, h] = val

    s_i = lax.broadcasted_iota(jnp.int32, (lc, lc), 0)
    t_i = lax.broadcasted_iota(jnp.int32, (lc, lc), 1)
    causal = s_i <= t_i
    upper = jnp.where(causal, 1.0, 0.0).astype(BF16)
    head_row = lax.broadcasted_iota(jnp.int32, (8, 1), 0)

    pairs = [(b, h) for b in range(nbm) for h in range(NH_M)]
    hsl = lambda h: slice(h * DH_M, (h + 1) * DH_M)

    rows = []
    for b in range(nbm):
        gates_t = jnp.transpose(gates_ref[b])
        ig_t = gates_t[0:8]
        lf_t = _log_sigmoid(gates_t[8:16])
        b_t = sum(_dot(part, upper) for part in _split_bf16(lf_t))
        c_t = ig_t - b_t
        b_end = b_t[:, lc - 1:lc]
        m_prev = m_ref[b][:, 0:1]
        log_g = b_end + c_t
        m_end = jnp.maximum(b_end + m_prev, jnp.max(log_g, axis=-1, keepdims=True))
        decay = jnp.exp(b_end + m_prev - m_end)
        g_rows = jnp.exp(log_g - m_end)
        log_inter = b_t + m_prev
        c2_t = c_t * LOG2E
        c2_cols = jnp.transpose(jnp.concatenate([c2_t, jnp.zeros((LANES - 8, lc), F32)], axis=0))
        if lc % LANES == 0:
            cmax2_t = _prefix_max_lanes(c2_t)
        else:
            cmax2_t = jnp.concatenate(
                [jnp.max(jnp.where(causal, c2_cols[:, h:h + 1], NEG), axis=0, keepdims=True)
                 for h in range(8)], axis=0)
        m_intra = b_t + cmax2_t * (1.0 / LOG2E)
        m_tok = jnp.maximum(log_inter, m_intra)
        rows.append(dict(decay=decay, g=g_rows, g_b=g_rows.astype(BF16), m_end=m_end,
                         c2_cols=c2_cols, cmax2=cmax2_t, r_intra=jnp.exp(m_intra - m_tok),
                         w_inter=jnp.exp(log_inter - m_tok), floor=jnp.exp(-m_tok),
                         n_prev_b=n_ref[b].astype(BF16)))

    early = {}
    for b, h in pairs:
        q = q_ref[b, :, hsl(h)]
        k = k_ref[b, :, hsl(h)]
        ct_prev = ct_get(b, h)
        early[b, h] = dict(
            ct=ct_prev if single_chunk else None,
            s_t=_dot_nt(k, q),
            cq_t=_dot_nt(ct_prev.astype(BF16), q),
            nq=_dot_nt(rows[b]["n_prev_b"], q)[h:h + 1, :],
            v_t=jnp.transpose(v_ref[b, :, hsl(h)].astype(F32)))

    n_upd = [jnp.zeros((8, DH_M), F32) for _ in range(nbm)]
    for b, h in pairs:
        r, e = rows[b], early[b, h]
        k = k_ref[b, :, hsl(h)]
        d_t = jnp.exp2(r["c2_cols"][:, h:h + 1] - r["cmax2"][h:h + 1, :])
        a_t = jnp.where(causal, d_t, 0.0) * e["s_t"]
        r_intra, w_inter = r["r_intra"][h:h + 1, :], r["w_inter"][h:h + 1, :]
        num_t = _dot(e["v_t"].astype(BF16), a_t.astype(BF16)) * r_intra + e["cq_t"] * w_inter
        den = jnp.sum(a_t, axis=0, keepdims=True) * r_intra + e["nq"] * w_inter
        h_t = num_t * (1.0 / jnp.maximum(jnp.abs(den), r["floor"][h:h + 1, :]))
        hn_t = h_t * lax.rsqrt(jnp.mean(h_t * h_t, axis=0, keepdims=True) + EPS)
        hn = jnp.transpose(hn_t) * gm_ref[:, hsl(h)]
        hm_ref[b, :, hsl(h)] = (_sigmoid(mo_ref[b, :, hsl(h)]) * hn).astype(BF16)

        vg_t = (e["v_t"] * r["g"][h:h + 1, :]).astype(BF16)
        ct_prev = e["ct"] if single_chunk else ct_get(b, h)
        ct_put(b, h, r["decay"][h:h + 1, :] * ct_prev + _dot(vg_t, k))
        n_upd[b] = n_upd[b] + jnp.where(head_row == h, _dot(r["g_b"], k), 0.0)

    for b in range(nbm):
        n_ref[b] = rows[b]["decay"] * n_ref[b] + n_upd[b]
        m_ref[b] = jnp.broadcast_to(rows[b]["m_end"], (8, LANES))

    if not single_chunk:
        @pl.when(j == pl.num_programs(1) - 1)
        def _():
            transpose_states(ct_ref, c_ref)


def _mlstm(q, k, v, gates, mo, c0, n0, m0, g_mlstm_row, *, nbm, lc):
    NB, L, _ = q.shape
    grid = (NB // nbm, L // lc)
    row = lambda w: pl.BlockSpec((nbm, lc, w), lambda i, j: (i, j, 0))
    st = lambda s: pl.BlockSpec((nbm,) + s, lambda i, j: (i,) + (0,) * len(s))
    c_s, n_s = (NH_M, DH_M, DH_M), (8, DH_M)
    return pl.pallas_call(
        functools.partial(_mlstm_kernel, nbm=nbm, lc=lc, single_chunk=L == lc),
        out_shape=[jax.ShapeDtypeStruct((NB, L, D_M), BF16),
                   jax.ShapeDtypeStruct((NB,) + c_s, F32),
                   jax.ShapeDtypeStruct((NB,) + n_s, F32),
                   jax.ShapeDtypeStruct((NB,) + n_s, F32)],
        grid=grid,
        in_specs=[row(D_M), row(D_M), row(D_M), row(LANES), row(D_M), st(c_s), st(n_s), st(n_s),
                  _const_spec((1, D_M))],
        out_specs=[row(D_M), st(c_s), st(n_s), st(n_s)],
        scratch_shapes=[pltpu.VMEM((nbm,) + c_s, F32)],
        compiler_params=_cparams(2),
        name="mlstm",
    )(q, k, v, gates, mo, c0, n0, m0, g_mlstm_row)


def _build_band_bias(ext_ref, tq, klen, put):
    p_len = ext_ref.shape[-1]
    q_pos = lax.broadcasted_iota(jnp.int32, (tq, klen), 0) + (klen - tq)
    k_pos = lax.broadcasted_iota(jnp.int32, (tq, klen), 1)
    back = q_pos // CHUNK - k_pos // CHUNK
    for h in range(NH_A):
        base = jnp.broadcast_to(ext_ref[h:h + 1, :] * LOG2E, (tq, p_len))
        toeplitz = pltpu.roll(base, 0, 1, stride=1, stride_axis=0)[:, :klen]
        put(h, jnp.where(back >= 0, jnp.where(back <= LEFT_CHUNKS, toeplitz, NEG), NEG))


def _lane_tiles(s):
    n = s.shape[-1]
    if n % LANES:
        return [s]
    return [s[:, i * LANES:(i + 1) * LANES] for i in range(n // LANES)]


def _row_reduce(tiles, op, lane_op):
    by_width = {}
    for t in tiles:
        by_width[t.shape[-1]] = t if t.shape[-1] not in by_width else op(by_width[t.shape[-1]], t)
    return functools.reduce(op, [lane_op(t, axis=-1, keepdims=True) for t in by_width.values()])


def _softmax_pv(ss, vs):
    mx = _row_reduce([t for s in ss for t in _lane_tiles(s)], jnp.maximum, jnp.max)
    es = [jnp.exp2(s - mx) for s in ss]
    l = _row_reduce([t for e in es for t in _lane_tiles(e)], jnp.add, jnp.sum)
    o = functools.reduce(jnp.add, [_dot(e.astype(BF16), v) for e, v in zip(es, vs)])
    return o / l


def _attn_pairs(jobs):
    first = lax.broadcasted_iota(jnp.int32, (1, LANES), 1) < DH_A
    zero = jnp.zeros((), BF16)

    def scores(q, segs, p):
        sl = slice(p * LANES, (p + 1) * LANES)
        qp = q(sl)
        q2 = jnp.concatenate([jnp.where(first, qp, zero), jnp.where(first, zero, qp)], axis=0)
        ss = []
        for k, _, bias, pen in segs:
            s = _dot_nt(q2, k(sl)) + bias(p)
            ss.append(s if pen is None else s + pen)
        return ss

    def finish(ss, segs, put, p):
        sl = slice(p * LANES, (p + 1) * LANES)
        r = ss[0].shape[0] // 2
        o2 = _softmax_pv(ss, [v(sl) for _, v, _, _ in segs])
        put(sl, jnp.where(first, o2[:r], o2[r:]))

    pending = None
    for q, segs, put in jobs:
        for p in range(NH_A // 2):
            ss = scores(q, segs, p)
            if pending is not None:
                finish(*pending)
            pending = (ss, segs, put, p)
    finish(*pending)


def _attn_prompt_kernel(q_ref, kp_ref, ko_ref, vp_ref, vo_ref, ext_ref, o_ref, bias_ref):
    tq = q_ref.shape[1]
    j = pl.program_id(1)

    @pl.when((pl.program_id(0) == 0) & (j == 0))
    def _():
        def put_bias(h, tile):
            bias_ref[h // 2, (h % 2) * ATT_SUB:(h % 2 + 1) * ATT_SUB, :] = tile
        _build_band_bias(ext_ref, ATT_SUB, tq + ATT_SUB, put_bias)

    def run(pen_prev):
        jobs = []
        for i in range(tq // ATT_SUB):
            lo = i * ATT_SUB
            n_prev = tq - lo
            bias = lambda a, b: (lambda p: bias_ref[p, :, a:b])
            rows = lambda ref, a, b: (lambda sl: ref[0, a:b, sl])

            def put(sl, val, lo=lo):
                o_ref[0, lo:lo + ATT_SUB, sl] = val.astype(o_ref.dtype)

            segs = [(rows(kp_ref, lo, tq), rows(vp_ref, lo, tq), bias(0, n_prev), pen_prev),
                    (rows(ko_ref, 0, lo + ATT_SUB), rows(vo_ref, 0, lo + ATT_SUB),
                     bias(n_prev, tq + ATT_SUB), None)]
            jobs.append((rows(q_ref, lo, lo + ATT_SUB), segs, put))
        _attn_pairs(jobs)

    @pl.when(j >= 1)
    def _():
        run(None)

    @pl.when(j < 1)
    def _():
        run(jnp.full((), NEG, F32))


def _attn_prompt(aq, ak, av, ext):
    NB, L, _ = aq.shape
    tq = ATT_WINDOW
    assert L % tq == 0
    grid = (NB, L // tq)
    row = pl.BlockSpec((1, tq, D_A), lambda b, j: (b, j, 0))
    prev = pl.BlockSpec((1, tq, D_A), lambda b, j: (b, jnp.maximum(j - 1, 0), 0))
    return pl.pallas_call(
        _attn_prompt_kernel,
        out_shape=jax.ShapeDtypeStruct((NB, L, D_A), BF16),
        grid=grid,
        in_specs=[row, prev, row, prev, row, _const_spec(ext.shape)],
        out_specs=row,
        scratch_shapes=[pltpu.VMEM((NH_A // 2, 2 * ATT_SUB, tq + ATT_SUB), F32)],
        compiler_params=_cparams(2),
        name="attn_prompt",
    )(aq, ak, ak, av, av, ext)


def _attn_sample_kernel(q_ref, k_ref, v_ref, ckt_ref, cvt_ref, ext_ref, o_ref, bias_ref, *, nb):
    tq = q_ref.shape[1]
    nc = ckt_ref.shape[-1]

    @pl.when(pl.program_id(0) == 0)
    def _():
        def put_bias(h, tile):
            bias_ref[h] = tile
        _build_band_bias(ext_ref, tq, nc + tq, put_bias)

    def scores(b, h):
        hs = slice(h * DH_A, (h + 1) * DH_A)
        qh = q_ref[b, :, hs]
        return [_dot(qh, ckt_ref[b, h].astype(BF16)) + bias_ref[h, :, 0:nc],
                _dot_nt(qh, k_ref[b, :, hs]) + bias_ref[h, :, nc:nc + tq]]

    def finish(b, h, ss):
        hs = slice(h * DH_A, (h + 1) * DH_A)
        mx = jnp.maximum(*[jnp.max(s, axis=-1, keepdims=True) for s in ss])
        e_c, e_o = [jnp.exp2(s - mx) for s in ss]
        l = jnp.sum(e_c, axis=-1, keepdims=True) + jnp.sum(e_o, axis=-1, keepdims=True)
        o = (_dot_nt(e_c.astype(BF16), cvt_ref[b, h].astype(BF16))
             + _dot(e_o.astype(BF16), v_ref[b, :, hs]))
        return o / l

    for b in range(nb):
        outs = []
        pending = None
        for h in range(NH_A):
            ss = scores(b, h)
            if pending is not None:
                outs.append(finish(b, *pending))
            pending = (h, ss)
        outs.append(finish(b, *pending))
        o_ref[b] = jnp.concatenate(outs, axis=-1).astype(o_ref.dtype)


def _attn_sample(aq, ak, av, cache_kt, cache_vt, ext, *, nb):
    NB, L, _ = aq.shape
    nc = cache_kt.shape[-1]
    row = pl.BlockSpec((nb, L, D_A), lambda i: (i, 0, 0))
    crow = pl.BlockSpec((nb, NH_A, DH_A, nc), lambda i: (i, 0, 0, 0))
    return pl.pallas_call(
        functools.partial(_attn_sample_kernel, nb=nb),
        out_shape=jax.ShapeDtypeStruct((NB, L, D_A), BF16),
        grid=(NB // nb,),
        in_specs=[row, row, row, crow, crow, _const_spec(ext.shape)],
        out_specs=row,
        scratch_shapes=[pltpu.VMEM((NH_A, L, nc + L), F32)],
        compiler_params=_cparams(1),
        name="attn_sample",
    )(aq, ak, av, cache_kt, cache_vt, ext)


def _rel_ext(rel_table, tq, klen):
    p_len = -(-(klen + tq) // LANES) * LANES
    n_lo = tq - CHUNK
    n_hi = p_len - tq - MAX_REL
    rep = lambda col, n: jnp.broadcast_to(col, (NH_A, n))
    by_dist = jnp.concatenate([rep(rel_table[:, :1], n_lo), rel_table, rep(rel_table[:, -1:], n_hi)],
                              axis=1)
    rev = by_dist[:, ::-1]
    n = p_len - klen
    return jnp.concatenate([rev[:, n:], rev[:, :n]], axis=1).astype(F32)


def _gate_lanes(g):
    z = lambda n: jnp.zeros((g.shape[0], n), g.dtype)
    return jnp.concatenate([g[:, :NH_M], z(8 - NH_M), g[:, NH_M:], z(LANES - 8 - NH_M)], axis=1)


def _layer(x, mod, conv_prev, state, cache, w, *, nb, tl, nbm, lc, final):
    NB, L, d = x.shape
    x1 = _ffn(x, mod, w["g0"], w["up1"], w["dn1"], w["g_final"], nb=nb, tl=tl, sub=0)
    (q, k, v, mo, gates, aq, ak, av, k_tail, v_tail, conv_new) = _proj(
        x1, mod, w["g1"], w["w_main"], w["w_gates"], w["b_gates"], w["conv_w"], w["conv_b"],
        conv_prev, nb=nb, tl=tl)
    c0, n0, m0 = state
    n0 = jnp.pad(n0, ((0, 0), (0, 8 - NH_M), (0, 0)))
    m0 = jnp.broadcast_to(jnp.pad(m0, ((0, 0), (0, 8 - NH_M)))[:, :, None], (NB, 8, LANES))
    hm, c_new, n_new, m_new = _mlstm(q, k, v, gates, mo, c0, n0, m0, w["g_mlstm"], nbm=nbm, lc=lc)
    if cache is None:
        att = _attn_prompt(aq, ak, av, w["ext_prompt"])
    else:
        att = _attn_sample(aq, ak, av, cache[0], cache[1], w["ext_sample"], nb=min(NB, 4))
    y = _ffn(x1, mod, w["g2"], w["up2"], w["dn2"], w["g_final"], nb=nb, tl=tl, sub=2,
             mix=(hm, att, w["w_out"]), final=final)
    n_keep = k_tail.shape[1]
    states = (c_new, n_new[:, :NH_M, :], m_new[:, :NH_M, 0], conv_new,
              k_tail.reshape(NB, n_keep, NH_A, DH_A), v_tail.reshape(NB, n_keep, NH_A, DH_A))
    return y, states


def kernel(x_prompt, x_sample, state_mlstm_C, state_mlstm_n, state_mlstm_m, state_conv, cache_att_k, cache_att_v, c_prompt, c_sample, w_ada, b_ada, g_norm, w_ffn1_up, w_ffn1_down, w_ffn2_up, w_ffn2_down, w_in, conv_w, conv_b, b_gates, g_mlstm, rel_bias_table, w_out, g_final):
    depth = w_ada.shape[0]
    bp, seq, d = x_prompt.shape
    bs, dseq, _ = x_sample.shape
    xp, xs = x_prompt, x_sample
    st_p, st_s = [], []
    n_c = cache_att_k.shape[2]
    tl_p = min(ATT_WINDOW, seq)
    lc_p = min(MLSTM_CHUNK, seq)
    nb_s = max(1, min(bs, ATT_WINDOW // dseq))
    for l in range(depth):
        rows = bp + bs
        pad = (-rows) % 8
        c_all = jnp.concatenate([c_prompt, c_sample, jnp.zeros((pad, d), F32)], axis=0)
        mod = _adaln(c_all, w_ada[l], b_ada[l])
        mod_p = mod[:bp].reshape(bp, N_MOD, d)
        mod_s = mod[bp:rows].reshape(bs, N_MOD, d)
        off_g = 4 * D_M
        off_a = off_g + 2 * NH_M
        wl = w_in[l]
        rel = rel_bias_table[l]
        w = {
            "g0": g_norm[l, 0:1], "g1": g_norm[l, 1:2], "g2": g_norm[l, 2:3],
            "g_final": g_final.reshape(1, d),
            "up1": w_ffn1_up[l].astype(BF16), "dn1": w_ffn1_down[l].astype(BF16),
            "up2": w_ffn2_up[l].astype(BF16), "dn2": w_ffn2_down[l].astype(BF16),
            "w_main": jnp.concatenate([wl[:, :off_g], wl[:, off_a:]], axis=1).astype(BF16),
            "w_gates": _gate_lanes(wl[:, off_g:off_a]).astype(BF16),
            "b_gates": _gate_lanes(b_gates[l].reshape(1, 2 * NH_M)),
            "conv_w": conv_w[l], "conv_b": conv_b[l].reshape(1, 2 * D_M),
            "g_mlstm": g_mlstm[l].reshape(1, D_M),
            "w_out": w_out[l].astype(BF16),
            "ext_prompt": _rel_ext(rel, ATT_SUB, ATT_WINDOW + ATT_SUB),
            "ext_sample": _rel_ext(rel, dseq, n_c + dseq),
        }
        zero_state = (jnp.zeros((bp, NH_M, DH_M, DH_M), F32), jnp.zeros((bp, NH_M, DH_M), F32),
                      jnp.zeros((bp, NH_M), F32))
        xp, sp = _layer(xp, mod_p, jnp.zeros((bp, CONV_W - 1, 2 * D_M), F32), zero_state, None, w,
                        nb=1, tl=tl_p, nbm=bp, lc=lc_p, final=l == depth - 1)
        state_s = (state_mlstm_C[l], state_mlstm_n[l], state_mlstm_m[l])
        cache = (jnp.transpose(cache_att_k[l], (0, 2, 3, 1)), jnp.transpose(cache_att_v[l], (0, 2, 3, 1)))
        xs, ss = _layer(xs, mod_s, state_conv[l], state_s, cache, w, nb=nb_s, tl=dseq,
                        nbm=min(bs, 4), lc=dseq, final=l == depth - 1)
        st_p.append(sp)
        st_s.append(ss)
    stk = lambda sts, i: jnp.stack([s[i] for s in sts])
    return ((xp, xs) + tuple(stk(st_p, i) for i in range(6)) + tuple(stk(st_s, i) for i in range(6)))
```

```python
import functools

import jax
import jax.numpy as jnp
from jax import lax
from jax.experimental import pallas as pl
from jax.experimental.pallas import tpu as pltpu

F32 = jnp.float32
BF16 = jnp.bfloat16

CHUNK = 64
NH_M = 4
DH_M = 128
D_M = NH_M * DH_M
NH_A = 8
DH_A = 64
D_A = NH_A * DH_A
CONV_W = 4
LEFT_CHUNKS = 8
ATT_WINDOW = LEFT_CHUNKS * CHUNK
MAX_REL = 2 * CHUNK
N_MOD = 9
EPS = 1e-6
NEG = -1e30
LOG2E = 1.4426950408889634

LANES = 128
V7X_VMEM_BYTES = 64 * 1024 * 1024
VMEM_LIMIT = V7X_VMEM_BYTES - 8 * 1024 * 1024

FF_CHUNK = 256
MLSTM_CHUNK = 256
ATT_SUB = 2 * CHUNK
SAMPLE_ATT_LOOKAHEAD = 3


def _cparams(n_axes):
    return pltpu.CompilerParams(dimension_semantics=("arbitrary",) * n_axes,
                                vmem_limit_bytes=VMEM_LIMIT)


def _const_spec(shape):
    nd = len(shape)
    return pl.BlockSpec(shape, lambda *_: (0,) * nd, pipeline_mode=pl.Buffered(1))


def _dot(a, b):
    return jnp.dot(a, b, preferred_element_type=F32)


def _dot_nt(a, b):
    return lax.dot_general(a, b, (((1,), (1,)), ((), ())), preferred_element_type=F32)


def _dot_tn(a, b):
    return lax.dot_general(a, b, (((0,), (0,)), ((), ())), preferred_element_type=F32)


def _sigmoid(x):
    return 1.0 / (1.0 + jnp.exp(-x))


def _log_sigmoid(x):
    return jnp.minimum(x, 0.0) - jnp.log(1.0 + jnp.exp(-jnp.abs(x)))


def _norm_mod(x, g, shift, scale):
    ms = jnp.mean(x * x, axis=-1, keepdims=True)
    h = x * lax.rsqrt(ms + EPS) * g
    return h * (1.0 + scale) + shift


def _adaln_kernel(c_ref, w_ref, b_ref, o_ref):
    c = c_ref[...]
    o_ref[...] = _dot(c * _sigmoid(c), w_ref[...]) + b_ref[...]


def _adaln(c_all, w_ada, b_ada):
    rows, d = c_all.shape
    n = w_ada.shape[1]
    tn = d
    return pl.pallas_call(
        _adaln_kernel,
        out_shape=jax.ShapeDtypeStruct((rows, n), F32),
        grid=(n // tn,),
        in_specs=[pl.BlockSpec((rows, d), lambda j: (0, 0)),
                  pl.BlockSpec((d, tn), lambda j: (0, j)),
                  pl.BlockSpec((1, tn), lambda j: (0, j))],
        out_specs=pl.BlockSpec((rows, tn), lambda j: (0, j)),
        compiler_params=_cparams(1),
        name="adaln",
    )(c_all, w_ada, b_ada.reshape(1, n))


def _ffn_kernel(*refs, nb, tl, sub, with_mix, with_final):
    if with_mix:
        x_ref, hm_ref, att_ref, mod_ref, gn_ref, wout_ref, wup_ref, wdn_ref, gfin_ref, o_ref, hb_ref, act_ref = refs
    else:
        x_ref, mod_ref, gn_ref, wup_ref, wdn_ref, gfin_ref, o_ref, hb_ref, act_ref = refs
    m = nb * tl
    d = x_ref.shape[-1]
    dff = wdn_ref.shape[0]
    mod = mod_ref[...]
    shift, scale, gate = (mod[:, 3 * sub + i:3 * sub + i + 1, :] for i in range(3))

    if with_mix:
        mix = _dot(hm_ref[...].reshape(m, D_M), wout_ref[0:D_M, :])
        mix = mix + _dot(att_ref[...].reshape(m, D_A), wout_ref[D_M:, :])
        o_ref[...] = x_ref[...] + mod[:, 5:6, :] * mix.reshape(nb, tl, d)
    else:
        o_ref[...] = x_ref[...]

    h = _norm_mod(o_ref[...], gn_ref[...], shift, scale)
    hb_ref[...] = h.reshape(m, d).astype(BF16)
    for c in range(dff // FF_CHUNK):
        lo = c * FF_CHUNK
        g = _dot(hb_ref[...], wup_ref[:, lo:lo + FF_CHUNK])
        u = _dot(hb_ref[...], wup_ref[:, dff + lo:dff + lo + FF_CHUNK])
        act_ref[:, lo:lo + FF_CHUNK] = (g * _sigmoid(g) * u).astype(BF16)
    dn = _dot(act_ref[...], wdn_ref[...])
    y = o_ref[...] + 0.5 * gate * dn.reshape(nb, tl, d)
    if with_final:
        ms = jnp.mean(y * y, axis=-1, keepdims=True)
        y = y * lax.rsqrt(ms + EPS) * gfin_ref[...]
    o_ref[...] = y


def _ffn(x, mod, g_norm_row, w_up, w_dn, g_final, *, nb, tl, sub, mix=None, final=False):
    NB, L, d = x.shape
    dff = w_dn.shape[0]
    m = nb * tl
    grid = (NB // nb, L // tl)
    row = lambda w: pl.BlockSpec((nb, tl, w), lambda i, j: (i, j, 0))
    in_specs = [row(d)]
    args = [x]
    if mix is not None:
        hm, att, w_out = mix
        in_specs += [row(D_M), row(D_A)]
        args += [hm, att]
    in_specs += [pl.BlockSpec((nb, N_MOD, d), lambda i, j: (i, 0, 0)), _const_spec((1, d))]
    args += [mod, g_norm_row]
    if mix is not None:
        in_specs.append(_const_spec(w_out.shape))
        args.append(w_out)
    in_specs += [_const_spec(w_up.shape), _const_spec(w_dn.shape), _const_spec((1, d))]
    args += [w_up, w_dn, g_final]
    kern = functools.partial(_ffn_kernel, nb=nb, tl=tl, sub=sub, with_mix=mix is not None,
                             with_final=final)
    return pl.pallas_call(
        kern,
        out_shape=jax.ShapeDtypeStruct((NB, L, d), F32),
        grid=grid,
        in_specs=in_specs,
        out_specs=row(d),
        scratch_shapes=[pltpu.VMEM((m, d), BF16), pltpu.VMEM((m, dff), BF16)],
        compiler_params=_cparams(2),
        name="ffn%d" % sub,
    )(*args)


def _proj_kernel(x_ref, mod_ref, gn_ref, w_ref, wg_ref, bg_ref, cw_ref, cb_ref, cprev_ref,
                 q_ref, k_ref, v_ref, o_ref, gates_ref, aq_ref, ak_ref, av_ref,
                 kt_ref, vt_ref, cnew_ref, hb_ref, carry_ref, *, nb, tl):
    m = nb * tl
    d = x_ref.shape[-1]
    j = pl.program_id(1)
    mod = mod_ref[...]
    h = _norm_mod(x_ref[...], gn_ref[...], mod[:, 3:4, :], mod[:, 4:5, :])
    hb_ref[...] = h.reshape(m, d).astype(BF16)

    @pl.when(j == 0)
    def _():
        carry_ref[...] = jnp.zeros(carry_ref.shape, F32)
        carry_ref[:, 8 - (CONV_W - 1):8, :] = cprev_ref[...]

    def qk_pre(c):
        lo = c * FF_CHUNK
        return _dot(hb_ref[...], w_ref[:, lo:lo + FF_CHUNK]).reshape(nb, tl, FF_CHUNK)

    def conv(c, pre):
        lo = c * FF_CHUNK
        ext = jnp.concatenate([carry_ref[:, :, lo:lo + FF_CHUNK], pre], axis=1)
        acc = cb_ref[:, lo:lo + FF_CHUNK]
        for t in reversed(range(CONV_W)):
            s0 = 8 - (CONV_W - 1) + t
            acc = acc + cw_ref[t:t + 1, lo:lo + FF_CHUNK] * ext[:, s0:s0 + tl, :]
        y = acc * _sigmoid(acc)
        if lo < D_M:
            q_ref[:, :, lo:lo + FF_CHUNK] = y.astype(BF16)
        else:
            k_ref[:, :, lo - D_M:lo - D_M + FF_CHUNK] = (y * DH_M ** -0.5).astype(BF16)
        cnew_ref[:, :, lo:lo + FF_CHUNK] = pre[:, tl - (CONV_W - 1):tl, :]
        carry_ref[:, :, lo:lo + FF_CHUNK] = pre[:, tl - 8:tl, :]

    def cols(c0):
        return _dot(hb_ref[...], w_ref[:, c0:c0 + D_M]).reshape(nb, tl, D_M)

    pre = qk_pre(0)
    v_ref[...] = cols(2 * D_M).astype(BF16)
    conv(0, pre)
    pre = qk_pre(1)
    o_ref[...] = cols(3 * D_M)
    conv(1, pre)
    pre = qk_pre(2)
    aq_ref[...] = (cols(4 * D_M) * (DH_A ** -0.5 * LOG2E)).astype(BF16)
    conv(2, pre)
    pre = qk_pre(3)
    ak = cols(4 * D_M + D_A)
    ak_ref[...] = ak.astype(BF16)
    kt_ref[...] = ak
    conv(3, pre)
    av = cols(4 * D_M + 2 * D_A)
    av_ref[...] = av.astype(BF16)
    vt_ref[...] = av
    gates_ref[...] = (_dot(hb_ref[...], wg_ref[...]) + bg_ref[...]).reshape(nb, tl, LANES)


def _proj(x, mod, g_norm_row, w_main, w_gates, b_gates, conv_w, conv_b, conv_prev, *, nb, tl):
    NB, L, d = x.shape
    assert tl == min(ATT_WINDOW, L)
    m = nb * tl
    grid = (NB // nb, L // tl)
    row = lambda w: pl.BlockSpec((nb, tl, w), lambda i, j: (i, j, 0))
    per_b = lambda r, w: pl.BlockSpec((nb, r, w), lambda i, j: (i, 0, 0))
    bshape = lambda w, dt: jax.ShapeDtypeStruct((NB, L, w), dt)
    in_specs = [row(d), per_b(N_MOD, d), _const_spec((1, d)), _const_spec(w_main.shape),
                _const_spec(w_gates.shape), _const_spec((1, LANES)), _const_spec(conv_w.shape),
                _const_spec((1, 2 * D_M)), per_b(CONV_W - 1, 2 * D_M)]
    out_shape = [bshape(D_M, BF16), bshape(D_M, BF16), bshape(D_M, BF16), bshape(D_M, F32),
                 bshape(LANES, F32), bshape(D_A, BF16), bshape(D_A, BF16), bshape(D_A, BF16),
                 jax.ShapeDtypeStruct((NB, tl, D_A), F32), jax.ShapeDtypeStruct((NB, tl, D_A), F32),
                 jax.ShapeDtypeStruct((NB, CONV_W - 1, 2 * D_M), F32)]
    out_specs = [row(D_M), row(D_M), row(D_M), row(D_M), row(LANES), row(D_A), row(D_A), row(D_A),
                 per_b(tl, D_A), per_b(tl, D_A), per_b(CONV_W - 1, 2 * D_M)]
    return pl.pallas_call(
        functools.partial(_proj_kernel, nb=nb, tl=tl),
        out_shape=out_shape,
        grid=grid,
        in_specs=in_specs,
        out_specs=out_specs,
        scratch_shapes=[pltpu.VMEM((m, d), BF16), pltpu.VMEM((nb, 8, 2 * D_M), F32)],
        compiler_params=_cparams(2),
        name="proj",
    )(x, mod, g_norm_row, w_main, w_gates, b_gates, conv_w, conv_b, conv_prev)


def _prefix_max_lanes(x):
    n = x.shape[-1]
    lane = lax.broadcasted_iota(jnp.int32, x.shape, 1)
    shift = 1
    while shift < n:
        x = jnp.maximum(x, jnp.where(lane >= shift, pltpu.roll(x, shift, 1), NEG))
        shift *= 2
    return x


def _split_bf16(x):
    hi = x.astype(BF16)
    r = x - hi.astype(F32)
    mid = r.astype(BF16)
    return hi, mid, (r - mid.astype(F32)).astype(BF16)


def _mlstm_kernel(q_ref, k_ref, v_ref, gates_ref, mo_ref, c0_ref, n0_ref, m0_ref, gm_ref,
                  hm_ref, c_ref, n_ref, m_ref, ct_ref, *, nbm, lc, single_chunk):
    j = pl.program_id(1)

    def transpose_states(src_ref, dst_ref):
        def one(i, carry):
            dst_ref[i // NH_M, i % NH_M] = jnp.transpose(src_ref[i // NH_M, i % NH_M])
            return carry
        lax.fori_loop(0, nbm * NH_M, one, 0)

    if single_chunk:
        n_ref[...] = n0_ref[...]
        m_ref[...] = m0_ref[...]
        ct_get = lambda b, h: jnp.transpose(c0_ref[b, h])

        def ct_put(b, h, val):
            c_ref[b, h] = jnp.transpose(val)
    else:
        @pl.when(j == 0)
        def _():
            transpose_states(c0_ref, ct_ref)
            n_ref[...] = n0_ref[...]
            m_ref[...] = m0_ref[...]
        ct_get = lambda b, h: ct_ref[b, h]

        def ct_put(b, h, val):
            ct_ref[b, h] = val

    s_i = lax.broadcasted_iota(jnp.int32, (lc, lc), 0)
    t_i = lax.broadcasted_iota(jnp.int32, (lc, lc), 1)
    causal = s_i <= t_i
    upper = jnp.where(causal, 1.0, 0.0).astype(BF16)
    head_row = lax.broadcasted_iota(jnp.int32, (8, 1), 0)

    pairs = [(b, h) for b in range(nbm) for h in range(NH_M)]
    hsl = lambda h: slice(h * DH_M, (h + 1) * DH_M)

    rows = []
    for b in range(nbm):
        gates_t = jnp.transpose(gates_ref[b])
        ig_t = gates_t[0:8]
        lf_t = _log_sigmoid(gates_t[8:16])
        b_t = sum(_dot(part, upper) for part in _split_bf16(lf_t))
        c_t = ig_t - b_t
        b_end = b_t[:, lc - 1:lc]
        m_prev = m_ref[b][:, 0:1]
        log_g = b_end + c_t
        m_end = jnp.maximum(b_end + m_prev, jnp.max(log_g, axis=-1, keepdims=True))
        decay = jnp.exp(b_end + m_prev - m_end)
        g_rows = jnp.exp(log_g - m_end)
        log_inter = b_t + m_prev
        c2_t = c_t * LOG2E
        c2_cols = jnp.transpose(jnp.concatenate([c2_t, jnp.zeros((LANES - 8, lc), F32)], axis=0))
        cmax2_t = jnp.concatenate(
            [jnp.max(jnp.where(causal, c2_cols[:, h:h + 1], NEG), axis=0, keepdims=True)
             for h in range(NH_M)] + [jnp.zeros((8 - NH_M, lc), F32)], axis=0)
        m_intra = b_t + cmax2_t * (1.0 / LOG2E)
        m_tok = jnp.maximum(log_inter, m_intra)
        rows.append(dict(decay=decay, g=g_rows, g_b=g_rows.astype(BF16), m_end=m_end,
                         c2_cols=c2_cols, cmax2=cmax2_t, r_intra=jnp.exp(m_intra - m_tok),
                         w_inter=jnp.exp(log_inter - m_tok), floor=jnp.exp(-m_tok),
                         n_prev_b=n_ref[b].astype(BF16)))

    early = {}
    for b, h in pairs:
        q = q_ref[b, :, hsl(h)]
        k = k_ref[b, :, hsl(h)]
        ct_prev = ct_get(b, h)
        early[b, h] = dict(
            ct=ct_prev if single_chunk else None,
            s_t=_dot_nt(k, q),
            cq_t=_dot_nt(ct_prev.astype(BF16), q),
            nq=_dot_nt(rows[b]["n_prev_b"], q)[h:h + 1, :],
            v_t=jnp.transpose(v_ref[b, :, hsl(h)].astype(F32)))

    n_upd = [jnp.zeros((8, DH_M), F32) for _ in range(nbm)]
    for b, h in pairs:
        r, e = rows[b], early[b, h]
        k = k_ref[b, :, hsl(h)]
        d_t = jnp.exp2(r["c2_cols"][:, h:h + 1] - r["cmax2"][h:h + 1, :])
        a_t = jnp.where(causal, d_t, 0.0) * e["s_t"]
        r_intra, w_inter = r["r_intra"][h:h + 1, :], r["w_inter"][h:h + 1, :]
        num_t = _dot(e["v_t"].astype(BF16), a_t.astype(BF16)) * r_intra + e["cq_t"] * w_inter
        den = jnp.sum(a_t, axis=0, keepdims=True) * r_intra + e["nq"] * w_inter
        h_t = num_t * (1.0 / jnp.maximum(jnp.abs(den), r["floor"][h:h + 1, :]))
        hn_t = h_t * lax.rsqrt(jnp.mean(h_t * h_t, axis=0, keepdims=True) + EPS)
        hn = jnp.transpose(hn_t) * gm_ref[:, hsl(h)]
        hm_ref[b, :, hsl(h)] = (_sigmoid(mo_ref[b, :, hsl(h)]) * hn).astype(BF16)

        vg_t = (e["v_t"] * r["g"][h:h + 1, :]).astype(BF16)
        ct_prev = e["ct"] if single_chunk else ct_get(b, h)
        ct_put(b, h, r["decay"][h:h + 1, :] * ct_prev + _dot(vg_t, k))
        n_upd[b] = n_upd[b] + jnp.where(head_row == h, _dot(r["g_b"], k), 0.0)

    for b in range(nbm):
        n_ref[b] = rows[b]["decay"] * n_ref[b] + n_upd[b]
        m_ref[b] = jnp.broadcast_to(rows[b]["m_end"], (8, LANES))

    if not single_chunk:
        @pl.when(j == pl.num_programs(1) - 1)
        def _():
            transpose_states(ct_ref, c_ref)


def _mlstm(q, k, v, gates, mo, c0, n0, m0, g_mlstm_row, *, nbm, lc):
    NB, L, _ = q.shape
    grid = (NB // nbm, L // lc)
    row = lambda w: pl.BlockSpec((nbm, lc, w), lambda i, j: (i, j, 0))
    st = lambda s: pl.BlockSpec((nbm,) + s, lambda i, j: (i,) + (0,) * len(s))
    c_s, n_s = (NH_M, DH_M, DH_M), (8, DH_M)
    return pl.pallas_call(
        functools.partial(_mlstm_kernel, nbm=nbm, lc=lc, single_chunk=L == lc),
        out_shape=[jax.ShapeDtypeStruct((NB, L, D_M), BF16),
                   jax.ShapeDtypeStruct((NB,) + c_s, F32),
                   jax.ShapeDtypeStruct((NB,) + n_s, F32),
                   jax.ShapeDtypeStruct((NB,) + n_s, F32)],
        grid=grid,
        in_specs=[row(D_M), row(D_M), row(D_M), row(LANES), row(D_M), st(c_s), st(n_s), st(n_s),
                  _const_spec((1, D_M))],
        out_specs=[row(D_M), st(c_s), st(n_s), st(n_s)],
        scratch_shapes=[pltpu.VMEM((nbm,) + c_s, F32)],
        compiler_params=_cparams(2),
        name="mlstm",
    )(q, k, v, gates, mo, c0, n0, m0, g_mlstm_row)


def _build_band_bias(ext_ref, tq, klen, put):
    p_len = ext_ref.shape[-1]
    q_pos = lax.broadcasted_iota(jnp.int32, (tq, klen), 0) + (klen - tq)
    k_pos = lax.broadcasted_iota(jnp.int32, (tq, klen), 1)
    back = q_pos // CHUNK - k_pos // CHUNK
    for h in range(NH_A):
        base = jnp.broadcast_to(ext_ref[h:h + 1, :] * LOG2E, (tq, p_len))
        toeplitz = pltpu.roll(base, 0, 1, stride=1, stride_axis=0)[:, :klen]
        put(h, jnp.where(back >= 0, jnp.where(back <= LEFT_CHUNKS, toeplitz, NEG), NEG))


def _lane_tiles(s):
    n = s.shape[-1]
    if n % LANES:
        return [s]
    return [s[:, i * LANES:(i + 1) * LANES] for i in range(n // LANES)]


def _row_reduce(tiles, op, lane_op):
    by_width = {}
    for t in tiles:
        by_width[t.shape[-1]] = t if t.shape[-1] not in by_width else op(by_width[t.shape[-1]], t)
    return functools.reduce(op, [lane_op(t, axis=-1, keepdims=True) for t in by_width.values()])


def _softmax_pv(ss, vs):
    mx = _row_reduce([t for s in ss for t in _lane_tiles(s)], jnp.maximum, jnp.max)
    es = [jnp.exp2(s - mx) for s in ss]
    l = _row_reduce([t for e in es for t in _lane_tiles(e)], jnp.add, jnp.sum)
    o = functools.reduce(jnp.add, [_dot(e.astype(BF16), v) for e, v in zip(es, vs)])
    return o / l


def _attn_pairs(jobs):
    first = lax.broadcasted_iota(jnp.int32, (1, LANES), 1) < DH_A
    zero = jnp.zeros((), BF16)

    def scores(q, segs, p):
        sl = slice(p * LANES, (p + 1) * LANES)
        qp = q(sl)
        q2 = jnp.concatenate([jnp.where(first, qp, zero), jnp.where(first, zero, qp)], axis=0)
        ss = []
        for k, _, bias, pen in segs:
            s = _dot_nt(q2, k(sl)) + bias(p)
            ss.append(s if pen is None else s + pen)
        return ss

    def finish(ss, segs, put, p):
        sl = slice(p * LANES, (p + 1) * LANES)
        r = ss[0].shape[0] // 2
        o2 = _softmax_pv(ss, [v(sl) for _, v, _, _ in segs])
        put(sl, jnp.where(first, o2[:r], o2[r:]))

    pending = None
    for q, segs, put in jobs:
        for p in range(NH_A // 2):
            ss = scores(q, segs, p)
            if pending is not None:
                finish(*pending)
            pending = (ss, segs, put, p)
    finish(*pending)


def _attn_prompt_kernel(q_ref, kp_ref, ko_ref, vp_ref, vo_ref, ext_ref, o_ref, bias_ref):
    tq = q_ref.shape[1]
    j = pl.program_id(1)

    @pl.when((pl.program_id(0) == 0) & (j == 0))
    def _():
        def put_bias(h, tile):
            bias_ref[h // 2, (h % 2) * ATT_SUB:(h % 2 + 1) * ATT_SUB, :] = tile
        _build_band_bias(ext_ref, ATT_SUB, tq + ATT_SUB, put_bias)

    def run(pen_prev):
        jobs = []
        for i in range(tq // ATT_SUB):
            lo = i * ATT_SUB
            n_prev = tq - lo
            bias = lambda a, b: (lambda p: bias_ref[p, :, a:b])
            rows = lambda ref, a, b: (lambda sl: ref[0, a:b, sl])

            def put(sl, val, lo=lo):
                o_ref[0, lo:lo + ATT_SUB, sl] = val.astype(o_ref.dtype)

            segs = [(rows(kp_ref, lo, tq), rows(vp_ref, lo, tq), bias(0, n_prev), pen_prev),
                    (rows(ko_ref, 0, lo + ATT_SUB), rows(vo_ref, 0, lo + ATT_SUB),
                     bias(n_prev, tq + ATT_SUB), None)]
            jobs.append((rows(q_ref, lo, lo + ATT_SUB), segs, put))
        _attn_pairs(jobs)

    @pl.when(j >= 1)
    def _():
        run(None)

    @pl.when(j < 1)
    def _():
        run(jnp.full((), NEG, F32))


def _attn_prompt(aq, ak, av, ext):
    NB, L, _ = aq.shape
    tq = ATT_WINDOW
    assert L % tq == 0
    grid = (NB, L // tq)
    row = pl.BlockSpec((1, tq, D_A), lambda b, j: (b, j, 0))
    prev = pl.BlockSpec((1, tq, D_A), lambda b, j: (b, jnp.maximum(j - 1, 0), 0))
    return pl.pallas_call(
        _attn_prompt_kernel,
        out_shape=jax.ShapeDtypeStruct((NB, L, D_A), BF16),
        grid=grid,
        in_specs=[row, prev, row, prev, row, _const_spec(ext.shape)],
        out_specs=row,
        scratch_shapes=[pltpu.VMEM((NH_A // 2, 2 * ATT_SUB, tq + ATT_SUB), F32)],
        compiler_params=_cparams(2),
        name="attn_prompt",
    )(aq, ak, ak, av, av, ext)


def _attn_sample_kernel(q_ref, k_ref, v_ref, ckt_ref, cvt_ref, ext_ref, o_ref, bias_ref, *, nb):
    tq = q_ref.shape[1]
    nc = ckt_ref.shape[-1]

    @pl.when(pl.program_id(0) == 0)
    def _():
        def put_bias(h, tile):
            bias_ref[h] = tile
        _build_band_bias(ext_ref, tq, nc + tq, put_bias)

    def scores(b, h):
        hs = slice(h * DH_A, (h + 1) * DH_A)
        qh = q_ref[b, :, hs]
        return [_dot(qh, ckt_ref[b, h].astype(BF16)) + bias_ref[h, :, 0:nc],
                _dot_nt(qh, k_ref[b, :, hs]) + bias_ref[h, :, nc:nc + tq]]

    def finish(b, h, ss):
        hs = slice(h * DH_A, (h + 1) * DH_A)
        mx = jnp.maximum(*[jnp.max(s, axis=-1, keepdims=True) for s in ss])
        e_c, e_o = [jnp.exp2(s - mx) for s in ss]
        l = jnp.sum(e_c, axis=-1, keepdims=True) + jnp.sum(e_o, axis=-1, keepdims=True)
        o = (_dot_nt(e_c.astype(BF16), cvt_ref[b, h].astype(BF16))
             + _dot(e_o.astype(BF16), v_ref[b, :, hs]))
        return o / l

    outs = {b: [] for b in range(nb)}
    pending = []

    def retire():
        b, h, ss = pending.pop(0)
        outs[b].append(finish(b, h, ss))
        if h == NH_A - 1:
            o_ref[b] = jnp.concatenate(outs[b], axis=-1).astype(o_ref.dtype)

    for b in range(nb):
        for h in range(NH_A):
            pending.append((b, h, scores(b, h)))
            if len(pending) > SAMPLE_ATT_LOOKAHEAD:
                retire()
    while pending:
        retire()


def _attn_sample(aq, ak, av, cache_kt, cache_vt, ext, *, nb):
    NB, L, _ = aq.shape
    nc = cache_kt.shape[-1]
    row = pl.BlockSpec((nb, L, D_A), lambda i: (i, 0, 0))
    crow = pl.BlockSpec((nb, NH_A, DH_A, nc), lambda i: (i, 0, 0, 0))
    return pl.pallas_call(
        functools.partial(_attn_sample_kernel, nb=nb),
        out_shape=jax.ShapeDtypeStruct((NB, L, D_A), BF16),
        grid=(NB // nb,),
        in_specs=[row, row, row, crow, crow, _const_spec(ext.shape)],
        out_specs=row,
        scratch_shapes=[pltpu.VMEM((NH_A, L, nc + L), F32)],
        compiler_params=_cparams(1),
        name="attn_sample",
    )(aq, ak, av, cache_kt, cache_vt, ext)


def _rel_ext(rel_table, tq, klen):
    p_len = -(-(klen + tq) // LANES) * LANES
    n_lo = tq - CHUNK
    n_hi = p_len - tq - MAX_REL
    rep = lambda col, n: jnp.broadcast_to(col, (NH_A, n))
    by_dist = jnp.concatenate([rep(rel_table[:, :1], n_lo), rel_table, rep(rel_table[:, -1:], n_hi)],
                              axis=1)
    rev = by_dist[:, ::-1]
    n = p_len - klen
    return jnp.concatenate([rev[:, n:], rev[:, :n]], axis=1).astype(F32)


def _gate_lanes(g):
    z = lambda n: jnp.zeros((g.shape[0], n), g.dtype)
    return jnp.concatenate([g[:, :NH_M], z(8 - NH_M), g[:, NH_M:], z(LANES - 8 - NH_M)], axis=1)


def _layer(x, mod, conv_prev, state, cache, w, *, nb, tl, nbm, lc, final):
    NB, L, d = x.shape
    x1 = _ffn(x, mod, w["g0"], w["up1"], w["dn1"], w["g_final"], nb=nb, tl=tl, sub=0)
    (q, k, v, mo, gates, aq, ak, av, k_tail, v_tail, conv_new) = _proj(
        x1, mod, w["g1"], w["w_main"], w["w_gates"], w["b_gates"], w["conv_w"], w["conv_b"],
        conv_prev, nb=nb, tl=tl)
    c0, n0, m0 = state
    n0 = jnp.pad(n0, ((0, 0), (0, 8 - NH_M), (0, 0)))
    m0 = jnp.broadcast_to(jnp.pad(m0, ((0, 0), (0, 8 - NH_M)))[:, :, None], (NB, 8, LANES))
    hm, c_new, n_new, m_new = _mlstm(q, k, v, gates, mo, c0, n0, m0, w["g_mlstm"], nbm=nbm, lc=lc)
    if cache is None:
        att = _attn_prompt(aq, ak, av, w["ext_prompt"])
    else:
        att = _attn_sample(aq, ak, av, cache[0], cache[1], w["ext_sample"], nb=min(NB, 4))
    y = _ffn(x1, mod, w["g2"], w["up2"], w["dn2"], w["g_final"], nb=nb, tl=tl, sub=2,
             mix=(hm, att, w["w_out"]), final=final)
    n_keep = k_tail.shape[1]
    states = (c_new, n_new[:, :NH_M, :], m_new[:, :NH_M, 0], conv_new,
              k_tail.reshape(NB, n_keep, NH_A, DH_A), v_tail.reshape(NB, n_keep, NH_A, DH_A))
    return y, states


def kernel(x_prompt, x_sample, state_mlstm_C, state_mlstm_n, state_mlstm_m, state_conv, cache_att_k, cache_att_v, c_prompt, c_sample, w_ada, b_ada, g_norm, w_ffn1_up, w_ffn1_down, w_ffn2_up, w_ffn2_down, w_in, conv_w, conv_b, b_gates, g_mlstm, rel_bias_table, w_out, g_final):
    depth = w_ada.shape[0]
    bp, seq, d = x_prompt.shape
    bs, dseq, _ = x_sample.shape
    xp, xs = x_prompt, x_sample
    st_p, st_s = [], []
    n_c = cache_att_k.shape[2]
    tl_p = min(ATT_WINDOW, seq)
    lc_p = min(MLSTM_CHUNK, seq)
    nb_s = max(1, min(bs, ATT_WINDOW // dseq))
    for l in range(depth):
        rows = bp + bs
        pad = (-rows) % 8
        c_all = jnp.concatenate([c_prompt, c_sample, jnp.zeros((pad, d), F32)], axis=0)
        mod = _adaln(c_all, w_ada[l], b_ada[l])
        mod_p = mod[:bp].reshape(bp, N_MOD, d)
        mod_s = mod[bp:rows].reshape(bs, N_MOD, d)
        off_g = 4 * D_M
        off_a = off_g + 2 * NH_M
        wl = w_in[l]
        rel = rel_bias_table[l]
        w = {
            "g0": g_norm[l, 0:1], "g1": g_norm[l, 1:2], "g2": g_norm[l, 2:3],
            "g_final": g_final.reshape(1, d),
            "up1": w_ffn1_up[l].astype(BF16), "dn1": w_ffn1_down[l].astype(BF16),
            "up2": w_ffn2_up[l].astype(BF16), "dn2": w_ffn2_down[l].astype(BF16),
            "w_main": jnp.concatenate([wl[:, :off_g], wl[:, off_a:]], axis=1).astype(BF16),
            "w_gates": _gate_lanes(wl[:, off_g:off_a]).astype(BF16),
            "b_gates": _gate_lanes(b_gates[l].reshape(1, 2 * NH_M)),
            "conv_w": conv_w[l], "conv_b": conv_b[l].reshape(1, 2 * D_M),
            "g_mlstm": g_mlstm[l].reshape(1, D_M),
            "w_out": w_out[l].astype(BF16),
            "ext_prompt": _rel_ext(rel, ATT_SUB, ATT_WINDOW + ATT_SUB),
            "ext_sample": _rel_ext(rel, dseq, n_c + dseq),
        }
        zero_state = (jnp.zeros((bp, NH_M, DH_M, DH_M), F32), jnp.zeros((bp, NH_M, DH_M), F32),
                      jnp.zeros((bp, NH_M), F32))
        xp, sp = _layer(xp, mod_p, jnp.zeros((bp, CONV_W - 1, 2 * D_M), F32), zero_state, None, w,
                        nb=1, tl=tl_p, nbm=bp, lc=lc_p, final=l == depth - 1)
        state_s = (state_mlstm_C[l], state_mlstm_n[l], state_mlstm_m[l])
        cache = (jnp.transpose(cache_att_k[l], (0, 2, 3, 1)), jnp.transpose(cache_att_v[l], (0, 2, 3, 1)))
        xs, ss = _layer(xs, mod_s, state_conv[l], state_s, cache, w, nb=nb_s, tl=dseq,
                        nbm=min(bs, 4), lc=dseq, final=l == depth - 1)
        st_p.append(sp)
        st_s.append(ss)
    stk = lambda sts, i: jnp.stack([s[i] for s in sts])
    return ((xp, xs) + tuple(stk(st_p, i) for i in range(6)) + tuple(stk(st_s, i) for i in range(6)))
```

```python
import functools

import jax
import jax.numpy as jnp
from jax import lax
from jax.experimental import pallas as pl
from jax.experimental.pallas import tpu as pltpu

F32 = jnp.float32
BF16 = jnp.bfloat16

CHUNK = 64
NH_M = 4
DH_M = 128
D_M = NH_M * DH_M
NH_A = 8
DH_A = 64
D_A = NH_A * DH_A
CONV_W = 4
LEFT_CHUNKS = 8
ATT_WINDOW = LEFT_CHUNKS * CHUNK
MAX_REL = 2 * CHUNK
N_MOD = 9
EPS = 1e-6
NEG = -1e30
LOG2E = 1.4426950408889634

LANES = 128
V7X_VMEM_BYTES = 64 * 1024 * 1024
VMEM_LIMIT = V7X_VMEM_BYTES - 8 * 1024 * 1024

FF_CHUNK = 256
MLSTM_CHUNK = 256
MLSTM_CHUNKS_PER_STEP = 2
ATT_SUB = 2 * CHUNK
SAMPLE_ATT_LOOKAHEAD = 3


def _cparams(n_axes):
    return pltpu.CompilerParams(dimension_semantics=("arbitrary",) * n_axes,
                                vmem_limit_bytes=VMEM_LIMIT)


def _const_spec(shape):
    nd = len(shape)
    return pl.BlockSpec(shape, lambda *_: (0,) * nd, pipeline_mode=pl.Buffered(1))


def _dot(a, b):
    return jnp.dot(a, b, preferred_element_type=F32)


def _dot_nt(a, b):
    return lax.dot_general(a, b, (((1,), (1,)), ((), ())), preferred_element_type=F32)


def _dot_tn(a, b):
    return lax.dot_general(a, b, (((0,), (0,)), ((), ())), preferred_element_type=F32)


def _sigmoid(x):
    return 1.0 / (1.0 + jnp.exp(-x))


def _log_sigmoid(x):
    return jnp.minimum(x, 0.0) - jnp.log(1.0 + jnp.exp(-jnp.abs(x)))


def _norm_mod(x, g, shift, scale):
    ms = jnp.mean(x * x, axis=-1, keepdims=True)
    h = x * lax.rsqrt(ms + EPS) * g
    return h * (1.0 + scale) + shift


def _adaln_kernel(c_ref, w_ref, b_ref, o_ref):
    c = c_ref[...]
    o_ref[...] = _dot(c * _sigmoid(c), w_ref[...]) + b_ref[...]


def _adaln(c_all, w_ada, b_ada):
    rows, d = c_all.shape
    n = w_ada.shape[1]
    tn = d
    return pl.pallas_call(
        _adaln_kernel,
        out_shape=jax.ShapeDtypeStruct((rows, n), F32),
        grid=(n // tn,),
        in_specs=[pl.BlockSpec((rows, d), lambda j: (0, 0)),
                  pl.BlockSpec((d, tn), lambda j: (0, j)),
                  pl.BlockSpec((1, tn), lambda j: (0, j))],
        out_specs=pl.BlockSpec((rows, tn), lambda j: (0, j)),
        compiler_params=_cparams(1),
        name="adaln",
    )(c_all, w_ada, b_ada.reshape(1, n))


def _ffn_kernel(*refs, nb, tl, sub, with_mix, with_final):
    if with_mix:
        x_ref, hm_ref, att_ref, mod_ref, gn_ref, wout_ref, wup_ref, wdn_ref, gfin_ref, o_ref, hb_ref, act_ref = refs
    else:
        x_ref, mod_ref, gn_ref, wup_ref, wdn_ref, gfin_ref, o_ref, hb_ref, act_ref = refs
    m = nb * tl
    d = x_ref.shape[-1]
    dff = wdn_ref.shape[0]
    mod = mod_ref[...]
    shift, scale, gate = (mod[:, 3 * sub + i:3 * sub + i + 1, :] for i in range(3))

    if with_mix:
        mix = _dot(hm_ref[...].reshape(m, D_M), wout_ref[0:D_M, :])
        mix = mix + _dot(att_ref[...].reshape(m, D_A), wout_ref[D_M:, :])
        o_ref[...] = x_ref[...] + mod[:, 5:6, :] * mix.reshape(nb, tl, d)
    else:
        o_ref[...] = x_ref[...]

    h = _norm_mod(o_ref[...], gn_ref[...], shift, scale)
    hb_ref[...] = h.reshape(m, d).astype(BF16)
    for c in range(dff // FF_CHUNK):
        lo = c * FF_CHUNK
        g = _dot(hb_ref[...], wup_ref[:, lo:lo + FF_CHUNK])
        u = _dot(hb_ref[...], wup_ref[:, dff + lo:dff + lo + FF_CHUNK])
        act_ref[:, lo:lo + FF_CHUNK] = (g * _sigmoid(g) * u).astype(BF16)
    dn = _dot(act_ref[...], wdn_ref[...])
    y = o_ref[...] + 0.5 * gate * dn.reshape(nb, tl, d)
    if with_final:
        ms = jnp.mean(y * y, axis=-1, keepdims=True)
        y = y * lax.rsqrt(ms + EPS) * gfin_ref[...]
    o_ref[...] = y


def _ffn(x, mod, g_norm_row, w_up, w_dn, g_final, *, nb, tl, sub, mix=None, final=False):
    NB, L, d = x.shape
    dff = w_dn.shape[0]
    m = nb * tl
    grid = (NB // nb, L // tl)
    row = lambda w: pl.BlockSpec((nb, tl, w), lambda i, j: (i, j, 0))
    in_specs = [row(d)]
    args = [x]
    if mix is not None:
        hm, att, w_out = mix
        in_specs += [row(D_M), row(D_A)]
        args += [hm, att]
    in_specs += [pl.BlockSpec((nb, N_MOD, d), lambda i, j: (i, 0, 0)), _const_spec((1, d))]
    args += [mod, g_norm_row]
    if mix is not None:
        in_specs.append(_const_spec(w_out.shape))
        args.append(w_out)
    in_specs += [_const_spec(w_up.shape), _const_spec(w_dn.shape), _const_spec((1, d))]
    args += [w_up, w_dn, g_final]
    kern = functools.partial(_ffn_kernel, nb=nb, tl=tl, sub=sub, with_mix=mix is not None,
                             with_final=final)
    return pl.pallas_call(
        kern,
        out_shape=jax.ShapeDtypeStruct((NB, L, d), F32),
        grid=grid,
        in_specs=in_specs,
        out_specs=row(d),
        scratch_shapes=[pltpu.VMEM((m, d), BF16), pltpu.VMEM((m, dff), BF16)],
        compiler_params=_cparams(2),
        name="ffn%d" % sub,
    )(*args)


def _proj_kernel(x_ref, mod_ref, gn_ref, w_ref, wg_ref, bg_ref, cw_ref, cb_ref, cprev_ref,
                 q_ref, k_ref, v_ref, o_ref, gates_ref, aq_ref, ak_ref, av_ref,
                 kt_ref, vt_ref, cnew_ref, hb_ref, carry_ref, *, nb, tl):
    m = nb * tl
    d = x_ref.shape[-1]
    j = pl.program_id(1)
    mod = mod_ref[...]
    h = _norm_mod(x_ref[...], gn_ref[...], mod[:, 3:4, :], mod[:, 4:5, :])
    hb_ref[...] = h.reshape(m, d).astype(BF16)

    @pl.when(j == 0)
    def _():
        carry_ref[...] = jnp.zeros(carry_ref.shape, F32)
        carry_ref[:, 8 - (CONV_W - 1):8, :] = cprev_ref[...]

    def qk_pre(c):
        lo = c * FF_CHUNK
        return _dot(hb_ref[...], w_ref[:, lo:lo + FF_CHUNK]).reshape(nb, tl, FF_CHUNK)

    def conv(c, pre):
        lo = c * FF_CHUNK
        ext = jnp.concatenate([carry_ref[:, :, lo:lo + FF_CHUNK], pre], axis=1)
        acc = cb_ref[:, lo:lo + FF_CHUNK]
        for t in reversed(range(CONV_W)):
            s0 = 8 - (CONV_W - 1) + t
            acc = acc + cw_ref[t:t + 1, lo:lo + FF_CHUNK] * ext[:, s0:s0 + tl, :]
        y = acc * _sigmoid(acc)
        if lo < D_M:
            q_ref[:, :, lo:lo + FF_CHUNK] = y.astype(BF16)
        else:
            k_ref[:, :, lo - D_M:lo - D_M + FF_CHUNK] = (y * DH_M ** -0.5).astype(BF16)
        cnew_ref[:, :, lo:lo + FF_CHUNK] = pre[:, tl - (CONV_W - 1):tl, :]
        carry_ref[:, :, lo:lo + FF_CHUNK] = pre[:, tl - 8:tl, :]

    def cols(c0):
        return _dot(hb_ref[...], w_ref[:, c0:c0 + D_M]).reshape(nb, tl, D_M)

    pre = qk_pre(0)
    v_ref[...] = cols(2 * D_M).astype(BF16)
    conv(0, pre)
    pre = qk_pre(1)
    o_ref[...] = cols(3 * D_M)
    conv(1, pre)
    pre = qk_pre(2)
    aq_ref[...] = (cols(4 * D_M) * (DH_A ** -0.5 * LOG2E)).astype(BF16)
    conv(2, pre)
    pre = qk_pre(3)
    ak = cols(4 * D_M + D_A)
    ak_ref[...] = ak.astype(BF16)
    kt_ref[...] = ak
    conv(3, pre)
    av = cols(4 * D_M + 2 * D_A)
    av_ref[...] = av.astype(BF16)
    vt_ref[...] = av
    gates_ref[...] = (_dot(hb_ref[...], wg_ref[...]) + bg_ref[...]).reshape(nb, tl, LANES)


def _proj(x, mod, g_norm_row, w_main, w_gates, b_gates, conv_w, conv_b, conv_prev, *, nb, tl):
    NB, L, d = x.shape
    assert tl == min(ATT_WINDOW, L)
    m = nb * tl
    grid = (NB // nb, L // tl)
    row = lambda w: pl.BlockSpec((nb, tl, w), lambda i, j: (i, j, 0))
    per_b = lambda r, w: pl.BlockSpec((nb, r, w), lambda i, j: (i, 0, 0))
    bshape = lambda w, dt: jax.ShapeDtypeStruct((NB, L, w), dt)
    in_specs = [row(d), per_b(N_MOD, d), _const_spec((1, d)), _const_spec(w_main.shape),
                _const_spec(w_gates.shape), _const_spec((1, LANES)), _const_spec(conv_w.shape),
                _const_spec((1, 2 * D_M)), per_b(CONV_W - 1, 2 * D_M)]
    out_shape = [bshape(D_M, BF16), bshape(D_M, BF16), bshape(D_M, BF16), bshape(D_M, F32),
                 bshape(LANES, F32), bshape(D_A, BF16), bshape(D_A, BF16), bshape(D_A, BF16),
                 jax.ShapeDtypeStruct((NB, tl, D_A), F32), jax.ShapeDtypeStruct((NB, tl, D_A), F32),
                 jax.ShapeDtypeStruct((NB, CONV_W - 1, 2 * D_M), F32)]
    out_specs = [row(D_M), row(D_M), row(D_M), row(D_M), row(LANES), row(D_A), row(D_A), row(D_A),
                 per_b(tl, D_A), per_b(tl, D_A), per_b(CONV_W - 1, 2 * D_M)]
    return pl.pallas_call(
        functools.partial(_proj_kernel, nb=nb, tl=tl),
        out_shape=out_shape,
        grid=grid,
        in_specs=in_specs,
        out_specs=out_specs,
        scratch_shapes=[pltpu.VMEM((m, d), BF16), pltpu.VMEM((nb, 8, 2 * D_M), F32)],
        compiler_params=_cparams(2),
        name="proj",
    )(x, mod, g_norm_row, w_main, w_gates, b_gates, conv_w, conv_b, conv_prev)


def _prefix_max_lanes(x):
    n = x.shape[-1]
    lane = lax.broadcasted_iota(jnp.int32, x.shape, 1)
    shift = 1
    while shift < n:
        x = jnp.maximum(x, jnp.where(lane >= shift, pltpu.roll(x, shift, 1), NEG))
        shift *= 2
    return x


def _split_bf16(x):
    hi = x.astype(BF16)
    r = x - hi.astype(F32)
    mid = r.astype(BF16)
    return hi, mid, (r - mid.astype(F32)).astype(BF16)


def _mlstm_kernel(q_ref, k_ref, v_ref, gates_ref, mo_ref, c0_ref, n0_ref, m0_ref, gm_ref,
                  hm_ref, c_ref, n_ref, m_ref, ct_ref, *, nbm, lc, nck, single_chunk):
    j = pl.program_id(1)

    def transpose_states(src_ref, dst_ref):
        def one(i, carry):
            dst_ref[i // NH_M, i % NH_M] = jnp.transpose(src_ref[i // NH_M, i % NH_M])
            return carry
        lax.fori_loop(0, nbm * NH_M, one, 0)

    if single_chunk:
        n_ref[...] = n0_ref[...]
        m_ref[...] = m0_ref[...]
        ct_get = lambda b, h: jnp.transpose(c0_ref[b, h])

        def ct_put(b, h, val):
            c_ref[b, h] = jnp.transpose(val)
    else:
        @pl.when(j == 0)
        def _():
            transpose_states(c0_ref, ct_ref)
            n_ref[...] = n0_ref[...]
            m_ref[...] = m0_ref[...]
        ct_get = lambda b, h: ct_ref[b, h]

        def ct_put(b, h, val):
            ct_ref[b, h] = val

    s_i = lax.broadcasted_iota(jnp.int32, (lc, lc), 0)
    t_i = lax.broadcasted_iota(jnp.int32, (lc, lc), 1)
    causal = s_i <= t_i
    upper = jnp.where(causal, 1.0, 0.0).astype(BF16)
    head_row = lax.broadcasted_iota(jnp.int32, (8, 1), 0)

    pairs = [(b, h) for b in range(nbm) for h in range(NH_M)]
    units = [(ck, b) for ck in range(nck) for b in range(nbm)]
    hsl = lambda h: slice(h * DH_M, (h + 1) * DH_M)
    tok = lambda ck: slice(ck * lc, (ck + 1) * lc)

    gate = {}
    for ck, b in units:
        gates_t = jnp.transpose(gates_ref[b, tok(ck)])
        ig_t = gates_t[0:8]
        lf_t = _log_sigmoid(gates_t[8:16])
        b_t = sum(_dot(part, upper) for part in _split_bf16(lf_t))
        c_t = ig_t - b_t
        c2_t = c_t * LOG2E
        c2_cols = jnp.transpose(jnp.concatenate([c2_t, jnp.zeros((LANES - 8, lc), F32)], axis=0))
        cmax2_t = jnp.concatenate(
            [jnp.max(jnp.where(causal, c2_cols[:, h:h + 1], NEG), axis=0, keepdims=True)
             for h in range(NH_M)] + [jnp.zeros((8 - NH_M, lc), F32)], axis=0)
        gate[ck, b] = dict(b_t=b_t, c_t=c_t, c2_cols=c2_cols, cmax2=cmax2_t,
                           m_intra=b_t + cmax2_t * (1.0 / LOG2E))

    early = {}
    for ck, b in units:
        for h in range(NH_M):
            early[ck, b, h] = dict(
                s_t=_dot_nt(k_ref[b, tok(ck), hsl(h)], q_ref[b, tok(ck), hsl(h)]),
                v_t=jnp.transpose(v_ref[b, tok(ck), hsl(h)].astype(F32)))

    for ck in range(nck):
        rows = []
        for b in range(nbm):
            g = gate[ck, b]
            b_t, c_t = g["b_t"], g["c_t"]
            b_end = b_t[:, lc - 1:lc]
            m_prev = m_ref[b][:, 0:1]
            log_g = b_end + c_t
            m_end = jnp.maximum(b_end + m_prev, jnp.max(log_g, axis=-1, keepdims=True))
            g_rows = jnp.exp(log_g - m_end)
            log_inter = b_t + m_prev
            m_tok = jnp.maximum(log_inter, g["m_intra"])
            rows.append(dict(decay=jnp.exp(b_end + m_prev - m_end), g=g_rows,
                             g_b=g_rows.astype(BF16), m_end=m_end,
                             r_intra=jnp.exp(g["m_intra"] - m_tok),
                             w_inter=jnp.exp(log_inter - m_tok), floor=jnp.exp(-m_tok),
                             n_prev_b=n_ref[b].astype(BF16)))

        for b, h in pairs:
            q = q_ref[b, tok(ck), hsl(h)]
            ct_prev = ct_get(b, h)
            early[ck, b, h].update(
                ct=ct_prev if single_chunk else None,
                cq_t=_dot_nt(ct_prev.astype(BF16), q),
                nq=_dot_nt(rows[b]["n_prev_b"], q)[h:h + 1, :])

        n_upd = [jnp.zeros((8, DH_M), F32) for _ in range(nbm)]
        for b, h in pairs:
            r, g, e = rows[b], gate[ck, b], early[ck, b, h]
            k = k_ref[b, tok(ck), hsl(h)]
            d_t = jnp.exp2(g["c2_cols"][:, h:h + 1] - g["cmax2"][h:h + 1, :])
            a_t = jnp.where(causal, d_t, 0.0) * e["s_t"]
            r_intra, w_inter = r["r_intra"][h:h + 1, :], r["w_inter"][h:h + 1, :]
            num_t = _dot(e["v_t"].astype(BF16), a_t.astype(BF16)) * r_intra + e["cq_t"] * w_inter
            den = jnp.sum(a_t, axis=0, keepdims=True) * r_intra + e["nq"] * w_inter
            h_t = num_t * (1.0 / jnp.maximum(jnp.abs(den), r["floor"][h:h + 1, :]))
            hn_t = h_t * lax.rsqrt(jnp.mean(h_t * h_t, axis=0, keepdims=True) + EPS)
            hn = jnp.transpose(hn_t) * gm_ref[:, hsl(h)]
            hm_ref[b, tok(ck), hsl(h)] = (_sigmoid(mo_ref[b, tok(ck), hsl(h)]) * hn).astype(BF16)

            vg_t = (e["v_t"] * r["g"][h:h + 1, :]).astype(BF16)
            ct_prev = e["ct"] if single_chunk else ct_get(b, h)
            ct_put(b, h, r["decay"][h:h + 1, :] * ct_prev + _dot(vg_t, k))
            n_upd[b] = n_upd[b] + jnp.where(head_row == h, _dot(r["g_b"], k), 0.0)

        for b in range(nbm):
            n_ref[b] = rows[b]["decay"] * n_ref[b] + n_upd[b]
            m_ref[b] = jnp.broadcast_to(rows[b]["m_end"], (8, LANES))

    if not single_chunk:
        @pl.when(j == pl.num_programs(1) - 1)
        def _():
            transpose_states(ct_ref, c_ref)


def _mlstm(q, k, v, gates, mo, c0, n0, m0, g_mlstm_row, *, nbm, lc):
    NB, L, _ = q.shape
    nck = MLSTM_CHUNKS_PER_STEP if L % (MLSTM_CHUNKS_PER_STEP * lc) == 0 else 1
    grid = (NB // nbm, L // (nck * lc))
    row = lambda w: pl.BlockSpec((nbm, nck * lc, w), lambda i, j: (i, j, 0))
    st = lambda s: pl.BlockSpec((nbm,) + s, lambda i, j: (i,) + (0,) * len(s))
    c_s, n_s = (NH_M, DH_M, DH_M), (8, DH_M)
    return pl.pallas_call(
        functools.partial(_mlstm_kernel, nbm=nbm, lc=lc, nck=nck, single_chunk=L == lc),
        out_shape=[jax.ShapeDtypeStruct((NB, L, D_M), BF16),
                   jax.ShapeDtypeStruct((NB,) + c_s, F32),
                   jax.ShapeDtypeStruct((NB,) + n_s, F32),
                   jax.ShapeDtypeStruct((NB,) + n_s, F32)],
        grid=grid,
        in_specs=[row(D_M), row(D_M), row(D_M), row(LANES), row(D_M), st(c_s), st(n_s), st(n_s),
                  _const_spec((1, D_M))],
        out_specs=[row(D_M), st(c_s), st(n_s), st(n_s)],
        scratch_shapes=[pltpu.VMEM((nbm,) + c_s, F32)],
        compiler_params=_cparams(2),
        name="mlstm",
    )(q, k, v, gates, mo, c0, n0, m0, g_mlstm_row)


def _build_band_bias(ext_ref, tq, klen, put):
    p_len = ext_ref.shape[-1]
    q_pos = lax.broadcasted_iota(jnp.int32, (tq, klen), 0) + (klen - tq)
    k_pos = lax.broadcasted_iota(jnp.int32, (tq, klen), 1)
    back = q_pos // CHUNK - k_pos // CHUNK
    for h in range(NH_A):
        base = jnp.broadcast_to(ext_ref[h:h + 1, :] * LOG2E, (tq, p_len))
        toeplitz = pltpu.roll(base, 0, 1, stride=1, stride_axis=0)[:, :klen]
        put(h, jnp.where(back >= 0, jnp.where(back <= LEFT_CHUNKS, toeplitz, NEG), NEG))


def _lane_tiles(s):
    n = s.shape[-1]
    if n % LANES:
        return [s]
    return [s[:, i * LANES:(i + 1) * LANES] for i in range(n // LANES)]


def _row_reduce(tiles, op, lane_op):
    by_width = {}
    for t in tiles:
        by_width[t.shape[-1]] = t if t.shape[-1] not in by_width else op(by_width[t.shape[-1]], t)
    return functools.reduce(op, [lane_op(t, axis=-1, keepdims=True) for t in by_width.values()])


def _softmax_pv(ss, vs):
    mx = _row_reduce([t for s in ss for t in _lane_tiles(s)], jnp.maximum, jnp.max)
    es = [jnp.exp2(s - mx) for s in ss]
    l = _row_reduce([t for e in es for t in _lane_tiles(e)], jnp.add, jnp.sum)
    o = functools.reduce(jnp.add, [_dot(e.astype(BF16), v) for e, v in zip(es, vs)])
    return o / l


def _attn_pairs(jobs):
    first = lax.broadcasted_iota(jnp.int32, (1, LANES), 1) < DH_A
    zero = jnp.zeros((), BF16)

    def scores(q, segs, p):
        sl = slice(p * LANES, (p + 1) * LANES)
        qp = q(sl)
        q2 = jnp.concatenate([jnp.where(first, qp, zero), jnp.where(first, zero, qp)], axis=0)
        ss = []
        for k, _, bias, pen in segs:
            s = _dot_nt(q2, k(sl)) + bias(p)
            ss.append(s if pen is None else s + pen)
        return ss

    def finish(ss, segs, put, p):
        sl = slice(p * LANES, (p + 1) * LANES)
        r = ss[0].shape[0] // 2
        o2 = _softmax_pv(ss, [v(sl) for _, v, _, _ in segs])
        put(sl, jnp.where(first, o2[:r], o2[r:]))

    pending = None
    for q, segs, put in jobs:
        for p in range(NH_A // 2):
            ss = scores(q, segs, p)
            if pending is not None:
                finish(*pending)
            pending = (ss, segs, put, p)
    finish(*pending)


def _attn_prompt_kernel(q_ref, kp_ref, ko_ref, vp_ref, vo_ref, ext_ref, o_ref, bias_ref):
    tq = q_ref.shape[1]
    j = pl.program_id(1)

    @pl.when((pl.program_id(0) == 0) & (j == 0))
    def _():
        def put_bias(h, tile):
            bias_ref[h // 2, (h % 2) * ATT_SUB:(h % 2 + 1) * ATT_SUB, :] = tile
        _build_band_bias(ext_ref, ATT_SUB, tq + ATT_SUB, put_bias)

    def run(pen_prev):
        jobs = []
        for i in range(tq // ATT_SUB):
            lo = i * ATT_SUB
            n_prev = tq - lo
            bias = lambda a, b: (lambda p: bias_ref[p, :, a:b])
            rows = lambda ref, a, b: (lambda sl: ref[0, a:b, sl])

            def put(sl, val, lo=lo):
                o_ref[0, lo:lo + ATT_SUB, sl] = val.astype(o_ref.dtype)

            segs = [(rows(kp_ref, lo, tq), rows(vp_ref, lo, tq), bias(0, n_prev), pen_prev),
                    (rows(ko_ref, 0, lo + ATT_SUB), rows(vo_ref, 0, lo + ATT_SUB),
                     bias(n_prev, tq + ATT_SUB), None)]
            jobs.append((rows(q_ref, lo, lo + ATT_SUB), segs, put))
        _attn_pairs(jobs)

    @pl.when(j >= 1)
    def _():
        run(None)

    @pl.when(j < 1)
    def _():
        run(jnp.full((), NEG, F32))


def _attn_prompt(aq, ak, av, ext):
    NB, L, _ = aq.shape
    tq = ATT_WINDOW
    assert L % tq == 0
    grid = (NB, L // tq)
    row = pl.BlockSpec((1, tq, D_A), lambda b, j: (b, j, 0))
    prev = pl.BlockSpec((1, tq, D_A), lambda b, j: (b, jnp.maximum(j - 1, 0), 0))
    return pl.pallas_call(
        _attn_prompt_kernel,
        out_shape=jax.ShapeDtypeStruct((NB, L, D_A), BF16),
        grid=grid,
        in_specs=[row, prev, row, prev, row, _const_spec(ext.shape)],
        out_specs=row,
        scratch_shapes=[pltpu.VMEM((NH_A // 2, 2 * ATT_SUB, tq + ATT_SUB), F32)],
        compiler_params=_cparams(2),
        name="attn_prompt",
    )(aq, ak, ak, av, av, ext)


def _attn_sample_kernel(q_ref, k_ref, v_ref, ckt_ref, cvt_ref, ext_ref, o_ref, bias_ref, *, nb):
    tq = q_ref.shape[1]
    nc = ckt_ref.shape[-1]

    @pl.when(pl.program_id(0) == 0)
    def _():
        def put_bias(h, tile):
            bias_ref[h] = tile
        _build_band_bias(ext_ref, tq, nc + tq, put_bias)

    def scores(b, h):
        hs = slice(h * DH_A, (h + 1) * DH_A)
        qh = q_ref[b, :, hs]
        return [_dot(qh, ckt_ref[b, h].astype(BF16)) + bias_ref[h, :, 0:nc],
                _dot_nt(qh, k_ref[b, :, hs]) + bias_ref[h, :, nc:nc + tq]]

    def finish(b, h, ss):
        hs = slice(h * DH_A, (h + 1) * DH_A)
        mx = jnp.maximum(*[jnp.max(s, axis=-1, keepdims=True) for s in ss])
        e_c, e_o = [jnp.exp2(s - mx) for s in ss]
        l = jnp.sum(e_c, axis=-1, keepdims=True) + jnp.sum(e_o, axis=-1, keepdims=True)
        o = (_dot_nt(e_c.astype(BF16), cvt_ref[b, h].astype(BF16))
             + _dot(e_o.astype(BF16), v_ref[b, :, hs]))
        return o / l

    outs = {b: [] for b in range(nb)}
    pending = []

    def retire():
        b, h, ss = pending.pop(0)
        outs[b].append(finish(b, h, ss))
        if h == NH_A - 1:
            o_ref[b] = jnp.concatenate(outs[b], axis=-1).astype(o_ref.dtype)

    for b in range(nb):
        for h in range(NH_A):
            pending.append((b, h, scores(b, h)))
            if len(pending) > SAMPLE_ATT_LOOKAHEAD:
                retire()
    while pending:
        retire()


def _attn_sample(aq, ak, av, cache_kt, cache_vt, ext, *, nb):
    NB, L, _ = aq.shape
    nc = cache_kt.shape[-1]
    row = pl.BlockSpec((nb, L, D_A), lambda i: (i, 0, 0))
    crow = pl.BlockSpec((nb, NH_A, DH_A, nc), lambda i: (i, 0, 0, 0))
    return pl.pallas_call(
        functools.partial(_attn_sample_kernel, nb=nb),
        out_shape=jax.ShapeDtypeStruct((NB, L, D_A), BF16),
        grid=(NB // nb,),
        in_specs=[row, row, row, crow, crow, _const_spec(ext.shape)],
        out_specs=row,
        scratch_shapes=[pltpu.VMEM((NH_A, L, nc + L), F32)],
        compiler_params=_cparams(1),
        name="attn_sample",
    )(aq, ak, av, cache_kt, cache_vt, ext)


def _rel_ext(rel_table, tq, klen):
    p_len = -(-(klen + tq) // LANES) * LANES
    n_lo = tq - CHUNK
    n_hi = p_len - tq - MAX_REL
    rep = lambda col, n: jnp.broadcast_to(col, (NH_A, n))
    by_dist = jnp.concatenate([rep(rel_table[:, :1], n_lo), rel_table, rep(rel_table[:, -1:], n_hi)],
                              axis=1)
    rev = by_dist[:, ::-1]
    n = p_len - klen
    return jnp.concatenate([rev[:, n:], rev[:, :n]], axis=1).astype(F32)


def _gate_lanes(g):
    z = lambda n: jnp.zeros((g.shape[0], n), g.dtype)
    return jnp.concatenate([g[:, :NH_M], z(8 - NH_M), g[:, NH_M:], z(LANES - 8 - NH_M)], axis=1)


def _layer(x, mod, conv_prev, state, cache, w, *, nb, tl, nbm, lc, final):
    NB, L, d = x.shape
    x1 = _ffn(x, mod, w["g0"], w["up1"], w["dn1"], w["g_final"], nb=nb, tl=tl, sub=0)
    (q, k, v, mo, gates, aq, ak, av, k_tail, v_tail, conv_new) = _proj(
        x1, mod, w["g1"], w["w_main"], w["w_gates"], w["b_gates"], w["conv_w"], w["conv_b"],
        conv_prev, nb=nb, tl=tl)
    c0, n0, m0 = state
    n0 = jnp.pad(n0, ((0, 0), (0, 8 - NH_M), (0, 0)))
    m0 = jnp.broadcast_to(jnp.pad(m0, ((0, 0), (0, 8 - NH_M)))[:, :, None], (NB, 8, LANES))
    hm, c_new, n_new, m_new = _mlstm(q, k, v, gates, mo, c0, n0, m0, w["g_mlstm"], nbm=nbm, lc=lc)
    if cache is None:
        att = _attn_prompt(aq, ak, av, w["ext_prompt"])
    else:
        att = _attn_sample(aq, ak, av, cache[0], cache[1], w["ext_sample"], nb=min(NB, 4))
    y = _ffn(x1, mod, w["g2"], w["up2"], w["dn2"], w["g_final"], nb=nb, tl=tl, sub=2,
             mix=(hm, att, w["w_out"]), final=final)
    n_keep = k_tail.shape[1]
    states = (c_new, n_new[:, :NH_M, :], m_new[:, :NH_M, 0], conv_new,
              k_tail.reshape(NB, n_keep, NH_A, DH_A), v_tail.reshape(NB, n_keep, NH_A, DH_A))
    return y, states


def kernel(x_prompt, x_sample, state_mlstm_C, state_mlstm_n, state_mlstm_m, state_conv, cache_att_k, cache_att_v, c_prompt, c_sample, w_ada, b_ada, g_norm, w_ffn1_up, w_ffn1_down, w_ffn2_up, w_ffn2_down, w_in, conv_w, conv_b, b_gates, g_mlstm, rel_bias_table, w_out, g_final):
    depth = w_ada.shape[0]
    bp, seq, d = x_prompt.shape
    bs, dseq, _ = x_sample.shape
    xp, xs = x_prompt, x_sample
    st_p, st_s = [], []
    n_c = cache_att_k.shape[2]
    tl_p = min(ATT_WINDOW, seq)
    lc_p = min(MLSTM_CHUNK, seq)
    nb_s = max(1, min(bs, ATT_WINDOW // dseq))
    for l in range(depth):
        rows = bp + bs
        pad = (-rows) % 8
        c_all = jnp.concatenate([c_prompt, c_sample, jnp.zeros((pad, d), F32)], axis=0)
        mod = _adaln(c_all, w_ada[l], b_ada[l])
        mod_p = mod[:bp].reshape(bp, N_MOD, d)
        mod_s = mod[bp:rows].reshape(bs, N_MOD, d)
        off_g = 4 * D_M
        off_a = off_g + 2 * NH_M
        wl = w_in[l]
        rel = rel_bias_table[l]
        w = {
            "g0": g_norm[l, 0:1], "g1": g_norm[l, 1:2], "g2": g_norm[l, 2:3],
            "g_final": g_final.reshape(1, d),
            "up1": w_ffn1_up[l].astype(BF16), "dn1": w_ffn1_down[l].astype(BF16),
            "up2": w_ffn2_up[l].astype(BF16), "dn2": w_ffn2_down[l].astype(BF16),
            "w_main": jnp.concatenate([wl[:, :off_g], wl[:, off_a:]], axis=1).astype(BF16),
            "w_gates": _gate_lanes(wl[:, off_g:off_a]).astype(BF16),
            "b_gates": _gate_lanes(b_gates[l].reshape(1, 2 * NH_M)),
            "conv_w": conv_w[l], "conv_b": conv_b[l].reshape(1, 2 * D_M),
            "g_mlstm": g_mlstm[l].reshape(1, D_M),
            "w_out": w_out[l].astype(BF16),
            "ext_prompt": _rel_ext(rel, ATT_SUB, ATT_WINDOW + ATT_SUB),
            "ext_sample": _rel_ext(rel, dseq, n_c + dseq),
        }
        zero_state = (jnp.zeros((bp, NH_M, DH_M, DH_M), F32), jnp.zeros((bp, NH_M, DH_M), F32),
                      jnp.zeros((bp, NH_M), F32))
        xp, sp = _layer(xp, mod_p, jnp.zeros((bp, CONV_W - 1, 2 * D_M), F32), zero_state, None, w,
                        nb=1, tl=tl_p, nbm=bp, lc=lc_p, final=l == depth - 1)
        state_s = (state_mlstm_C[l], state_mlstm_n[l], state_mlstm_m[l])
        cache = (jnp.transpose(cache_att_k[l], (0, 2, 3, 1)), jnp.transpose(cache_att_v[l], (0, 2, 3, 1)))
        xs, ss = _layer(xs, mod_s, state_conv[l], state_s, cache, w, nb=nb_s, tl=dseq,
                        nbm=min(bs, 4), lc=dseq, final=l == depth - 1)
        st_p.append(sp)
        st_s.append(ss)
    stk = lambda sts, i: jnp.stack([s[i] for s in sts])
    return ((xp, xs) + tuple(stk(st_p, i) for i in range(6)) + tuple(stk(st_s, i) for i in range(6)))
```

```python
import functools

import jax
import jax.numpy as jnp
from jax import lax
from jax.experimental import pallas as pl
from jax.experimental.pallas import tpu as pltpu

F32 = jnp.float32
BF16 = jnp.bfloat16

CHUNK = 64
NH_M = 4
DH_M = 128
D_M = NH_M * DH_M
NH_A = 8
DH_A = 64
D_A = NH_A * DH_A
CONV_W = 4
LEFT_CHUNKS = 8
ATT_WINDOW = LEFT_CHUNKS * CHUNK
MAX_REL = 2 * CHUNK
N_MOD = 9
EPS = 1e-6
NEG = -1e30
LOG2E = 1.4426950408889634

LANES = 128
SUBLANES = 8
V7X_VMEM_BYTES = 64 * 1024 * 1024
VMEM_LIMIT = V7X_VMEM_BYTES - 8 * 1024 * 1024

FF_CHUNK = 256
MLSTM_CHUNK = 256
MLSTM_CHUNKS_PER_STEP = 2
ATT_SUB = 2 * CHUNK
SAMPLE_ATT_LOOKAHEAD = 3
PAIR_ATT_LOOKAHEAD = 1


def _cparams(n_axes):
    return pltpu.CompilerParams(dimension_semantics=("arbitrary",) * n_axes,
                                vmem_limit_bytes=VMEM_LIMIT)


def _const_spec(shape):
    nd = len(shape)
    return pl.BlockSpec(shape, lambda *_: (0,) * nd, pipeline_mode=pl.Buffered(1))


def _dot(a, b):
    return jnp.dot(a, b, preferred_element_type=F32)


def _dot_nt(a, b):
    return lax.dot_general(a, b, (((1,), (1,)), ((), ())), preferred_element_type=F32)


def _sigmoid(x):
    return 1.0 / (1.0 + jnp.exp(-x))


def _log_sigmoid(x):
    return jnp.minimum(x, 0.0) - jnp.log(1.0 + jnp.exp(-jnp.abs(x)))


def _norm_mod(x, g, shift, scale):
    ms = jnp.mean(x * x, axis=-1, keepdims=True)
    return x * lax.rsqrt(ms + EPS) * (g * (1.0 + scale)) + shift


def _adaln_kernel(c_ref, w_ref, b_ref, o_ref):
    c = c_ref[...]
    o_ref[...] = _dot(c * _sigmoid(c), w_ref[...]) + b_ref[...]


def _adaln(c_all, w_ada, b_ada):
    rows, d = c_all.shape
    n = w_ada.shape[1]
    tn = d
    return pl.pallas_call(
        _adaln_kernel,
        out_shape=jax.ShapeDtypeStruct((rows, n), F32),
        grid=(n // tn,),
        in_specs=[pl.BlockSpec((rows, d), lambda j: (0, 0)),
                  pl.BlockSpec((d, tn), lambda j: (0, j)),
                  pl.BlockSpec((1, tn), lambda j: (0, j))],
        out_specs=pl.BlockSpec((rows, tn), lambda j: (0, j)),
        compiler_params=_cparams(1),
        name="adaln",
    )(c_all, w_ada, b_ada.reshape(1, n))


def _ffn_kernel(*refs, nb, tl, sub, with_mix, with_final):
    if with_mix:
        x_ref, hm_ref, att_ref, mod_ref, gn_ref, wout_ref, wup_ref, wdn_ref, gfin_ref, o_ref, hb_ref, act_ref = refs
    else:
        x_ref, mod_ref, gn_ref, wup_ref, wdn_ref, gfin_ref, o_ref, hb_ref, act_ref = refs
    m = nb * tl
    d = x_ref.shape[-1]
    dff = wdn_ref.shape[0]
    mod = mod_ref[...]
    shift, scale, gate = (mod[:, 3 * sub + i:3 * sub + i + 1, :] for i in range(3))

    if with_mix:
        mix = _dot(hm_ref[...].reshape(m, D_M), wout_ref[0:D_M, :])
        mix = mix + _dot(att_ref[...].reshape(m, D_A), wout_ref[D_M:, :])
        o_ref[...] = x_ref[...] + mod[:, 5:6, :] * mix.reshape(nb, tl, d)
    else:
        o_ref[...] = x_ref[...]

    h = _norm_mod(o_ref[...], gn_ref[...], shift, scale)
    hb_ref[...] = h.reshape(m, d).astype(BF16)
    for c in range(dff // FF_CHUNK):
        lo = c * FF_CHUNK
        g = _dot(hb_ref[...], wup_ref[:, lo:lo + FF_CHUNK])
        u = _dot(hb_ref[...], wup_ref[:, dff + lo:dff + lo + FF_CHUNK])
        act_ref[:, lo:lo + FF_CHUNK] = (g * _sigmoid(g) * u).astype(BF16)
    dn = _dot(act_ref[...], wdn_ref[...])
    y = o_ref[...] + 0.5 * gate * dn.reshape(nb, tl, d)
    if with_final:
        ms = jnp.mean(y * y, axis=-1, keepdims=True)
        y = y * lax.rsqrt(ms + EPS) * gfin_ref[...]
    o_ref[...] = y


def _ffn(x, mod, g_norm_row, w_up, w_dn, g_final, *, nb, tl, sub, mix=None, final=False):
    NB, L, d = x.shape
    dff = w_dn.shape[0]
    m = nb * tl
    grid = (NB // nb, L // tl)
    row = lambda w: pl.BlockSpec((nb, tl, w), lambda i, j: (i, j, 0))
    in_specs = [row(d)]
    args = [x]
    if mix is not None:
        hm, att, w_out = mix
        in_specs += [row(D_M), row(D_A)]
        args += [hm, att]
    in_specs += [pl.BlockSpec((nb, N_MOD, d), lambda i, j: (i, 0, 0)), _const_spec((1, d))]
    args += [mod, g_norm_row]
    if mix is not None:
        in_specs.append(_const_spec(w_out.shape))
        args.append(w_out)
    in_specs += [_const_spec(w_up.shape), _const_spec(w_dn.shape), _const_spec((1, d))]
    args += [w_up, w_dn, g_final]
    kern = functools.partial(_ffn_kernel, nb=nb, tl=tl, sub=sub, with_mix=mix is not None,
                             with_final=final)
    return pl.pallas_call(
        kern,
        out_shape=jax.ShapeDtypeStruct((NB, L, d), F32),
        grid=grid,
        in_specs=in_specs,
        out_specs=row(d),
        scratch_shapes=[pltpu.VMEM((m, d), BF16), pltpu.VMEM((m, dff), BF16)],
        compiler_params=_cparams(2),
        name="ffn%d" % sub,
    )(*args)


def _proj_kernel(x_ref, mod_ref, gn_ref, w_ref, wg_ref, bg_ref, cw_ref, cb_ref, cprev_ref,
                 q_ref, k_ref, v_ref, o_ref, gates_ref, aq_ref, ak_ref, av_ref,
                 kt_ref, vt_ref, cnew_ref, hb_ref, carry_ref, *, nb, tl):
    m = nb * tl
    d = x_ref.shape[-1]
    j = pl.program_id(1)
    mod = mod_ref[...]
    h = _norm_mod(x_ref[...], gn_ref[...], mod[:, 3:4, :], mod[:, 4:5, :])
    hb_ref[...] = h.reshape(m, d).astype(BF16)

    @pl.when(j == 0)
    def _():
        carry_ref[...] = jnp.zeros(carry_ref.shape, F32)
        carry_ref[:, SUBLANES - (CONV_W - 1):SUBLANES, :] = cprev_ref[...]

    def qk_pre(c):
        lo = c * FF_CHUNK
        return _dot(hb_ref[...], w_ref[:, lo:lo + FF_CHUNK]).reshape(nb, tl, FF_CHUNK)

    def conv(c, pre):
        lo = c * FF_CHUNK
        ext = jnp.concatenate([carry_ref[:, :, lo:lo + FF_CHUNK], pre], axis=1)
        acc = cb_ref[:, lo:lo + FF_CHUNK]
        for t in reversed(range(CONV_W)):
            s0 = SUBLANES - (CONV_W - 1) + t
            acc = acc + cw_ref[t:t + 1, lo:lo + FF_CHUNK] * ext[:, s0:s0 + tl, :]
        y = acc * _sigmoid(acc)
        if lo < D_M:
            q_ref[:, :, lo:lo + FF_CHUNK] = y.astype(BF16)
        else:
            k_ref[:, :, lo - D_M:lo - D_M + FF_CHUNK] = (y * DH_M ** -0.5).astype(BF16)
        cnew_ref[:, :, lo:lo + FF_CHUNK] = pre[:, tl - (CONV_W - 1):tl, :]
        carry_ref[:, :, lo:lo + FF_CHUNK] = pre[:, tl - SUBLANES:tl, :]

    def cols(c0):
        return _dot(hb_ref[...], w_ref[:, c0:c0 + D_M]).reshape(nb, tl, D_M)

    pre = qk_pre(0)
    v_ref[...] = cols(2 * D_M).astype(BF16)
    conv(0, pre)
    pre = qk_pre(1)
    o_ref[...] = cols(3 * D_M)
    conv(1, pre)
    pre = qk_pre(2)
    aq_ref[...] = (cols(4 * D_M) * (DH_A ** -0.5 * LOG2E)).astype(BF16)
    conv(2, pre)
    pre = qk_pre(3)
    ak = cols(4 * D_M + D_A)
    ak_ref[...] = ak.astype(BF16)
    kt_ref[...] = ak
    conv(3, pre)
    av = cols(4 * D_M + 2 * D_A)
    av_ref[...] = av.astype(BF16)
    vt_ref[...] = av
    gates_ref[...] = (_dot(hb_ref[...], wg_ref[...]) + bg_ref[...]).reshape(nb, tl, LANES)


def _proj(x, mod, g_norm_row, w_main, w_gates, b_gates, conv_w, conv_b, conv_prev, *, nb, tl):
    NB, L, d = x.shape
    assert tl == min(ATT_WINDOW, L)
    m = nb * tl
    grid = (NB // nb, L // tl)
    row = lambda w: pl.BlockSpec((nb, tl, w), lambda i, j: (i, j, 0))
    per_b = lambda r, w: pl.BlockSpec((nb, r, w), lambda i, j: (i, 0, 0))
    bshape = lambda w, dt: jax.ShapeDtypeStruct((NB, L, w), dt)
    in_specs = [row(d), per_b(N_MOD, d), _const_spec((1, d)), _const_spec(w_main.shape),
                _const_spec(w_gates.shape), _const_spec((1, LANES)), _const_spec(conv_w.shape),
                _const_spec((1, 2 * D_M)), per_b(CONV_W - 1, 2 * D_M)]
    out_shape = [bshape(D_M, BF16), bshape(D_M, BF16), bshape(D_M, BF16), bshape(D_M, F32),
                 bshape(LANES, F32), bshape(D_A, BF16), bshape(D_A, BF16), bshape(D_A, BF16),
                 jax.ShapeDtypeStruct((NB, tl, D_A), F32), jax.ShapeDtypeStruct((NB, tl, D_A), F32),
                 jax.ShapeDtypeStruct((NB, CONV_W - 1, 2 * D_M), F32)]
    out_specs = [row(D_M), row(D_M), row(D_M), row(D_M), row(LANES), row(D_A), row(D_A), row(D_A),
                 per_b(tl, D_A), per_b(tl, D_A), per_b(CONV_W - 1, 2 * D_M)]
    return pl.pallas_call(
        functools.partial(_proj_kernel, nb=nb, tl=tl),
        out_shape=out_shape,
        grid=grid,
        in_specs=in_specs,
        out_specs=out_specs,
        scratch_shapes=[pltpu.VMEM((m, d), BF16), pltpu.VMEM((nb, SUBLANES, 2 * D_M), F32)],
        compiler_params=_cparams(2),
        name="proj",
    )(x, mod, g_norm_row, w_main, w_gates, b_gates, conv_w, conv_b, conv_prev)


def _split_bf16(x):
    hi = x.astype(BF16)
    r = x - hi.astype(F32)
    mid = r.astype(BF16)
    return hi, mid, (r - mid.astype(F32)).astype(BF16)


def _mlstm_kernel(q_ref, k_ref, v_ref, gates_ref, mo_ref, c0_ref, n0_ref, m0_ref, gm_ref,
                  hm_ref, c_ref, n_ref, m_ref, ct_ref, *, nbm, lc, nck, single_chunk):
    j = pl.program_id(1)

    def transpose_states(src_ref, dst_ref):
        def one(i, carry):
            dst_ref[i // NH_M, i % NH_M] = jnp.transpose(src_ref[i // NH_M, i % NH_M])
            return carry
        lax.fori_loop(0, nbm * NH_M, one, 0)

    if single_chunk:
        n_ref[...] = n0_ref[...]
        m_ref[...] = m0_ref[...]
        ct_get = lambda b, h: jnp.transpose(c0_ref[b, h])

        def ct_put(b, h, val):
            c_ref[b, h] = jnp.transpose(val)
    else:
        @pl.when(j == 0)
        def _():
            transpose_states(c0_ref, ct_ref)
            n_ref[...] = n0_ref[...]
            m_ref[...] = m0_ref[...]
        ct_get = lambda b, h: ct_ref[b, h]

        def ct_put(b, h, val):
            ct_ref[b, h] = val

    s_i = lax.broadcasted_iota(jnp.int32, (lc, lc), 0)
    t_i = lax.broadcasted_iota(jnp.int32, (lc, lc), 1)
    causal = s_i <= t_i
    upper = jnp.where(causal, 1.0, 0.0).astype(BF16)
    head_row = lax.broadcasted_iota(jnp.int32, (SUBLANES, 1), 0)

    pairs = [(b, h) for b in range(nbm) for h in range(NH_M)]
    units = [(ck, b) for ck in range(nck) for b in range(nbm)]
    hsl = lambda h: slice(h * DH_M, (h + 1) * DH_M)
    tok = lambda ck: slice(ck * lc, (ck + 1) * lc)

    gate = {}
    for ck, b in units:
        gates_t = jnp.transpose(gates_ref[b, tok(ck)])
        ig_t = gates_t[0:SUBLANES]
        lf_t = _log_sigmoid(gates_t[SUBLANES:2 * SUBLANES])
        b_t = sum(_dot(part, upper) for part in _split_bf16(lf_t))
        c_t = ig_t - b_t
        c2_t = c_t * LOG2E
        c2_cols = jnp.transpose(jnp.concatenate([c2_t, jnp.zeros((LANES - SUBLANES, lc), F32)], axis=0))
        cmax2_t = jnp.concatenate(
            [jnp.max(jnp.where(causal, c2_cols[:, h:h + 1], NEG), axis=0, keepdims=True)
             for h in range(NH_M)] + [jnp.zeros((SUBLANES - NH_M, lc), F32)], axis=0)
        gate[ck, b] = dict(b_t=b_t, c_t=c_t, c2_cols=c2_cols, cmax2=cmax2_t,
                           m_intra=b_t + cmax2_t * (1.0 / LOG2E))

    early = {}
    for ck, b in units:
        for h in range(NH_M):
            early[ck, b, h] = dict(
                s_t=_dot_nt(k_ref[b, tok(ck), hsl(h)], q_ref[b, tok(ck), hsl(h)]),
                v_t=jnp.transpose(v_ref[b, tok(ck), hsl(h)].astype(F32)))

    for ck in range(nck):
        rows = []
        for b in range(nbm):
            g = gate[ck, b]
            b_t, c_t = g["b_t"], g["c_t"]
            b_end = b_t[:, lc - 1:lc]
            m_prev = m_ref[b][:, 0:1]
            log_g = b_end + c_t
            m_end = jnp.maximum(b_end + m_prev, jnp.max(log_g, axis=-1, keepdims=True))
            g_rows = jnp.exp(log_g - m_end)
            log_inter = b_t + m_prev
            m_tok = jnp.maximum(log_inter, g["m_intra"])
            rows.append(dict(decay=jnp.exp(b_end + m_prev - m_end), g=g_rows,
                             g_b=g_rows.astype(BF16), m_end=m_end,
                             r_intra=jnp.exp(g["m_intra"] - m_tok),
                             w_inter=jnp.exp(log_inter - m_tok), floor=jnp.exp(-m_tok),
                             n_prev_b=n_ref[b].astype(BF16)))

        for b, h in pairs:
            q = q_ref[b, tok(ck), hsl(h)]
            ct_prev = ct_get(b, h)
            early[ck, b, h].update(
                ct=ct_prev if single_chunk else None,
                cq_t=_dot_nt(ct_prev.astype(BF16), q),
                nq=_dot_nt(rows[b]["n_prev_b"], q)[h:h + 1, :])

        n_upd = [jnp.zeros((SUBLANES, DH_M), F32) for _ in range(nbm)]
        for b, h in pairs:
            r, g, e = rows[b], gate[ck, b], early[ck, b, h]
            k = k_ref[b, tok(ck), hsl(h)]
            d_t = jnp.exp2(g["c2_cols"][:, h:h + 1] - g["cmax2"][h:h + 1, :])
            a_t = jnp.where(causal, d_t, 0.0) * e["s_t"]
            r_intra, w_inter = r["r_intra"][h:h + 1, :], r["w_inter"][h:h + 1, :]
            num_t = _dot(e["v_t"].astype(BF16), a_t.astype(BF16)) * r_intra + e["cq_t"] * w_inter
            den = jnp.sum(a_t, axis=0, keepdims=True) * r_intra + e["nq"] * w_inter
            h_t = num_t * (1.0 / jnp.maximum(jnp.abs(den), r["floor"][h:h + 1, :]))
            hn_t = h_t * lax.rsqrt(jnp.mean(h_t * h_t, axis=0, keepdims=True) + EPS)
            hn = jnp.transpose(hn_t) * gm_ref[:, hsl(h)]
            hm_ref[b, tok(ck), hsl(h)] = (_sigmoid(mo_ref[b, tok(ck), hsl(h)]) * hn).astype(BF16)

            vg_t = (e["v_t"] * r["g"][h:h + 1, :]).astype(BF16)
            ct_prev = e["ct"] if single_chunk else ct_get(b, h)
            ct_put(b, h, r["decay"][h:h + 1, :] * ct_prev + _dot(vg_t, k))
            n_upd[b] = n_upd[b] + jnp.where(head_row == h, _dot(r["g_b"], k), 0.0)

        for b in range(nbm):
            n_ref[b] = rows[b]["decay"] * n_ref[b] + n_upd[b]
            m_ref[b] = jnp.broadcast_to(rows[b]["m_end"], (SUBLANES, LANES))

    if not single_chunk:
        @pl.when(j == pl.num_programs(1) - 1)
        def _():
            transpose_states(ct_ref, c_ref)


def _mlstm(q, k, v, gates, mo, c0, n0, m0, g_mlstm_row, *, nbm, lc):
    NB, L, _ = q.shape
    nck = MLSTM_CHUNKS_PER_STEP if L % (MLSTM_CHUNKS_PER_STEP * lc) == 0 else 1
    grid = (NB // nbm, L // (nck * lc))
    row = lambda w: pl.BlockSpec((nbm, nck * lc, w), lambda i, j: (i, j, 0))
    st = lambda s: pl.BlockSpec((nbm,) + s, lambda i, j: (i,) + (0,) * len(s))
    c_s, n_s = (NH_M, DH_M, DH_M), (SUBLANES, DH_M)
    return pl.pallas_call(
        functools.partial(_mlstm_kernel, nbm=nbm, lc=lc, nck=nck, single_chunk=L == lc),
        out_shape=[jax.ShapeDtypeStruct((NB, L, D_M), BF16),
                   jax.ShapeDtypeStruct((NB,) + c_s, F32),
                   jax.ShapeDtypeStruct((NB,) + n_s, F32),
                   jax.ShapeDtypeStruct((NB,) + n_s, F32)],
        grid=grid,
        in_specs=[row(D_M), row(D_M), row(D_M), row(LANES), row(D_M), st(c_s), st(n_s), st(n_s),
                  _const_spec((1, D_M))],
        out_specs=[row(D_M), st(c_s), st(n_s), st(n_s)],
        scratch_shapes=[pltpu.VMEM((nbm,) + c_s, F32)],
        compiler_params=_cparams(2),
        name="mlstm",
    )(q, k, v, gates, mo, c0, n0, m0, g_mlstm_row)


def _build_band_bias(ext_ref, tq, klen, put):
    p_len = ext_ref.shape[-1]
    q_pos = lax.broadcasted_iota(jnp.int32, (tq, klen), 0) + (klen - tq)
    k_pos = lax.broadcasted_iota(jnp.int32, (tq, klen), 1)
    back = q_pos // CHUNK - k_pos // CHUNK
    for h in range(NH_A):
        base = jnp.broadcast_to(ext_ref[h:h + 1, :] * LOG2E, (tq, p_len))
        toeplitz = pltpu.roll(base, 0, 1, stride=1, stride_axis=0)[:, :klen]
        put(h, jnp.where(back >= 0, jnp.where(back <= LEFT_CHUNKS, toeplitz, NEG), NEG))


def _lane_tiles(s):
    n = s.shape[-1]
    if n % LANES:
        return [s]
    return [s[:, i * LANES:(i + 1) * LANES] for i in range(n // LANES)]


def _row_reduce(tiles, op, lane_op):
    by_width = {}
    for t in tiles:
        by_width[t.shape[-1]] = t if t.shape[-1] not in by_width else op(by_width[t.shape[-1]], t)
    return functools.reduce(op, [lane_op(t, axis=-1, keepdims=True) for t in by_width.values()])


def _softmax_pv(ss, vs):
    mx = _row_reduce([t for s in ss for t in _lane_tiles(s)], jnp.maximum, jnp.max)
    es = [jnp.exp2(s - mx) for s in ss]
    l = _row_reduce([t for e in es for t in _lane_tiles(e)], jnp.add, jnp.sum)
    o = functools.reduce(jnp.add, [_dot(e.astype(BF16), v) for e, v in zip(es, vs)])
    return o / l


def _attn_pairs(jobs):
    first = lax.broadcasted_iota(jnp.int32, (1, LANES), 1) < DH_A
    zero = jnp.zeros((), BF16)

    def scores(q, segs, p):
        sl = slice(p * LANES, (p + 1) * LANES)
        qp = q(sl)
        q2 = jnp.concatenate([jnp.where(first, qp, zero), jnp.where(first, zero, qp)], axis=0)
        ss = []
        for k, _, bias, pen in segs:
            s = _dot_nt(q2, k(sl)) + bias(p)
            ss.append(s if pen is None else s + pen)
        return ss

    def finish(ss, segs, put, p):
        sl = slice(p * LANES, (p + 1) * LANES)
        r = ss[0].shape[0] // 2
        o2 = _softmax_pv(ss, [v(sl) for _, v, _, _ in segs])
        put(sl, jnp.where(first, o2[:r], o2[r:]))

    pending = []
    for q, segs, put in jobs:
        for p in range(NH_A // 2):
            pending.append((scores(q, segs, p), segs, put, p))
            if len(pending) > PAIR_ATT_LOOKAHEAD:
                finish(*pending.pop(0))
    while pending:
        finish(*pending.pop(0))


def _attn_prompt_kernel(q_ref, kp_ref, ko_ref, vp_ref, vo_ref, ext_ref, o_ref, bias_ref):
    tq = q_ref.shape[1]
    j = pl.program_id(1)

    @pl.when((pl.program_id(0) == 0) & (j == 0))
    def _():
        def put_bias(h, tile):
            bias_ref[h // 2, (h % 2) * ATT_SUB:(h % 2 + 1) * ATT_SUB, :] = tile
        _build_band_bias(ext_ref, ATT_SUB, tq + ATT_SUB, put_bias)

    def run(pen_prev):
        jobs = []
        for i in range(tq // ATT_SUB):
            lo = i * ATT_SUB
            n_prev = tq - lo
            bias = lambda a, b: (lambda p: bias_ref[p, :, a:b])
            rows = lambda ref, a, b: (lambda sl: ref[0, a:b, sl])

            def put(sl, val, lo=lo):
                o_ref[0, lo:lo + ATT_SUB, sl] = val.astype(o_ref.dtype)

            segs = [(rows(kp_ref, lo, tq), rows(vp_ref, lo, tq), bias(0, n_prev), pen_prev),
                    (rows(ko_ref, 0, lo + ATT_SUB), rows(vo_ref, 0, lo + ATT_SUB),
                     bias(n_prev, tq + ATT_SUB), None)]
            jobs.append((rows(q_ref, lo, lo + ATT_SUB), segs, put))
        _attn_pairs(jobs)

    @pl.when(j >= 1)
    def _():
        run(None)

    @pl.when(j < 1)
    def _():
        run(jnp.full((), NEG, F32))


def _attn_prompt(aq, ak, av, ext):
    NB, L, _ = aq.shape
    tq = ATT_WINDOW
    assert L % tq == 0
    grid = (NB, L // tq)
    row = pl.BlockSpec((1, tq, D_A), lambda b, j: (b, j, 0))
    prev = pl.BlockSpec((1, tq, D_A), lambda b, j: (b, jnp.maximum(j - 1, 0), 0))
    return pl.pallas_call(
        _attn_prompt_kernel,
        out_shape=jax.ShapeDtypeStruct((NB, L, D_A), BF16),
        grid=grid,
        in_specs=[row, prev, row, prev, row, _const_spec(ext.shape)],
        out_specs=row,
        scratch_shapes=[pltpu.VMEM((NH_A // 2, 2 * ATT_SUB, tq + ATT_SUB), F32)],
        compiler_params=_cparams(2),
        name="attn_prompt",
    )(aq, ak, ak, av, av, ext)


def _attn_sample_kernel(q_ref, k_ref, v_ref, ckt_ref, cvt_ref, ext_ref, o_ref, bias_ref, *, nb):
    tq = q_ref.shape[1]
    nc = ckt_ref.shape[-1]

    @pl.when(pl.program_id(0) == 0)
    def _():
        def put_bias(h, tile):
            bias_ref[h] = tile
        _build_band_bias(ext_ref, tq, nc + tq, put_bias)

    def scores(b, h):
        hs = slice(h * DH_A, (h + 1) * DH_A)
        qh = q_ref[b, :, hs]
        return [_dot(qh, ckt_ref[b, h].astype(BF16)) + bias_ref[h, :, 0:nc],
                _dot_nt(qh, k_ref[b, :, hs]) + bias_ref[h, :, nc:nc + tq]]

    def finish(b, h, ss):
        hs = slice(h * DH_A, (h + 1) * DH_A)
        mx = jnp.maximum(*[jnp.max(s, axis=-1, keepdims=True) for s in ss])
        e_c, e_o = [jnp.exp2(s - mx) for s in ss]
        l = jnp.sum(e_c, axis=-1, keepdims=True) + jnp.sum(e_o, axis=-1, keepdims=True)
        o = (_dot_nt(e_c.astype(BF16), cvt_ref[b, h].astype(BF16))
             + _dot(e_o.astype(BF16), v_ref[b, :, hs]))
        return o / l

    outs = {b: [] for b in range(nb)}
    pending = []

    def retire():
        b, h, ss = pending.pop(0)
        outs[b].append(finish(b, h, ss))
        if h == NH_A - 1:
            o_ref[b] = jnp.concatenate(outs[b], axis=-1).astype(o_ref.dtype)

    for b in range(nb):
        for h in range(NH_A):
            pending.append((b, h, scores(b, h)))
            if len(pending) > SAMPLE_ATT_LOOKAHEAD:
                retire()
    while pending:
        retire()


def _attn_sample(aq, ak, av, cache_kt, cache_vt, ext, *, nb):
    NB, L, _ = aq.shape
    nc = cache_kt.shape[-1]
    row = pl.BlockSpec((nb, L, D_A), lambda i: (i, 0, 0))
    crow = pl.BlockSpec((nb, NH_A, DH_A, nc), lambda i: (i, 0, 0, 0))
    return pl.pallas_call(
        functools.partial(_attn_sample_kernel, nb=nb),
        out_shape=jax.ShapeDtypeStruct((NB, L, D_A), BF16),
        grid=(NB // nb,),
        in_specs=[row, row, row, crow, crow, _const_spec(ext.shape)],
        out_specs=row,
        scratch_shapes=[pltpu.VMEM((NH_A, L, nc + L), F32)],
        compiler_params=_cparams(1),
        name="attn_sample",
    )(aq, ak, av, cache_kt, cache_vt, ext)


def _rel_ext(rel_table, tq, klen):
    p_len = -(-(klen + tq) // LANES) * LANES
    n_lo = tq - CHUNK
    n_hi = p_len - tq - MAX_REL
    rep = lambda col, n: jnp.broadcast_to(col, (NH_A, n))
    by_dist = jnp.concatenate([rep(rel_table[:, :1], n_lo), rel_table, rep(rel_table[:, -1:], n_hi)],
                              axis=1)
    rev = by_dist[:, ::-1]
    n = p_len - klen
    return jnp.concatenate([rev[:, n:], rev[:, :n]], axis=1).astype(F32)


def _gate_lanes(g):
    z = lambda n: jnp.zeros((g.shape[0], n), g.dtype)
    return jnp.concatenate([g[:, :NH_M], z(SUBLANES - NH_M), g[:, NH_M:], z(LANES - SUBLANES - NH_M)], axis=1)


def _layer(x, mod, conv_prev, state, cache, w, *, nb, tl, nbm, lc, final):
    NB, L, d = x.shape
    x1 = _ffn(x, mod, w["g0"], w["up1"], w["dn1"], w["g_final"], nb=nb, tl=tl, sub=0)
    (q, k, v, mo, gates, aq, ak, av, k_tail, v_tail, conv_new) = _proj(
        x1, mod, w["g1"], w["w_main"], w["w_gates"], w["b_gates"], w["conv_w"], w["conv_b"],
        conv_prev, nb=nb, tl=tl)
    c0, n0, m0 = state
    n0 = jnp.pad(n0, ((0, 0), (0, SUBLANES - NH_M), (0, 0)))
    m0 = jnp.broadcast_to(jnp.pad(m0, ((0, 0), (0, SUBLANES - NH_M)))[:, :, None], (NB, SUBLANES, LANES))
    hm, c_new, n_new, m_new = _mlstm(q, k, v, gates, mo, c0, n0, m0, w["g_mlstm"], nbm=nbm, lc=lc)
    if cache is None:
        att = _attn_prompt(aq, ak, av, w["ext_prompt"])
    else:
        att = _attn_sample(aq, ak, av, cache[0], cache[1], w["ext_sample"], nb=min(NB, 4))
    y = _ffn(x1, mod, w["g2"], w["up2"], w["dn2"], w["g_final"], nb=nb, tl=tl, sub=2,
             mix=(hm, att, w["w_out"]), final=final)
    n_keep = k_tail.shape[1]
    states = (c_new, n_new[:, :NH_M, :], m_new[:, :NH_M, 0], conv_new,
              k_tail.reshape(NB, n_keep, NH_A, DH_A), v_tail.reshape(NB, n_keep, NH_A, DH_A))
    return y, states


def kernel(x_prompt, x_sample, state_mlstm_C, state_mlstm_n, state_mlstm_m, state_conv, cache_att_k, cache_att_v, c_prompt, c_sample, w_ada, b_ada, g_norm, w_ffn1_up, w_ffn1_down, w_ffn2_up, w_ffn2_down, w_in, conv_w, conv_b, b_gates, g_mlstm, rel_bias_table, w_out, g_final):
    depth = w_ada.shape[0]
    bp, seq, d = x_prompt.shape
    bs, dseq, _ = x_sample.shape
    xp, xs = x_prompt, x_sample
    st_p, st_s = [], []
    n_c = cache_att_k.shape[2]
    tl_p = min(ATT_WINDOW, seq)
    lc_p = min(MLSTM_CHUNK, seq)
    nb_s = max(1, min(bs, ATT_WINDOW // dseq))
    for l in range(depth):
        rows = bp + bs
        pad = (-rows) % SUBLANES
        c_all = jnp.concatenate([c_prompt, c_sample, jnp.zeros((pad, d), F32)], axis=0)
        mod = _adaln(c_all, w_ada[l], b_ada[l])
        mod_p = mod[:bp].reshape(bp, N_MOD, d)
        mod_s = mod[bp:rows].reshape(bs, N_MOD, d)
        off_g = 4 * D_M
        off_a = off_g + 2 * NH_M
        wl = w_in[l]
        rel = rel_bias_table[l]
        w = {
            "g0": g_norm[l, 0:1], "g1": g_norm[l, 1:2], "g2": g_norm[l, 2:3],
            "g_final": g_final.reshape(1, d),
            "up1": w_ffn1_up[l].astype(BF16), "dn1": w_ffn1_down[l].astype(BF16),
            "up2": w_ffn2_up[l].astype(BF16), "dn2": w_ffn2_down[l].astype(BF16),
            "w_main": jnp.concatenate([wl[:, :off_g], wl[:, off_a:]], axis=1).astype(BF16),
            "w_gates": _gate_lanes(wl[:, off_g:off_a]).astype(BF16),
            "b_gates": _gate_lanes(b_gates[l].reshape(1, 2 * NH_M)),
            "conv_w": conv_w[l], "conv_b": conv_b[l].reshape(1, 2 * D_M),
            "g_mlstm": g_mlstm[l].reshape(1, D_M),
            "w_out": w_out[l].astype(BF16),
            "ext_prompt": _rel_ext(rel, ATT_SUB, ATT_WINDOW + ATT_SUB),
            "ext_sample": _rel_ext(rel, dseq, n_c + dseq),
        }
        zero_state = (jnp.zeros((bp, NH_M, DH_M, DH_M), F32), jnp.zeros((bp, NH_M, DH_M), F32),
                      jnp.zeros((bp, NH_M), F32))
        xp, sp = _layer(xp, mod_p, jnp.zeros((bp, CONV_W - 1, 2 * D_M), F32), zero_state, None, w,
                        nb=1, tl=tl_p, nbm=bp, lc=lc_p, final=l == depth - 1)
        state_s = (state_mlstm_C[l], state_mlstm_n[l], state_mlstm_m[l])
        cache = (jnp.transpose(cache_att_k[l], (0, 2, 3, 1)), jnp.transpose(cache_att_v[l], (0, 2, 3, 1)))
        xs, ss = _layer(xs, mod_s, state_conv[l], state_s, cache, w, nb=nb_s, tl=dseq,
                        nbm=min(bs, 4), lc=dseq, final=l == depth - 1)
        st_p.append(sp)
        st_s.append(ss)
    stk = lambda sts, i: jnp.stack([s[i] for s in sts])
    return ((xp, xs) + tuple(stk(st_p, i) for i in range(6)) + tuple(stk(st_s, i) for i in range(6)))
```

```python
import functools

import jax
import jax.numpy as jnp
from jax import lax
from jax.experimental import pallas as pl
from jax.experimental.pallas import tpu as pltpu

F32 = jnp.float32
BF16 = jnp.bfloat16

CHUNK = 64
NH_M = 4
DH_M = 128
D_M = NH_M * DH_M
NH_A = 8
DH_A = 64
D_A = NH_A * DH_A
CONV_W = 4
LEFT_CHUNKS = 8
ATT_WINDOW = LEFT_CHUNKS * CHUNK
MAX_REL = 2 * CHUNK
N_MOD = 9
EPS = 1e-6
NEG = -1e30
LOG2E = 1.4426950408889634

LANES = 128
SUBLANES = 8
V7X_VMEM_BYTES = 64 * 1024 * 1024
VMEM_LIMIT = V7X_VMEM_BYTES - 8 * 1024 * 1024

FF_CHUNK = 256
MLSTM_CHUNK = 256
MLSTM_CHUNKS_PER_STEP = 2
ATT_SUB = 2 * CHUNK
SAMPLE_ATT_LOOKAHEAD = 3
PAIR_ATT_LOOKAHEAD = 1


def _cparams(n_axes):
    return pltpu.CompilerParams(dimension_semantics=("arbitrary",) * n_axes,
                                vmem_limit_bytes=VMEM_LIMIT)


def _const_spec(shape):
    nd = len(shape)
    return pl.BlockSpec(shape, lambda *_: (0,) * nd, pipeline_mode=pl.Buffered(1))


def _dot(a, b):
    return jnp.dot(a, b, preferred_element_type=F32)


def _dot_nt(a, b):
    return lax.dot_general(a, b, (((1,), (1,)), ((), ())), preferred_element_type=F32)


def _sigmoid(x):
    return 1.0 / (1.0 + jnp.exp(-x))


def _log_sigmoid(x):
    return jnp.minimum(x, 0.0) - jnp.log(1.0 + jnp.exp(-jnp.abs(x)))


def _norm_mod(x, g, shift, scale):
    ms = jnp.mean(x * x, axis=-1, keepdims=True)
    return x * lax.rsqrt(ms + EPS) * (g * (1.0 + scale)) + shift


def _adaln_kernel(c_ref, w_ref, b_ref, o_ref):
    c = c_ref[...]
    o_ref[...] = _dot(c * _sigmoid(c), w_ref[...]) + b_ref[...]


def _adaln(c_all, w_ada, b_ada):
    rows, d = c_all.shape
    n = w_ada.shape[1]
    tn = d
    return pl.pallas_call(
        _adaln_kernel,
        out_shape=jax.ShapeDtypeStruct((rows, n), F32),
        grid=(n // tn,),
        in_specs=[pl.BlockSpec((rows, d), lambda j: (0, 0)),
                  pl.BlockSpec((d, tn), lambda j: (0, j)),
                  pl.BlockSpec((1, tn), lambda j: (0, j))],
        out_specs=pl.BlockSpec((rows, tn), lambda j: (0, j)),
        compiler_params=_cparams(1),
        name="adaln",
    )(c_all, w_ada, b_ada.reshape(1, n))


def _ffn_kernel(*refs, nb, tl, sub, with_mix, with_final):
    if with_mix:
        x_ref, hm_ref, att_ref, mod_ref, gn_ref, wout_ref, wup_ref, wdn_ref, gfin_ref, o_ref, hb_ref, act_ref = refs
    else:
        x_ref, mod_ref, gn_ref, wup_ref, wdn_ref, gfin_ref, o_ref, hb_ref, act_ref = refs
    m = nb * tl
    d = x_ref.shape[-1]
    dff = wdn_ref.shape[0]
    mod = mod_ref[...]
    shift, scale, gate = (mod[:, 3 * sub + i:3 * sub + i + 1, :] for i in range(3))

    if with_mix:
        mix = _dot(hm_ref[...].reshape(m, D_M), wout_ref[0:D_M, :])
        mix = mix + _dot(att_ref[...].reshape(m, D_A), wout_ref[D_M:, :])
        o_ref[...] = x_ref[...] + mod[:, 5:6, :] * mix.reshape(nb, tl, d)
        res_ref = o_ref
    else:
        res_ref = x_ref

    h = _norm_mod(res_ref[...], gn_ref[...], shift, scale)
    hb_ref[...] = h.reshape(m, d).astype(BF16)
    for c in range(dff // FF_CHUNK):
        lo = c * FF_CHUNK
        g = _dot(hb_ref[...], wup_ref[:, lo:lo + FF_CHUNK])
        u = _dot(hb_ref[...], wup_ref[:, dff + lo:dff + lo + FF_CHUNK])
        act_ref[:, lo:lo + FF_CHUNK] = (g * _sigmoid(g) * u).astype(BF16)
    dn = _dot(act_ref[...], wdn_ref[...])
    y = res_ref[...] + 0.5 * gate * dn.reshape(nb, tl, d)
    if with_final:
        ms = jnp.mean(y * y, axis=-1, keepdims=True)
        y = y * lax.rsqrt(ms + EPS) * gfin_ref[...]
    o_ref[...] = y


def _ffn(x, mod, g_norm_row, w_up, w_dn, g_final, *, nb, tl, sub, mix=None, final=False):
    NB, L, d = x.shape
    dff = w_dn.shape[0]
    m = nb * tl
    grid = (NB // nb, L // tl)
    row = lambda w: pl.BlockSpec((nb, tl, w), lambda i, j: (i, j, 0))
    in_specs = [row(d)]
    args = [x]
    if mix is not None:
        hm, att, w_out = mix
        in_specs += [row(D_M), row(D_A)]
        args += [hm, att]
    in_specs += [pl.BlockSpec((nb, N_MOD, d), lambda i, j: (i, 0, 0)), _const_spec((1, d))]
    args += [mod, g_norm_row]
    if mix is not None:
        in_specs.append(_const_spec(w_out.shape))
        args.append(w_out)
    in_specs += [_const_spec(w_up.shape), _const_spec(w_dn.shape), _const_spec((1, d))]
    args += [w_up, w_dn, g_final]
    kern = functools.partial(_ffn_kernel, nb=nb, tl=tl, sub=sub, with_mix=mix is not None,
                             with_final=final)
    return pl.pallas_call(
        kern,
        out_shape=jax.ShapeDtypeStruct((NB, L, d), F32),
        grid=grid,
        in_specs=in_specs,
        out_specs=row(d),
        scratch_shapes=[pltpu.VMEM((m, d), BF16), pltpu.VMEM((m, dff), BF16)],
        compiler_params=_cparams(2),
        name="ffn%d" % sub,
    )(*args)


def _proj_kernel(x_ref, mod_ref, gn_ref, w_ref, wg_ref, bg_ref, cw_ref, cb_ref, cprev_ref,
                 q_ref, k_ref, v_ref, o_ref, gates_ref, aq_ref, ak_ref, av_ref,
                 kt_ref, vt_ref, cnew_ref, hb_ref, carry_ref, *, nb, tl):
    m = nb * tl
    d = x_ref.shape[-1]
    j = pl.program_id(1)
    mod = mod_ref[...]
    h = _norm_mod(x_ref[...], gn_ref[...], mod[:, 3:4, :], mod[:, 4:5, :])
    hb_ref[...] = h.reshape(m, d).astype(BF16)

    @pl.when(j == 0)
    def _():
        carry_ref[...] = jnp.zeros(carry_ref.shape, F32)
        carry_ref[:, SUBLANES - (CONV_W - 1):SUBLANES, :] = cprev_ref[...]

    def qk_pre(c):
        lo = c * FF_CHUNK
        return _dot(hb_ref[...], w_ref[:, lo:lo + FF_CHUNK]).reshape(nb, tl, FF_CHUNK)

    def conv(c, pre):
        lo = c * FF_CHUNK
        ext = jnp.concatenate([carry_ref[:, :, lo:lo + FF_CHUNK], pre], axis=1)
        acc = cb_ref[:, lo:lo + FF_CHUNK]
        for t in reversed(range(CONV_W)):
            s0 = SUBLANES - (CONV_W - 1) + t
            acc = acc + cw_ref[t:t + 1, lo:lo + FF_CHUNK] * ext[:, s0:s0 + tl, :]
        y = acc * _sigmoid(acc)
        if lo < D_M:
            q_ref[:, :, lo:lo + FF_CHUNK] = y.astype(BF16)
        else:
            k_ref[:, :, lo - D_M:lo - D_M + FF_CHUNK] = (y * DH_M ** -0.5).astype(BF16)
        cnew_ref[:, :, lo:lo + FF_CHUNK] = pre[:, tl - (CONV_W - 1):tl, :]
        carry_ref[:, :, lo:lo + FF_CHUNK] = pre[:, tl - SUBLANES:tl, :]

    def cols(c0):
        return _dot(hb_ref[...], w_ref[:, c0:c0 + D_M]).reshape(nb, tl, D_M)

    pre = qk_pre(0)
    v_ref[...] = cols(2 * D_M).astype(BF16)
    conv(0, pre)
    pre = qk_pre(1)
    o_ref[...] = cols(3 * D_M)
    conv(1, pre)
    pre = qk_pre(2)
    aq_ref[...] = (cols(4 * D_M) * (DH_A ** -0.5 * LOG2E)).astype(BF16)
    conv(2, pre)
    pre = qk_pre(3)
    ak = cols(4 * D_M + D_A)
    ak_ref[...] = ak.astype(BF16)
    kt_ref[...] = ak
    conv(3, pre)
    av = cols(4 * D_M + 2 * D_A)
    av_ref[...] = av.astype(BF16)
    vt_ref[...] = av
    gates_ref[...] = (_dot(hb_ref[...], wg_ref[...]) + bg_ref[...]).reshape(nb, tl, LANES)


def _proj(x, mod, g_norm_row, w_main, w_gates, b_gates, conv_w, conv_b, conv_prev, *, nb, tl):
    NB, L, d = x.shape
    assert tl == min(ATT_WINDOW, L)
    m = nb * tl
    grid = (NB // nb, L // tl)
    row = lambda w: pl.BlockSpec((nb, tl, w), lambda i, j: (i, j, 0))
    per_b = lambda r, w: pl.BlockSpec((nb, r, w), lambda i, j: (i, 0, 0))
    bshape = lambda w, dt: jax.ShapeDtypeStruct((NB, L, w), dt)
    in_specs = [row(d), per_b(N_MOD, d), _const_spec((1, d)), _const_spec(w_main.shape),
                _const_spec(w_gates.shape), _const_spec((1, LANES)), _const_spec(conv_w.shape),
                _const_spec((1, 2 * D_M)), per_b(CONV_W - 1, 2 * D_M)]
    out_shape = [bshape(D_M, BF16), bshape(D_M, BF16), bshape(D_M, BF16), bshape(D_M, F32),
                 bshape(LANES, F32), bshape(D_A, BF16), bshape(D_A, BF16), bshape(D_A, BF16),
                 jax.ShapeDtypeStruct((NB, tl, D_A), F32), jax.ShapeDtypeStruct((NB, tl, D_A), F32),
                 jax.ShapeDtypeStruct((NB, CONV_W - 1, 2 * D_M), F32)]
    out_specs = [row(D_M), row(D_M), row(D_M), row(D_M), row(LANES), row(D_A), row(D_A), row(D_A),
                 per_b(tl, D_A), per_b(tl, D_A), per_b(CONV_W - 1, 2 * D_M)]
    return pl.pallas_call(
        functools.partial(_proj_kernel, nb=nb, tl=tl),
        out_shape=out_shape,
        grid=grid,
        in_specs=in_specs,
        out_specs=out_specs,
        scratch_shapes=[pltpu.VMEM((m, d), BF16), pltpu.VMEM((nb, SUBLANES, 2 * D_M), F32)],
        compiler_params=_cparams(2),
        name="proj",
    )(x, mod, g_norm_row, w_main, w_gates, b_gates, conv_w, conv_b, conv_prev)


def _split_bf16(x):
    hi = x.astype(BF16)
    r = x - hi.astype(F32)
    mid = r.astype(BF16)
    return hi, mid, (r - mid.astype(F32)).astype(BF16)


def _mlstm_kernel(q_ref, k_ref, v_ref, gates_ref, mo_ref, c0_ref, n0_ref, m0_ref, gm_ref,
                  hm_ref, c_ref, n_ref, m_ref, ct_ref, *, nbm, lc, nck, single_chunk):
    j = pl.program_id(1)

    def transpose_states(src_ref, dst_ref):
        def one(i, carry):
            dst_ref[i // NH_M, i % NH_M] = jnp.transpose(src_ref[i // NH_M, i % NH_M])
            return carry
        lax.fori_loop(0, nbm * NH_M, one, 0)

    if single_chunk:
        n_ref[...] = n0_ref[...]
        m_ref[...] = m0_ref[...]
        ct_get = lambda b, h: jnp.transpose(c0_ref[b, h])

        def ct_put(b, h, val):
            c_ref[b, h] = jnp.transpose(val)
    else:
        @pl.when(j == 0)
        def _():
            transpose_states(c0_ref, ct_ref)
            n_ref[...] = n0_ref[...]
            m_ref[...] = m0_ref[...]
        ct_get = lambda b, h: ct_ref[b, h]

        def ct_put(b, h, val):
            ct_ref[b, h] = val

    s_i = lax.broadcasted_iota(jnp.int32, (lc, lc), 0)
    t_i = lax.broadcasted_iota(jnp.int32, (lc, lc), 1)
    causal = s_i <= t_i
    upper = jnp.where(causal, 1.0, 0.0).astype(BF16)
    head_row = lax.broadcasted_iota(jnp.int32, (SUBLANES, 1), 0)

    pairs = [(b, h) for b in range(nbm) for h in range(NH_M)]
    units = [(ck, b) for ck in range(nck) for b in range(nbm)]
    hsl = lambda h: slice(h * DH_M, (h + 1) * DH_M)
    tok = lambda ck: slice(ck * lc, (ck + 1) * lc)

    gate = {}
    for ck, b in units:
        gates_t = jnp.transpose(gates_ref[b, tok(ck)])
        ig_t = gates_t[0:SUBLANES]
        lf_t = _log_sigmoid(gates_t[SUBLANES:2 * SUBLANES])
        b_t = sum(_dot(part, upper) for part in _split_bf16(lf_t))
        c_t = ig_t - b_t
        c2_t = c_t * LOG2E
        c2_cols = jnp.transpose(jnp.concatenate([c2_t, jnp.zeros((LANES - SUBLANES, lc), F32)], axis=0))
        c2_masked = [jnp.where(causal, c2_cols[:, h:h + 1], NEG) for h in range(NH_M)]
        cmax2_t = jnp.concatenate(
            [jnp.max(cm, axis=0, keepdims=True) for cm in c2_masked]
            + [jnp.zeros((SUBLANES - NH_M, lc), F32)], axis=0)
        gate[ck, b] = dict(b_t=b_t, c_t=c_t, c2_masked=c2_masked, cmax2=cmax2_t,
                           m_intra=b_t + cmax2_t * (1.0 / LOG2E))

    early = {}
    for ck, b in units:
        for h in range(NH_M):
            early[ck, b, h] = dict(
                s_t=_dot_nt(k_ref[b, tok(ck), hsl(h)], q_ref[b, tok(ck), hsl(h)]),
                v_t=jnp.transpose(v_ref[b, tok(ck), hsl(h)].astype(F32)))

    for ck in range(nck):
        rows = []
        for b in range(nbm):
            g = gate[ck, b]
            b_t, c_t = g["b_t"], g["c_t"]
            b_end = b_t[:, lc - 1:lc]
            m_prev = m_ref[b][:, 0:1]
            log_g = b_end + c_t
            m_end = jnp.maximum(b_end + m_prev, jnp.max(log_g, axis=-1, keepdims=True))
            g_rows = jnp.exp(log_g - m_end)
            log_inter = b_t + m_prev
            m_tok = jnp.maximum(log_inter, g["m_intra"])
            rows.append(dict(decay=jnp.exp(b_end + m_prev - m_end), g=g_rows,
                             g_b=g_rows.astype(BF16), m_end=m_end,
                             r_intra=jnp.exp(g["m_intra"] - m_tok),
                             w_inter=jnp.exp(log_inter - m_tok), floor=jnp.exp(-m_tok),
                             n_prev_b=n_ref[b].astype(BF16)))

        for b, h in pairs:
            q = q_ref[b, tok(ck), hsl(h)]
            ct_prev = ct_get(b, h)
            early[ck, b, h].update(
                ct=ct_prev if single_chunk else None,
                cq_t=_dot_nt(ct_prev.astype(BF16), q),
                nq=_dot_nt(rows[b]["n_prev_b"], q)[h:h + 1, :])

        n_upd = [jnp.zeros((SUBLANES, DH_M), F32) for _ in range(nbm)]
        for b, h in pairs:
            r, g, e = rows[b], gate[ck, b], early[ck, b, h]
            k = k_ref[b, tok(ck), hsl(h)]
            a_t = jnp.exp2(g["c2_masked"][h] - g["cmax2"][h:h + 1, :]) * e["s_t"]
            r_intra, w_inter = r["r_intra"][h:h + 1, :], r["w_inter"][h:h + 1, :]
            num_t = _dot(e["v_t"].astype(BF16), a_t.astype(BF16)) * r_intra + e["cq_t"] * w_inter
            den = jnp.sum(a_t, axis=0, keepdims=True) * r_intra + e["nq"] * w_inter
            h_t = num_t * (1.0 / jnp.maximum(jnp.abs(den), r["floor"][h:h + 1, :]))
            hn_t = h_t * lax.rsqrt(jnp.mean(h_t * h_t, axis=0, keepdims=True) + EPS)
            hn = jnp.transpose(hn_t) * gm_ref[:, hsl(h)]
            hm_ref[b, tok(ck), hsl(h)] = (_sigmoid(mo_ref[b, tok(ck), hsl(h)]) * hn).astype(BF16)

            vg_t = (e["v_t"] * r["g"][h:h + 1, :]).astype(BF16)
            ct_prev = e["ct"] if single_chunk else ct_get(b, h)
            ct_put(b, h, r["decay"][h:h + 1, :] * ct_prev + _dot(vg_t, k))
            n_upd[b] = n_upd[b] + jnp.where(head_row == h, _dot(r["g_b"], k), 0.0)

        for b in range(nbm):
            n_ref[b] = rows[b]["decay"] * n_ref[b] + n_upd[b]
            m_ref[b] = jnp.broadcast_to(rows[b]["m_end"], (SUBLANES, LANES))

    if not single_chunk:
        @pl.when(j == pl.num_programs(1) - 1)
        def _():
            transpose_states(ct_ref, c_ref)


def _mlstm(q, k, v, gates, mo, c0, n0, m0, g_mlstm_row, *, nbm, lc):
    NB, L, _ = q.shape
    nck = MLSTM_CHUNKS_PER_STEP if L % (MLSTM_CHUNKS_PER_STEP * lc) == 0 else 1
    grid = (NB // nbm, L // (nck * lc))
    row = lambda w: pl.BlockSpec((nbm, nck * lc, w), lambda i, j: (i, j, 0))
    st = lambda s: pl.BlockSpec((nbm,) + s, lambda i, j: (i,) + (0,) * len(s))
    c_s, n_s = (NH_M, DH_M, DH_M), (SUBLANES, DH_M)
    return pl.pallas_call(
        functools.partial(_mlstm_kernel, nbm=nbm, lc=lc, nck=nck, single_chunk=L == lc),
        out_shape=[jax.ShapeDtypeStruct((NB, L, D_M), BF16),
                   jax.ShapeDtypeStruct((NB,) + c_s, F32),
                   jax.ShapeDtypeStruct((NB,) + n_s, F32),
                   jax.ShapeDtypeStruct((NB,) + n_s, F32)],
        grid=grid,
        in_specs=[row(D_M), row(D_M), row(D_M), row(LANES), row(D_M), st(c_s), st(n_s), st(n_s),
                  _const_spec((1, D_M))],
        out_specs=[row(D_M), st(c_s), st(n_s), st(n_s)],
        scratch_shapes=[pltpu.VMEM((nbm,) + c_s, F32)],
        compiler_params=_cparams(2),
        name="mlstm",
    )(q, k, v, gates, mo, c0, n0, m0, g_mlstm_row)


def _build_band_bias(ext_ref, tq, klen, put):
    p_len = ext_ref.shape[-1]
    q_pos = lax.broadcasted_iota(jnp.int32, (tq, klen), 0) + (klen - tq)
    k_pos = lax.broadcasted_iota(jnp.int32, (tq, klen), 1)
    back = q_pos // CHUNK - k_pos // CHUNK
    for h in range(NH_A):
        base = jnp.broadcast_to(ext_ref[h:h + 1, :] * LOG2E, (tq, p_len))
        toeplitz = pltpu.roll(base, 0, 1, stride=1, stride_axis=0)[:, :klen]
        put(h, jnp.where(back >= 0, jnp.where(back <= LEFT_CHUNKS, toeplitz, NEG), NEG))


def _lane_tiles(s):
    n = s.shape[-1]
    if n % LANES:
        return [s]
    return [s[:, i * LANES:(i + 1) * LANES] for i in range(n // LANES)]


def _row_reduce(tiles, op, lane_op):
    by_width = {}
    for t in tiles:
        by_width[t.shape[-1]] = t if t.shape[-1] not in by_width else op(by_width[t.shape[-1]], t)
    return functools.reduce(op, [lane_op(t, axis=-1, keepdims=True) for t in by_width.values()])


def _softmax_pv(ss, vs):
    mx = _row_reduce([t for s in ss for t in _lane_tiles(s)], jnp.maximum, jnp.max)
    es = [jnp.exp2(s - mx) for s in ss]
    l = _row_reduce([t for e in es for t in _lane_tiles(e)], jnp.add, jnp.sum)
    o = functools.reduce(jnp.add, [_dot(e.astype(BF16), v) for e, v in zip(es, vs)])
    return o / l


def _attn_pairs(jobs):
    first = lax.broadcasted_iota(jnp.int32, (1, LANES), 1) < DH_A
    zero = jnp.zeros((), BF16)

    def scores(q, segs, p):
        sl = slice(p * LANES, (p + 1) * LANES)
        qp = q(sl)
        q2 = jnp.concatenate([jnp.where(first, qp, zero), jnp.where(first, zero, qp)], axis=0)
        ss = []
        for k, _, bias, pen in segs:
            s = _dot_nt(q2, k(sl)) + bias(p)
            ss.append(s if pen is None else s + pen)
        return ss

    def finish(ss, segs, put, p):
        sl = slice(p * LANES, (p + 1) * LANES)
        r = ss[0].shape[0] // 2
        o2 = _softmax_pv(ss, [v(sl) for _, v, _, _ in segs])
        put(sl, jnp.where(first, o2[:r], o2[r:]))

    pending = []
    for q, segs, put in jobs:
        for p in range(NH_A // 2):
            pending.append((scores(q, segs, p), segs, put, p))
            if len(pending) > PAIR_ATT_LOOKAHEAD:
                finish(*pending.pop(0))
    while pending:
        finish(*pending.pop(0))


def _attn_prompt_kernel(q_ref, kp_ref, ko_ref, vp_ref, vo_ref, ext_ref, o_ref, bias_ref):
    tq = q_ref.shape[1]
    j = pl.program_id(1)

    @pl.when((pl.program_id(0) == 0) & (j == 0))
    def _():
        def put_bias(h, tile):
            bias_ref[h // 2, (h % 2) * ATT_SUB:(h % 2 + 1) * ATT_SUB, :] = tile
        _build_band_bias(ext_ref, ATT_SUB, tq + ATT_SUB, put_bias)

    def run(pen_prev):
        jobs = []
        for i in range(tq // ATT_SUB):
            lo = i * ATT_SUB
            n_prev = tq - lo
            bias = lambda a, b: (lambda p: bias_ref[p, :, a:b])
            rows = lambda ref, a, b: (lambda sl: ref[0, a:b, sl])

            def put(sl, val, lo=lo):
                o_ref[0, lo:lo + ATT_SUB, sl] = val.astype(o_ref.dtype)

            segs = [(rows(kp_ref, lo, tq), rows(vp_ref, lo, tq), bias(0, n_prev), pen_prev),
                    (rows(ko_ref, 0, lo + ATT_SUB), rows(vo_ref, 0, lo + ATT_SUB),
                     bias(n_prev, tq + ATT_SUB), None)]
            jobs.append((rows(q_ref, lo, lo + ATT_SUB), segs, put))
        _attn_pairs(jobs)

    @pl.when(j >= 1)
    def _():
        run(None)

    @pl.when(j < 1)
    def _():
        run(jnp.full((), NEG, F32))


def _attn_prompt(aq, ak, av, ext):
    NB, L, _ = aq.shape
    tq = ATT_WINDOW
    assert L % tq == 0
    grid = (NB, L // tq)
    row = pl.BlockSpec((1, tq, D_A), lambda b, j: (b, j, 0))
    prev = pl.BlockSpec((1, tq, D_A), lambda b, j: (b, jnp.maximum(j - 1, 0), 0))
    return pl.pallas_call(
        _attn_prompt_kernel,
        out_shape=jax.ShapeDtypeStruct((NB, L, D_A), BF16),
        grid=grid,
        in_specs=[row, prev, row, prev, row, _const_spec(ext.shape)],
        out_specs=row,
        scratch_shapes=[pltpu.VMEM((NH_A // 2, 2 * ATT_SUB, tq + ATT_SUB), F32)],
        compiler_params=_cparams(2),
        name="attn_prompt",
    )(aq, ak, ak, av, av, ext)


def _attn_sample_kernel(q_ref, k_ref, v_ref, ckt_ref, cvt_ref, ext_ref, o_ref, bias_ref, *, nb):
    tq = q_ref.shape[1]
    nc = ckt_ref.shape[-1]

    @pl.when(pl.program_id(0) == 0)
    def _():
        def put_bias(h, tile):
            bias_ref[h] = tile
        _build_band_bias(ext_ref, tq, nc + tq, put_bias)

    def scores(b, h):
        hs = slice(h * DH_A, (h + 1) * DH_A)
        qh = q_ref[b, :, hs]
        return [_dot(qh, ckt_ref[b, h].astype(BF16)) + bias_ref[h, :, 0:nc],
                _dot_nt(qh, k_ref[b, :, hs]) + bias_ref[h, :, nc:nc + tq]]

    def finish(b, h, ss):
        hs = slice(h * DH_A, (h + 1) * DH_A)
        mx = jnp.maximum(*[jnp.max(s, axis=-1, keepdims=True) for s in ss])
        e_c, e_o = [jnp.exp2(s - mx) for s in ss]
        l = jnp.sum(e_c, axis=-1, keepdims=True) + jnp.sum(e_o, axis=-1, keepdims=True)
        o = (_dot_nt(e_c.astype(BF16), cvt_ref[b, h].astype(BF16))
             + _dot(e_o.astype(BF16), v_ref[b, :, hs]))
        return o / l

    outs = {b: [] for b in range(nb)}
    pending = []

    def retire():
        b, h, ss = pending.pop(0)
        outs[b].append(finish(b, h, ss))
        if h == NH_A - 1:
            o_ref[b] = jnp.concatenate(outs[b], axis=-1).astype(o_ref.dtype)

    for b in range(nb):
        for h in range(NH_A):
            pending.append((b, h, scores(b, h)))
            if len(pending) > SAMPLE_ATT_LOOKAHEAD:
                retire()
    while pending:
        retire()


def _attn_sample(aq, ak, av, cache_kt, cache_vt, ext, *, nb):
    NB, L, _ = aq.shape
    nc = cache_kt.shape[-1]
    row = pl.BlockSpec((nb, L, D_A), lambda i: (i, 0, 0))
    crow = pl.BlockSpec((nb, NH_A, DH_A, nc), lambda i: (i, 0, 0, 0))
    return pl.pallas_call(
        functools.partial(_attn_sample_kernel, nb=nb),
        out_shape=jax.ShapeDtypeStruct((NB, L, D_A), BF16),
        grid=(NB // nb,),
        in_specs=[row, row, row, crow, crow, _const_spec(ext.shape)],
        out_specs=row,
        scratch_shapes=[pltpu.VMEM((NH_A, L, nc + L), F32)],
        compiler_params=_cparams(1),
        name="attn_sample",
    )(aq, ak, av, cache_kt, cache_vt, ext)


def _rel_ext(rel_table, tq, klen):
    p_len = -(-(klen + tq) // LANES) * LANES
    n_lo = tq - CHUNK
    n_hi = p_len - tq - MAX_REL
    rep = lambda col, n: jnp.broadcast_to(col, (NH_A, n))
    by_dist = jnp.concatenate([rep(rel_table[:, :1], n_lo), rel_table, rep(rel_table[:, -1:], n_hi)],
                              axis=1)
    rev = by_dist[:, ::-1]
    n = p_len - klen
    return jnp.concatenate([rev[:, n:], rev[:, :n]], axis=1).astype(F32)


def _gate_lanes(g):
    z = lambda n: jnp.zeros((g.shape[0], n), g.dtype)
    return jnp.concatenate([g[:, :NH_M], z(SUBLANES - NH_M), g[:, NH_M:], z(LANES - SUBLANES - NH_M)], axis=1)


def _layer(x, mod, conv_prev, state, cache, w, *, nb, tl, nbm, lc, final):
    NB, L, d = x.shape
    x1 = _ffn(x, mod, w["g0"], w["up1"], w["dn1"], w["g_final"], nb=nb, tl=tl, sub=0)
    (q, k, v, mo, gates, aq, ak, av, k_tail, v_tail, conv_new) = _proj(
        x1, mod, w["g1"], w["w_main"], w["w_gates"], w["b_gates"], w["conv_w"], w["conv_b"],
        conv_prev, nb=nb, tl=tl)
    c0, n0, m0 = state
    n0 = jnp.pad(n0, ((0, 0), (0, SUBLANES - NH_M), (0, 0)))
    m0 = jnp.broadcast_to(jnp.pad(m0, ((0, 0), (0, SUBLANES - NH_M)))[:, :, None], (NB, SUBLANES, LANES))
    hm, c_new, n_new, m_new = _mlstm(q, k, v, gates, mo, c0, n0, m0, w["g_mlstm"], nbm=nbm, lc=lc)
    if cache is None:
        att = _attn_prompt(aq, ak, av, w["ext_prompt"])
    else:
        att = _attn_sample(aq, ak, av, cache[0], cache[1], w["ext_sample"], nb=min(NB, 4))
    y = _ffn(x1, mod, w["g2"], w["up2"], w["dn2"], w["g_final"], nb=nb, tl=tl, sub=2,
             mix=(hm, att, w["w_out"]), final=final)
    n_keep = k_tail.shape[1]
    states = (c_new, n_new[:, :NH_M, :], m_new[:, :NH_M, 0], conv_new,
              k_tail.reshape(NB, n_keep, NH_A, DH_A), v_tail.reshape(NB, n_keep, NH_A, DH_A))
    return y, states


def kernel(x_prompt, x_sample, state_mlstm_C, state_mlstm_n, state_mlstm_m, state_conv, cache_att_k, cache_att_v, c_prompt, c_sample, w_ada, b_ada, g_norm, w_ffn1_up, w_ffn1_down, w_ffn2_up, w_ffn2_down, w_in, conv_w, conv_b, b_gates, g_mlstm, rel_bias_table, w_out, g_final):
    depth = w_ada.shape[0]
    bp, seq, d = x_prompt.shape
    bs, dseq, _ = x_sample.shape
    xp, xs = x_prompt, x_sample
    st_p, st_s = [], []
    n_c = cache_att_k.shape[2]
    tl_p = min(ATT_WINDOW, seq)
    lc_p = min(MLSTM_CHUNK, seq)
    nb_s = max(1, min(bs, ATT_WINDOW // dseq))
    for l in range(depth):
        rows = bp + bs
        pad = (-rows) % SUBLANES
        c_all = jnp.concatenate([c_prompt, c_sample, jnp.zeros((pad, d), F32)], axis=0)
        mod = _adaln(c_all, w_ada[l], b_ada[l])
        mod_p = mod[:bp].reshape(bp, N_MOD, d)
        mod_s = mod[bp:rows].reshape(bs, N_MOD, d)
        off_g = 4 * D_M
        off_a = off_g + 2 * NH_M
        wl = w_in[l]
        rel = rel_bias_table[l]
        w = {
            "g0": g_norm[l, 0:1], "g1": g_norm[l, 1:2], "g2": g_norm[l, 2:3],
            "g_final": g_final.reshape(1, d),
            "up1": w_ffn1_up[l].astype(BF16), "dn1": w_ffn1_down[l].astype(BF16),
            "up2": w_ffn2_up[l].astype(BF16), "dn2": w_ffn2_down[l].astype(BF16),
            "w_main": jnp.concatenate([wl[:, :off_g], wl[:, off_a:]], axis=1).astype(BF16),
            "w_gates": _gate_lanes(wl[:, off_g:off_a]).astype(BF16),
            "b_gates": _gate_lanes(b_gates[l].reshape(1, 2 * NH_M)),
            "conv_w": conv_w[l], "conv_b": conv_b[l].reshape(1, 2 * D_M),
            "g_mlstm": g_mlstm[l].reshape(1, D_M),
            "w_out": w_out[l].astype(BF16),
            "ext_prompt": _rel_ext(rel, ATT_SUB, ATT_WINDOW + ATT_SUB),
            "ext_sample": _rel_ext(rel, dseq, n_c + dseq),
        }
        zero_state = (jnp.zeros((bp, NH_M, DH_M, DH_M), F32), jnp.zeros((bp, NH_M, DH_M), F32),
                      jnp.zeros((bp, NH_M), F32))
        xp, sp = _layer(xp, mod_p, jnp.zeros((bp, CONV_W - 1, 2 * D_M), F32), zero_state, None, w,
                        nb=1, tl=tl_p, nbm=bp, lc=lc_p, final=l == depth - 1)
        state_s = (state_mlstm_C[l], state_mlstm_n[l], state_mlstm_m[l])
        cache = (jnp.transpose(cache_att_k[l], (0, 2, 3, 1)), jnp.transpose(cache_att_v[l], (0, 2, 3, 1)))
        xs, ss = _layer(xs, mod_s, state_conv[l], state_s, cache, w, nb=nb_s, tl=dseq,
                        nbm=min(bs, 4), lc=dseq, final=l == depth - 1)
        st_p.append(sp)
        st_s.append(ss)
    stk = lambda sts, i: jnp.stack([s[i] for s in sts])
    return ((xp, xs) + tuple(stk(st_p, i) for i in range(6)) + tuple(stk(st_s, i) for i in range(6)))
```

```python
import functools

import jax
import jax.numpy as jnp
from jax import lax
from jax.experimental import pallas as pl
from jax.experimental.pallas import tpu as pltpu

F32 = jnp.float32
BF16 = jnp.bfloat16

CHUNK = 64
NH_M = 4
DH_M = 128
D_M = NH_M * DH_M
NH_A = 8
DH_A = 64
D_A = NH_A * DH_A
CONV_W = 4
LEFT_CHUNKS = 8
ATT_WINDOW = LEFT_CHUNKS * CHUNK
MAX_REL = 2 * CHUNK
N_MOD = 9
EPS = 1e-6
NEG = -1e30
LOG2E = 1.4426950408889634

LANES = 128
SUBLANES = 8
V7X_VMEM_BYTES = 64 * 1024 * 1024
VMEM_LIMIT = V7X_VMEM_BYTES - 8 * 1024 * 1024

FF_CHUNK = 256
FFN_SLICES = 8
MLSTM_CHUNK = 256
MLSTM_CHUNKS_PER_STEP = 2
ATT_SUB = 2 * CHUNK
SAMPLE_ATT_LOOKAHEAD = 3
PAIR_ATT_LOOKAHEAD = 1


def _cparams(n_axes):
    return pltpu.CompilerParams(dimension_semantics=("arbitrary",) * n_axes,
                                vmem_limit_bytes=VMEM_LIMIT)


def _const_spec(shape):
    nd = len(shape)
    return pl.BlockSpec(shape, lambda *_: (0,) * nd, pipeline_mode=pl.Buffered(1))


def _dot(a, b):
    return jnp.dot(a, b, preferred_element_type=F32)


def _dot_nt(a, b):
    return lax.dot_general(a, b, (((1,), (1,)), ((), ())), preferred_element_type=F32)


def _sigmoid(x):
    return 1.0 / (1.0 + jnp.exp(-x))


def _log_sigmoid(x):
    return jnp.minimum(x, 0.0) - jnp.log(1.0 + jnp.exp(-jnp.abs(x)))


def _norm_mod(x, g, shift, scale):
    ms = jnp.mean(x * x, axis=-1, keepdims=True)
    return x * lax.rsqrt(ms + EPS) * (g * (1.0 + scale)) + shift


def _adaln_kernel(c_ref, w_ref, b_ref, o_ref):
    c = c_ref[...]
    o_ref[...] = _dot(c * _sigmoid(c), w_ref[...]) + b_ref[...]


def _adaln(c_all, w_ada, b_ada):
    rows, d = c_all.shape
    n = w_ada.shape[1]
    tn = d
    return pl.pallas_call(
        _adaln_kernel,
        out_shape=jax.ShapeDtypeStruct((rows, n), F32),
        grid=(n // tn,),
        in_specs=[pl.BlockSpec((rows, d), lambda j: (0, 0)),
                  pl.BlockSpec((d, tn), lambda j: (0, j)),
                  pl.BlockSpec((1, tn), lambda j: (0, j))],
        out_specs=pl.BlockSpec((rows, tn), lambda j: (0, j)),
        compiler_params=_cparams(1),
        name="adaln",
    )(c_all, w_ada, b_ada.reshape(1, n))


def _ffn_kernel(*refs, nb, tl, sub, with_mix, with_final):
    if with_mix:
        (x_ref, hm_ref, att_ref, mod_ref, gn_ref, wout_ref, wup_ref, wdn_ref, gfin_ref, o_ref, hb_ref,
         act_ref, mixb_ref) = refs
    else:
        x_ref, mod_ref, gn_ref, wup_ref, wdn_ref, gfin_ref, o_ref, hb_ref, act_ref = refs
    d = x_ref.shape[-1]
    dff = wdn_ref.shape[0]
    n_chunks = dff // FF_CHUNK
    if nb > 1:
        snb, stl = nb // 2, tl
        subs = [(slice(i * snb, (i + 1) * snb), slice(None)) for i in range(2)]
    else:
        snb, stl = nb, tl // 2
        subs = [(slice(None), slice(i * stl, (i + 1) * stl)) for i in range(2)]
    m = snb * stl
    sr = m // FFN_SLICES
    res_ref = o_ref if with_mix else x_ref

    def mods(b_sel):
        mod = mod_ref[b_sel]
        return [mod[:, i:i + 1, :] for i in range(N_MOD)]

    def mix_product(s):
        bs, ts = subs[s]
        mix = _dot(hm_ref[bs, ts].reshape(m, D_M), wout_ref[0:D_M, :])
        return mix + _dot(att_ref[bs, ts].reshape(m, D_A), wout_ref[D_M:, :])

    def prep_rows(s, b_sel, rows, flat, mix, zero):
        mod = mods(b_sel)
        res = x_ref[b_sel, rows]
        if with_mix:
            res = res + mod[5] * mix
            o_ref[b_sel, rows] = res
        xin = res if zero is None else res + zero
        h = _norm_mod(xin, gn_ref[...], mod[3 * sub], mod[3 * sub + 1])
        hb_ref[s, flat] = h.reshape(-1, d).astype(BF16)

    def up(s, c):
        lo = c * FF_CHUNK
        g = _dot(hb_ref[s], wup_ref[:, lo:lo + FF_CHUNK])
        u = _dot(hb_ref[s], wup_ref[:, dff + lo:dff + lo + FF_CHUNK])
        act_ref[s, :, lo:lo + FF_CHUNK] = (g * _sigmoid(g) * u).astype(BF16)
        return g

    def finish(s):
        bs, ts = subs[s]
        dn = _dot(act_ref[s], wdn_ref[...])
        y = res_ref[bs, ts] + 0.5 * mods(bs)[3 * sub + 2] * dn.reshape(snb, stl, d)
        if with_final:
            ms = jnp.mean(y * y, axis=-1, keepdims=True)
            y = y * lax.rsqrt(ms + EPS) * gfin_ref[...]
        o_ref[bs, ts] = y

    bs_a, ts_a = subs[0]
    prep_rows(0, bs_a, ts_a, slice(None), mix_product(0).reshape(snb, stl, d) if with_mix else None,
              None)
    if with_mix:
        mixb_ref[...] = mix_product(1)
    for c in range(n_chunks):
        g = up(0, c)
        if c < FFN_SLICES:
            bits = pltpu.bitcast(g[0:SUBLANES, 0:LANES], jnp.uint32)
            zero = pltpu.bitcast((bits >> 16) >> 16, F32)[0:1, :]
            zero = jnp.concatenate([zero] * (d // LANES), axis=1)
            flat = slice(c * sr, (c + 1) * sr)
            if nb > 1:
                b0 = snb + (c * sr) // stl
                b_sel, rows = slice(b0, b0 + 1), slice((c * sr) % stl, (c * sr) % stl + sr)
            else:
                b_sel, rows = slice(None), slice(stl + c * sr, stl + (c + 1) * sr)
            mix = mixb_ref[flat].reshape(1, sr, d) if with_mix else None
            prep_rows(1, b_sel, rows, flat, mix, zero)
    finish(0)
    for c in range(n_chunks):
        up(1, c)
    finish(1)


def _ffn(x, mod, g_norm_row, w_up, w_dn, g_final, *, nb, tl, sub, mix=None, final=False):
    NB, L, d = x.shape
    dff = w_dn.shape[0]
    m = nb * tl
    if nb > 1:
        nb = 2 * nb
        assert NB % nb == 0
    else:
        tl = 2 * tl
        assert L % tl == 0
    assert m % (16 * FFN_SLICES) == 0
    grid = (NB // nb, L // tl)
    row = lambda w: pl.BlockSpec((nb, tl, w), lambda i, j: (i, j, 0))
    in_specs = [row(d)]
    args = [x]
    scratch = [pltpu.VMEM((2, m, d), BF16), pltpu.VMEM((2, m, dff), BF16)]
    if mix is not None:
        scratch.append(pltpu.VMEM((m, d), F32))
    if mix is not None:
        hm, att, w_out = mix
        in_specs += [row(D_M), row(D_A)]
        args += [hm, att]
    in_specs += [pl.BlockSpec((nb, N_MOD, d), lambda i, j: (i, 0, 0)), _const_spec((1, d))]
    args += [mod, g_norm_row]
    if mix is not None:
        in_specs.append(_const_spec(w_out.shape))
        args.append(w_out)
    in_specs += [_const_spec(w_up.shape), _const_spec(w_dn.shape), _const_spec((1, d))]
    args += [w_up, w_dn, g_final]
    kern = functools.partial(_ffn_kernel, nb=nb, tl=tl, sub=sub, with_mix=mix is not None,
                             with_final=final)
    return pl.pallas_call(
        kern,
        out_shape=jax.ShapeDtypeStruct((NB, L, d), F32),
        grid=grid,
        in_specs=in_specs,
        out_specs=row(d),
        scratch_shapes=scratch,
        compiler_params=_cparams(2),
        name="ffn%d" % sub,
    )(*args)


def _proj_kernel(x_ref, mod_ref, gn_ref, w_ref, wg_ref, bg_ref, cw_ref, cb_ref, cprev_ref,
                 q_ref, k_ref, v_ref, o_ref, gates_ref, aq_ref, ak_ref, av_ref,
                 kt_ref, vt_ref, cnew_ref, hb_ref, carry_ref, *, nb, tl):
    m = nb * tl
    d = x_ref.shape[-1]
    j = pl.program_id(1)
    mod = mod_ref[...]
    h = _norm_mod(x_ref[...], gn_ref[...], mod[:, 3:4, :], mod[:, 4:5, :])
    hb_ref[...] = h.reshape(m, d).astype(BF16)

    @pl.when(j == 0)
    def _():
        carry_ref[...] = jnp.zeros(carry_ref.shape, F32)
        carry_ref[:, SUBLANES - (CONV_W - 1):SUBLANES, :] = cprev_ref[...]

    def qk_pre(c):
        lo = c * FF_CHUNK
        return _dot(hb_ref[...], w_ref[:, lo:lo + FF_CHUNK]).reshape(nb, tl, FF_CHUNK)

    def conv(c, pre):
        lo = c * FF_CHUNK
        ext = jnp.concatenate([carry_ref[:, :, lo:lo + FF_CHUNK], pre], axis=1)
        acc = cb_ref[:, lo:lo + FF_CHUNK]
        for t in reversed(range(CONV_W)):
            s0 = SUBLANES - (CONV_W - 1) + t
            acc = acc + cw_ref[t:t + 1, lo:lo + FF_CHUNK] * ext[:, s0:s0 + tl, :]
        y = acc * _sigmoid(acc)
        if lo < D_M:
            q_ref[:, :, lo:lo + FF_CHUNK] = y.astype(BF16)
        else:
            k_ref[:, :, lo - D_M:lo - D_M + FF_CHUNK] = (y * DH_M ** -0.5).astype(BF16)
        cnew_ref[:, :, lo:lo + FF_CHUNK] = pre[:, tl - (CONV_W - 1):tl, :]
        carry_ref[:, :, lo:lo + FF_CHUNK] = pre[:, tl - SUBLANES:tl, :]

    def cols(c0):
        return _dot(hb_ref[...], w_ref[:, c0:c0 + D_M]).reshape(nb, tl, D_M)

    pre = qk_pre(0)
    v_ref[...] = cols(2 * D_M).astype(BF16)
    conv(0, pre)
    pre = qk_pre(1)
    o_ref[...] = cols(3 * D_M)
    conv(1, pre)
    pre = qk_pre(2)
    aq_ref[...] = (cols(4 * D_M) * (DH_A ** -0.5 * LOG2E)).astype(BF16)
    conv(2, pre)
    pre = qk_pre(3)
    ak = cols(4 * D_M + D_A)
    ak_ref[...] = ak.astype(BF16)
    kt_ref[...] = ak
    conv(3, pre)
    av = cols(4 * D_M + 2 * D_A)
    av_ref[...] = av.astype(BF16)
    vt_ref[...] = av
    gates_ref[...] = (_dot(hb_ref[...], wg_ref[...]) + bg_ref[...]).reshape(nb, tl, LANES)


def _proj(x, mod, g_norm_row, w_main, w_gates, b_gates, conv_w, conv_b, conv_prev, *, nb, tl):
    NB, L, d = x.shape
    assert tl == min(ATT_WINDOW, L)
    m = nb * tl
    grid = (NB // nb, L // tl)
    row = lambda w: pl.BlockSpec((nb, tl, w), lambda i, j: (i, j, 0))
    per_b = lambda r, w: pl.BlockSpec((nb, r, w), lambda i, j: (i, 0, 0))
    bshape = lambda w, dt: jax.ShapeDtypeStruct((NB, L, w), dt)
    in_specs = [row(d), per_b(N_MOD, d), _const_spec((1, d)), _const_spec(w_main.shape),
                _const_spec(w_gates.shape), _const_spec((1, LANES)), _const_spec(conv_w.shape),
                _const_spec((1, 2 * D_M)), per_b(CONV_W - 1, 2 * D_M)]
    out_shape = [bshape(D_M, BF16), bshape(D_M, BF16), bshape(D_M, BF16), bshape(D_M, F32),
                 bshape(LANES, F32), bshape(D_A, BF16), bshape(D_A, BF16), bshape(D_A, BF16),
                 jax.ShapeDtypeStruct((NB, tl, D_A), F32), jax.ShapeDtypeStruct((NB, tl, D_A), F32),
                 jax.ShapeDtypeStruct((NB, CONV_W - 1, 2 * D_M), F32)]
    out_specs = [row(D_M), row(D_M), row(D_M), row(D_M), row(LANES), row(D_A), row(D_A), row(D_A),
                 per_b(tl, D_A), per_b(tl, D_A), per_b(CONV_W - 1, 2 * D_M)]
    return pl.pallas_call(
        functools.partial(_proj_kernel, nb=nb, tl=tl),
        out_shape=out_shape,
        grid=grid,
        in_specs=in_specs,
        out_specs=out_specs,
        scratch_shapes=[pltpu.VMEM((m, d), BF16), pltpu.VMEM((nb, SUBLANES, 2 * D_M), F32)],
        compiler_params=_cparams(2),
        name="proj",
    )(x, mod, g_norm_row, w_main, w_gates, b_gates, conv_w, conv_b, conv_prev)


def _split_bf16(x):
    hi = x.astype(BF16)
    r = x - hi.astype(F32)
    mid = r.astype(BF16)
    return hi, mid, (r - mid.astype(F32)).astype(BF16)


def _mlstm_kernel(q_ref, k_ref, v_ref, gates_ref, mo_ref, c0_ref, n0_ref, m0_ref, gm_ref,
                  hm_ref, c_ref, n_ref, m_ref, ct_ref, *, nbm, lc, nck, single_chunk):
    j = pl.program_id(1)

    def transpose_states(src_ref, dst_ref):
        def one(i, carry):
            dst_ref[i // NH_M, i % NH_M] = jnp.transpose(src_ref[i // NH_M, i % NH_M])
            return carry
        lax.fori_loop(0, nbm * NH_M, one, 0)

    if single_chunk:
        n_ref[...] = n0_ref[...]
        m_ref[...] = m0_ref[...]
        ct_get = lambda b, h: jnp.transpose(c0_ref[b, h])

        def ct_put(b, h, val):
            c_ref[b, h] = jnp.transpose(val)
    else:
        @pl.when(j == 0)
        def _():
            transpose_states(c0_ref, ct_ref)
            n_ref[...] = n0_ref[...]
            m_ref[...] = m0_ref[...]
        ct_get = lambda b, h: ct_ref[b, h]

        def ct_put(b, h, val):
            ct_ref[b, h] = val

    s_i = lax.broadcasted_iota(jnp.int32, (lc, lc), 0)
    t_i = lax.broadcasted_iota(jnp.int32, (lc, lc), 1)
    causal = s_i <= t_i
    upper = jnp.where(causal, 1.0, 0.0).astype(BF16)
    head_row = lax.broadcasted_iota(jnp.int32, (SUBLANES, 1), 0)

    pairs = [(b, h) for b in range(nbm) for h in range(NH_M)]
    units = [(ck, b) for ck in range(nck) for b in range(nbm)]
    hsl = lambda h: slice(h * DH_M, (h + 1) * DH_M)
    tok = lambda ck: slice(ck * lc, (ck + 1) * lc)

    gate = {}
    for ck, b in units:
        gates_t = jnp.transpose(gates_ref[b, tok(ck)])
        ig_t = gates_t[0:SUBLANES]
        lf_t = _log_sigmoid(gates_t[SUBLANES:2 * SUBLANES])
        b_t = sum(_dot(part, upper) for part in _split_bf16(lf_t))
        c_t = ig_t - b_t
        c2_t = c_t * LOG2E
        c2_cols = jnp.transpose(jnp.concatenate([c2_t, jnp.zeros((LANES - SUBLANES, lc), F32)], axis=0))
        c2_masked = [jnp.where(causal, c2_cols[:, h:h + 1], NEG) for h in range(NH_M)]
        cmax2_t = jnp.concatenate(
            [jnp.max(cm, axis=0, keepdims=True) for cm in c2_masked]
            + [jnp.zeros((SUBLANES - NH_M, lc), F32)], axis=0)
        gate[ck, b] = dict(b_t=b_t, c_t=c_t, c2_masked=c2_masked, cmax2=cmax2_t,
                           m_intra=b_t + cmax2_t * (1.0 / LOG2E))

    early = {}
    for ck, b in units:
        for h in range(NH_M):
            early[ck, b, h] = dict(
                s_t=_dot_nt(k_ref[b, tok(ck), hsl(h)], q_ref[b, tok(ck), hsl(h)]),
                v_t=jnp.transpose(v_ref[b, tok(ck), hsl(h)].astype(F32)))

    for ck in range(nck):
        rows = []
        for b in range(nbm):
            g = gate[ck, b]
            b_t, c_t = g["b_t"], g["c_t"]
            b_end = b_t[:, lc - 1:lc]
            m_prev = m_ref[b][:, 0:1]
            log_g = b_end + c_t
            m_end = jnp.maximum(b_end + m_prev, jnp.max(log_g, axis=-1, keepdims=True))
            g_rows = jnp.exp(log_g - m_end)
            log_inter = b_t + m_prev
            m_tok = jnp.maximum(log_inter, g["m_intra"])
            rows.append(dict(decay=jnp.exp(b_end + m_prev - m_end), g=g_rows,
                             g_b=g_rows.astype(BF16), m_end=m_end,
                             r_intra=jnp.exp(g["m_intra"] - m_tok),
                             w_inter=jnp.exp(log_inter - m_tok), floor=jnp.exp(-m_tok),
                             n_prev_b=n_ref[b].astype(BF16)))

        for b, h in pairs:
            q = q_ref[b, tok(ck), hsl(h)]
            ct_prev = ct_get(b, h)
            early[ck, b, h].update(
                ct=ct_prev if single_chunk else None,
                cq_t=_dot_nt(ct_prev.astype(BF16), q),
                nq=_dot_nt(rows[b]["n_prev_b"], q)[h:h + 1, :])

        n_upd = [jnp.zeros((SUBLANES, DH_M), F32) for _ in range(nbm)]
        for b, h in pairs:
            r, g, e = rows[b], gate[ck, b], early[ck, b, h]
            k = k_ref[b, tok(ck), hsl(h)]
            a_t = jnp.exp2(g["c2_masked"][h] - g["cmax2"][h:h + 1, :]) * e["s_t"]
            r_intra, w_inter = r["r_intra"][h:h + 1, :], r["w_inter"][h:h + 1, :]
            num_t = _dot(e["v_t"].astype(BF16), a_t.astype(BF16)) * r_intra + e["cq_t"] * w_inter
            den = jnp.sum(a_t, axis=0, keepdims=True) * r_intra + e["nq"] * w_inter
            h_t = num_t * (1.0 / jnp.maximum(jnp.abs(den), r["floor"][h:h + 1, :]))
            hn_t = h_t * lax.rsqrt(jnp.mean(h_t * h_t, axis=0, keepdims=True) + EPS)
            hn = jnp.transpose(hn_t) * gm_ref[:, hsl(h)]
            hm_ref[b, tok(ck), hsl(h)] = (_sigmoid(mo_ref[b, tok(ck), hsl(h)]) * hn).astype(BF16)

            vg_t = (e["v_t"] * r["g"][h:h + 1, :]).astype(BF16)
            ct_prev = e["ct"] if single_chunk else ct_get(b, h)
            ct_put(b, h, r["decay"][h:h + 1, :] * ct_prev + _dot(vg_t, k))
            n_upd[b] = n_upd[b] + jnp.where(head_row == h, _dot(r["g_b"], k), 0.0)

        for b in range(nbm):
            n_ref[b] = rows[b]["decay"] * n_ref[b] + n_upd[b]
            m_ref[b] = jnp.broadcast_to(rows[b]["m_end"], (SUBLANES, LANES))

    if not single_chunk:
        @pl.when(j == pl.num_programs(1) - 1)
        def _():
            transpose_states(ct_ref, c_ref)


def _mlstm(q, k, v, gates, mo, c0, n0, m0, g_mlstm_row, *, nbm, lc):
    NB, L, _ = q.shape
    nck = MLSTM_CHUNKS_PER_STEP if L % (MLSTM_CHUNKS_PER_STEP * lc) == 0 else 1
    grid = (NB // nbm, L // (nck * lc))
    row = lambda w: pl.BlockSpec((nbm, nck * lc, w), lambda i, j: (i, j, 0))
    st = lambda s: pl.BlockSpec((nbm,) + s, lambda i, j: (i,) + (0,) * len(s))
    c_s, n_s = (NH_M, DH_M, DH_M), (SUBLANES, DH_M)
    return pl.pallas_call(
        functools.partial(_mlstm_kernel, nbm=nbm, lc=lc, nck=nck, single_chunk=L == lc),
        out_shape=[jax.ShapeDtypeStruct((NB, L, D_M), BF16),
                   jax.ShapeDtypeStruct((NB,) + c_s, F32),
                   jax.ShapeDtypeStruct((NB,) + n_s, F32),
                   jax.ShapeDtypeStruct((NB,) + n_s, F32)],
        grid=grid,
        in_specs=[row(D_M), row(D_M), row(D_M), row(LANES), row(D_M), st(c_s), st(n_s), st(n_s),
                  _const_spec((1, D_M))],
        out_specs=[row(D_M), st(c_s), st(n_s), st(n_s)],
        scratch_shapes=[pltpu.VMEM((nbm,) + c_s, F32)],
        compiler_params=_cparams(2),
        name="mlstm",
    )(q, k, v, gates, mo, c0, n0, m0, g_mlstm_row)


def _build_band_bias(ext_ref, tq, klen, put):
    p_len = ext_ref.shape[-1]
    q_pos = lax.broadcasted_iota(jnp.int32, (tq, klen), 0) + (klen - tq)
    k_pos = lax.broadcasted_iota(jnp.int32, (tq, klen), 1)
    back = q_pos // CHUNK - k_pos // CHUNK
    for h in range(NH_A):
        base = jnp.broadcast_to(ext_ref[h:h + 1, :] * LOG2E, (tq, p_len))
        toeplitz = pltpu.roll(base, 0, 1, stride=1, stride_axis=0)[:, :klen]
        put(h, jnp.where(back >= 0, jnp.where(back <= LEFT_CHUNKS, toeplitz, NEG), NEG))


def _lane_tiles(s):
    n = s.shape[-1]
    if n % LANES:
        return [s]
    return [s[:, i * LANES:(i + 1) * LANES] for i in range(n // LANES)]


def _row_reduce(tiles, op, lane_op):
    by_width = {}
    for t in tiles:
        by_width[t.shape[-1]] = t if t.shape[-1] not in by_width else op(by_width[t.shape[-1]], t)
    return functools.reduce(op, [lane_op(t, axis=-1, keepdims=True) for t in by_width.values()])


def _softmax_pv(ss, vs):
    mx = _row_reduce([t for s in ss for t in _lane_tiles(s)], jnp.maximum, jnp.max)
    es = [jnp.exp2(s - mx) for s in ss]
    l = _row_reduce([t for e in es for t in _lane_tiles(e)], jnp.add, jnp.sum)
    o = functools.reduce(jnp.add, [_dot(e.astype(BF16), v) for e, v in zip(es, vs)])
    return o / l


def _attn_pairs(jobs):
    first = lax.broadcasted_iota(jnp.int32, (1, LANES), 1) < DH_A
    zero = jnp.zeros((), BF16)

    def scores(q, segs, p):
        sl = slice(p * LANES, (p + 1) * LANES)
        qp = q(sl)
        q2 = jnp.concatenate([jnp.where(first, qp, zero), jnp.where(first, zero, qp)], axis=0)
        ss = []
        for k, _, bias, pen in segs:
            s = _dot_nt(q2, k(sl)) + bias(p)
            ss.append(s if pen is None else s + pen)
        return ss

    def finish(ss, segs, put, p):
        sl = slice(p * LANES, (p + 1) * LANES)
        r = ss[0].shape[0] // 2
        o2 = _softmax_pv(ss, [v(sl) for _, v, _, _ in segs])
        put(sl, jnp.where(first, o2[:r], o2[r:]))

    pending = []
    for q, segs, put in jobs:
        for p in range(NH_A // 2):
            pending.append((scores(q, segs, p), segs, put, p))
            if len(pending) > PAIR_ATT_LOOKAHEAD:
                finish(*pending.pop(0))
    while pending:
        finish(*pending.pop(0))


def _attn_prompt_kernel(q_ref, kp_ref, ko_ref, vp_ref, vo_ref, ext_ref, o_ref, bias_ref):
    tq = q_ref.shape[1]
    j = pl.program_id(1)

    @pl.when((pl.program_id(0) == 0) & (j == 0))
    def _():
        def put_bias(h, tile):
            bias_ref[h // 2, (h % 2) * ATT_SUB:(h % 2 + 1) * ATT_SUB, :] = tile
        _build_band_bias(ext_ref, ATT_SUB, tq + ATT_SUB, put_bias)

    def run(pen_prev):
        jobs = []
        for i in range(tq // ATT_SUB):
            lo = i * ATT_SUB
            n_prev = tq - lo
            bias = lambda a, b: (lambda p: bias_ref[p, :, a:b])
            rows = lambda ref, a, b: (lambda sl: ref[0, a:b, sl])

            def put(sl, val, lo=lo):
                o_ref[0, lo:lo + ATT_SUB, sl] = val.astype(o_ref.dtype)

            segs = [(rows(kp_ref, lo, tq), rows(vp_ref, lo, tq), bias(0, n_prev), pen_prev),
                    (rows(ko_ref, 0, lo + ATT_SUB), rows(vo_ref, 0, lo + ATT_SUB),
                     bias(n_prev, tq + ATT_SUB), None)]
            jobs.append((rows(q_ref, lo, lo + ATT_SUB), segs, put))
        _attn_pairs(jobs)

    @pl.when(j >= 1)
    def _():
        run(None)

    @pl.when(j < 1)
    def _():
        run(jnp.full((), NEG, F32))


def _attn_prompt(aq, ak, av, ext):
    NB, L, _ = aq.shape
    tq = ATT_WINDOW
    assert L % tq == 0
    grid = (NB, L // tq)
    row = pl.BlockSpec((1, tq, D_A), lambda b, j: (b, j, 0))
    prev = pl.BlockSpec((1, tq, D_A), lambda b, j: (b, jnp.maximum(j - 1, 0), 0))
    return pl.pallas_call(
        _attn_prompt_kernel,
        out_shape=jax.ShapeDtypeStruct((NB, L, D_A), BF16),
        grid=grid,
        in_specs=[row, prev, row, prev, row, _const_spec(ext.shape)],
        out_specs=row,
        scratch_shapes=[pltpu.VMEM((NH_A // 2, 2 * ATT_SUB, tq + ATT_SUB), F32)],
        compiler_params=_cparams(2),
        name="attn_prompt",
    )(aq, ak, ak, av, av, ext)


def _attn_sample_kernel(q_ref, k_ref, v_ref, ckt_ref, cvt_ref, ext_ref, o_ref, bias_ref, *, nb):
    tq = q_ref.shape[1]
    nc = ckt_ref.shape[-1]

    @pl.when(pl.program_id(0) == 0)
    def _():
        def put_bias(h, tile):
            bias_ref[h] = tile
        _build_band_bias(ext_ref, tq, nc + tq, put_bias)

    def scores(b, h):
        hs = slice(h * DH_A, (h + 1) * DH_A)
        qh = q_ref[b, :, hs]
        return [_dot(qh, ckt_ref[b, h].astype(BF16)) + bias_ref[h, :, 0:nc],
                _dot_nt(qh, k_ref[b, :, hs]) + bias_ref[h, :, nc:nc + tq]]

    def finish(b, h, ss):
        hs = slice(h * DH_A, (h + 1) * DH_A)
        mx = jnp.maximum(*[jnp.max(s, axis=-1, keepdims=True) for s in ss])
        e_c, e_o = [jnp.exp2(s - mx) for s in ss]
        l = jnp.sum(e_c, axis=-1, keepdims=True) + jnp.sum(e_o, axis=-1, keepdims=True)
        o = (_dot_nt(e_c.astype(BF16), cvt_ref[b, h].astype(BF16))
             + _dot(e_o.astype(BF16), v_ref[b, :, hs]))
        return o / l

    outs = {b: [] for b in range(nb)}
    pending = []

    def retire():
        b, h, ss = pending.pop(0)
        outs[b].append(finish(b, h, ss))
        if h == NH_A - 1:
            o_ref[b] = jnp.concatenate(outs[b], axis=-1).astype(o_ref.dtype)

    for b in range(nb):
        for h in range(NH_A):
            pending.append((b, h, scores(b, h)))
            if len(pending) > SAMPLE_ATT_LOOKAHEAD:
                retire()
    while pending:
        retire()


def _attn_sample(aq, ak, av, cache_kt, cache_vt, ext, *, nb):
    NB, L, _ = aq.shape
    nc = cache_kt.shape[-1]
    row = pl.BlockSpec((nb, L, D_A), lambda i: (i, 0, 0))
    crow = pl.BlockSpec((nb, NH_A, DH_A, nc), lambda i: (i, 0, 0, 0))
    return pl.pallas_call(
        functools.partial(_attn_sample_kernel, nb=nb),
        out_shape=jax.ShapeDtypeStruct((NB, L, D_A), BF16),
        grid=(NB // nb,),
        in_specs=[row, row, row, crow, crow, _const_spec(ext.shape)],
        out_specs=row,
        scratch_shapes=[pltpu.VMEM((NH_A, L, nc + L), F32)],
        compiler_params=_cparams(1),
        name="attn_sample",
    )(aq, ak, av, cache_kt, cache_vt, ext)


def _rel_ext(rel_table, tq, klen):
    p_len = -(-(klen + tq) // LANES) * LANES
    n_lo = tq - CHUNK
    n_hi = p_len - tq - MAX_REL
    rep = lambda col, n: jnp.broadcast_to(col, (NH_A, n))
    by_dist = jnp.concatenate([rep(rel_table[:, :1], n_lo), rel_table, rep(rel_table[:, -1:], n_hi)],
                              axis=1)
    rev = by_dist[:, ::-1]
    n = p_len - klen
    return jnp.concatenate([rev[:, n:], rev[:, :n]], axis=1).astype(F32)


def _gate_lanes(g):
    z = lambda n: jnp.zeros((g.shape[0], n), g.dtype)
    return jnp.concatenate([g[:, :NH_M], z(SUBLANES - NH_M), g[:, NH_M:], z(LANES - SUBLANES - NH_M)], axis=1)


def _layer(x, mod, conv_prev, state, cache, w, *, nb, tl, nbm, lc, final):
    NB, L, d = x.shape
    x1 = _ffn(x, mod, w["g0"], w["up1"], w["dn1"], w["g_final"], nb=nb, tl=tl, sub=0)
    (q, k, v, mo, gates, aq, ak, av, k_tail, v_tail, conv_new) = _proj(
        x1, mod, w["g1"], w["w_main"], w["w_gates"], w["b_gates"], w["conv_w"], w["conv_b"],
        conv_prev, nb=nb, tl=tl)
    c0, n0, m0 = state
    n0 = jnp.pad(n0, ((0, 0), (0, SUBLANES - NH_M), (0, 0)))
    m0 = jnp.broadcast_to(jnp.pad(m0, ((0, 0), (0, SUBLANES - NH_M)))[:, :, None], (NB, SUBLANES, LANES))
    hm, c_new, n_new, m_new = _mlstm(q, k, v, gates, mo, c0, n0, m0, w["g_mlstm"], nbm=nbm, lc=lc)
    if cache is None:
        att = _attn_prompt(aq, ak, av, w["ext_prompt"])
    else:
        att = _attn_sample(aq, ak, av, cache[0], cache[1], w["ext_sample"], nb=min(NB, 4))
    y = _ffn(x1, mod, w["g2"], w["up2"], w["dn2"], w["g_final"], nb=nb, tl=tl, sub=2,
             mix=(hm, att, w["w_out"]), final=final)
    n_keep = k_tail.shape[1]
    states = (c_new, n_new[:, :NH_M, :], m_new[:, :NH_M, 0], conv_new,
              k_tail.reshape(NB, n_keep, NH_A, DH_A), v_tail.reshape(NB, n_keep, NH_A, DH_A))
    return y, states


def kernel(x_prompt, x_sample, state_mlstm_C, state_mlstm_n, state_mlstm_m, state_conv, cache_att_k, cache_att_v, c_prompt, c_sample, w_ada, b_ada, g_norm, w_ffn1_up, w_ffn1_down, w_ffn2_up, w_ffn2_down, w_in, conv_w, conv_b, b_gates, g_mlstm, rel_bias_table, w_out, g_final):
    depth = w_ada.shape[0]
    bp, seq, d = x_prompt.shape
    bs, dseq, _ = x_sample.shape
    xp, xs = x_prompt, x_sample
    st_p, st_s = [], []
    n_c = cache_att_k.shape[2]
    tl_p = min(ATT_WINDOW, seq)
    lc_p = min(MLSTM_CHUNK, seq)
    nb_s = max(1, min(bs, ATT_WINDOW // dseq))
    for l in range(depth):
        rows = bp + bs
        pad = (-rows) % SUBLANES
        c_all = jnp.concatenate([c_prompt, c_sample, jnp.zeros((pad, d), F32)], axis=0)
        mod = _adaln(c_all, w_ada[l], b_ada[l])
        mod_p = mod[:bp].reshape(bp, N_MOD, d)
        mod_s = mod[bp:rows].reshape(bs, N_MOD, d)
        off_g = 4 * D_M
        off_a = off_g + 2 * NH_M
        wl = w_in[l]
        rel = rel_bias_table[l]
        w = {
            "g0": g_norm[l, 0:1], "g1": g_norm[l, 1:2], "g2": g_norm[l, 2:3],
            "g_final": g_final.reshape(1, d),
            "up1": w_ffn1_up[l].astype(BF16), "dn1": w_ffn1_down[l].astype(BF16),
            "up2": w_ffn2_up[l].astype(BF16), "dn2": w_ffn2_down[l].astype(BF16),
            "w_main": jnp.concatenate([wl[:, :off_g], wl[:, off_a:]], axis=1).astype(BF16),
            "w_gates": _gate_lanes(wl[:, off_g:off_a]).astype(BF16),
            "b_gates": _gate_lanes(b_gates[l].reshape(1, 2 * NH_M)),
            "conv_w": conv_w[l], "conv_b": conv_b[l].reshape(1, 2 * D_M),
            "g_mlstm": g_mlstm[l].reshape(1, D_M),
            "w_out": w_out[l].astype(BF16),
            "ext_prompt": _rel_ext(rel, ATT_SUB, ATT_WINDOW + ATT_SUB),
            "ext_sample": _rel_ext(rel, dseq, n_c + dseq),
        }
        zero_state = (jnp.zeros((bp, NH_M, DH_M, DH_M), F32), jnp.zeros((bp, NH_M, DH_M), F32),
                      jnp.zeros((bp, NH_M), F32))
        xp, sp = _layer(xp, mod_p, jnp.zeros((bp, CONV_W - 1, 2 * D_M), F32), zero_state, None, w,
                        nb=1, tl=tl_p, nbm=bp, lc=lc_p, final=l == depth - 1)
        state_s = (state_mlstm_C[l], state_mlstm_n[l], state_mlstm_m[l])
        cache = (jnp.transpose(cache_att_k[l], (0, 2, 3, 1)), jnp.transpose(cache_att_v[l], (0, 2, 3, 1)))
        xs, ss = _layer(xs, mod_s, state_conv[l], state_s, cache, w, nb=nb_s, tl=dseq,
                        nbm=min(bs, 4), lc=dseq, final=l == depth - 1)
        st_p.append(sp)
        st_s.append(ss)
    stk = lambda sts, i: jnp.stack([s[i] for s in sts])
    return ((xp, xs) + tuple(stk(st_p, i) for i in range(6)) + tuple(stk(st_s, i) for i in range(6)))
```

```python
import functools

import jax
import jax.numpy as jnp
from jax import lax
from jax.experimental import pallas as pl
from jax.experimental.pallas import tpu as pltpu

F32 = jnp.float32
BF16 = jnp.bfloat16

CHUNK = 64
NH_M = 4
DH_M = 128
D_M = NH_M * DH_M
NH_A = 8
DH_A = 64
D_A = NH_A * DH_A
CONV_W = 4
LEFT_CHUNKS = 8
ATT_WINDOW = LEFT_CHUNKS * CHUNK
MAX_REL = 2 * CHUNK
N_MOD = 9
EPS = 1e-6
NEG = -1e30
LOG2E = 1.4426950408889634

LANES = 128
SUBLANES = 8
V7X_VMEM_BYTES = 64 * 1024 * 1024
VMEM_LIMIT = V7X_VMEM_BYTES - 8 * 1024 * 1024

FF_CHUNK = 256
FFN_SLICES = 8
CONV_SLICES = 32
MLSTM_CHUNK = 256
MLSTM_CHUNKS_PER_STEP = 2
ATT_SUB = 2 * CHUNK
SAMPLE_ATT_LOOKAHEAD = 3
PAIR_ATT_LOOKAHEAD = 1


def _cparams(n_axes):
    return pltpu.CompilerParams(dimension_semantics=("arbitrary",) * n_axes,
                                vmem_limit_bytes=VMEM_LIMIT)


def _const_spec(shape):
    nd = len(shape)
    return pl.BlockSpec(shape, lambda *_: (0,) * nd, pipeline_mode=pl.Buffered(1))


def _dot(a, b):
    return jnp.dot(a, b, preferred_element_type=F32)


def _dot_nt(a, b):
    return lax.dot_general(a, b, (((1,), (1,)), ((), ())), preferred_element_type=F32)


def _sigmoid(x):
    return 1.0 / (1.0 + jnp.exp(-x))


def _log_sigmoid(x):
    return jnp.minimum(x, 0.0) - jnp.log(1.0 + jnp.exp(-jnp.abs(x)))


def _norm_mod(x, g, shift, scale):
    ms = jnp.mean(x * x, axis=-1, keepdims=True)
    return x * lax.rsqrt(ms + EPS) * (g * (1.0 + scale)) + shift


def _adaln_kernel(c_ref, w_ref, b_ref, o_ref):
    c = c_ref[...]
    o_ref[...] = _dot(c * _sigmoid(c), w_ref[...]) + b_ref[...]


def _adaln(c_all, w_ada, b_ada):
    rows, d = c_all.shape
    n = w_ada.shape[1]
    tn = d
    return pl.pallas_call(
        _adaln_kernel,
        out_shape=jax.ShapeDtypeStruct((rows, n), F32),
        grid=(n // tn,),
        in_specs=[pl.BlockSpec((rows, d), lambda j: (0, 0)),
                  pl.BlockSpec((d, tn), lambda j: (0, j)),
                  pl.BlockSpec((1, tn), lambda j: (0, j))],
        out_specs=pl.BlockSpec((rows, tn), lambda j: (0, j)),
        compiler_params=_cparams(1),
        name="adaln",
    )(c_all, w_ada, b_ada.reshape(1, n))


def _ffn_kernel(*refs, nb, tl, sub, with_mix, with_final):
    if with_mix:
        (x_ref, hm_ref, att_ref, mod_ref, gn_ref, wout_ref, wup_ref, wdn_ref, gfin_ref, o_ref, hb_ref,
         act_ref, mixb_ref) = refs
    else:
        x_ref, mod_ref, gn_ref, wup_ref, wdn_ref, gfin_ref, o_ref, hb_ref, act_ref = refs
    d = x_ref.shape[-1]
    dff = wdn_ref.shape[0]
    n_chunks = dff // FF_CHUNK
    if nb > 1:
        snb, stl = nb // 2, tl
        subs = [(slice(i * snb, (i + 1) * snb), slice(None)) for i in range(2)]
    else:
        snb, stl = nb, tl // 2
        subs = [(slice(None), slice(i * stl, (i + 1) * stl)) for i in range(2)]
    m = snb * stl
    sr = m // FFN_SLICES
    res_ref = o_ref if with_mix else x_ref

    def mods(b_sel):
        mod = mod_ref[b_sel]
        return [mod[:, i:i + 1, :] for i in range(N_MOD)]

    def mix_product(s):
        bs, ts = subs[s]
        mix = _dot(hm_ref[bs, ts].reshape(m, D_M), wout_ref[0:D_M, :])
        return mix + _dot(att_ref[bs, ts].reshape(m, D_A), wout_ref[D_M:, :])

    def prep_rows(s, b_sel, rows, flat, mix, zero):
        mod = mods(b_sel)
        res = x_ref[b_sel, rows]
        if with_mix:
            res = res + mod[5] * mix
            o_ref[b_sel, rows] = res
        xin = res if zero is None else res + zero
        h = _norm_mod(xin, gn_ref[...], mod[3 * sub], mod[3 * sub + 1])
        hb_ref[s, flat] = h.reshape(-1, d).astype(BF16)

    def up(s, c):
        lo = c * FF_CHUNK
        g = _dot(hb_ref[s], wup_ref[:, lo:lo + FF_CHUNK])
        u = _dot(hb_ref[s], wup_ref[:, dff + lo:dff + lo + FF_CHUNK])
        act_ref[s, :, lo:lo + FF_CHUNK] = (g * _sigmoid(g) * u).astype(BF16)
        return g

    def finish(s):
        bs, ts = subs[s]
        dn = _dot(act_ref[s], wdn_ref[...])
        y = res_ref[bs, ts] + 0.5 * mods(bs)[3 * sub + 2] * dn.reshape(snb, stl, d)
        if with_final:
            ms = jnp.mean(y * y, axis=-1, keepdims=True)
            y = y * lax.rsqrt(ms + EPS) * gfin_ref[...]
        o_ref[bs, ts] = y

    bs_a, ts_a = subs[0]
    prep_rows(0, bs_a, ts_a, slice(None), mix_product(0).reshape(snb, stl, d) if with_mix else None,
              None)
    if with_mix:
        mixb_ref[...] = mix_product(1)
    for c in range(n_chunks):
        g = up(0, c)
        if c < FFN_SLICES:
            bits = pltpu.bitcast(g[0:SUBLANES, 0:LANES], jnp.uint32)
            zero = pltpu.bitcast((bits >> 16) >> 16, F32)[0:1, :]
            zero = jnp.concatenate([zero] * (d // LANES), axis=1)
            flat = slice(c * sr, (c + 1) * sr)
            if nb > 1:
                b0 = snb + (c * sr) // stl
                b_sel, rows = slice(b0, b0 + 1), slice((c * sr) % stl, (c * sr) % stl + sr)
            else:
                b_sel, rows = slice(None), slice(stl + c * sr, stl + (c + 1) * sr)
            mix = mixb_ref[flat].reshape(1, sr, d) if with_mix else None
            prep_rows(1, b_sel, rows, flat, mix, zero)
    finish(0)
    for c in range(n_chunks):
        up(1, c)
    finish(1)


def _ffn(x, mod, g_norm_row, w_up, w_dn, g_final, *, nb, tl, sub, mix=None, final=False):
    NB, L, d = x.shape
    dff = w_dn.shape[0]
    m = nb * tl
    if nb > 1:
        nb = 2 * nb
        assert NB % nb == 0
    else:
        tl = 2 * tl
        assert L % tl == 0
    assert m % (16 * FFN_SLICES) == 0
    grid = (NB // nb, L // tl)
    row = lambda w: pl.BlockSpec((nb, tl, w), lambda i, j: (i, j, 0))
    in_specs = [row(d)]
    args = [x]
    scratch = [pltpu.VMEM((2, m, d), BF16), pltpu.VMEM((2, m, dff), BF16)]
    if mix is not None:
        scratch.append(pltpu.VMEM((m, d), F32))
    if mix is not None:
        hm, att, w_out = mix
        in_specs += [row(D_M), row(D_A)]
        args += [hm, att]
    in_specs += [pl.BlockSpec((nb, N_MOD, d), lambda i, j: (i, 0, 0)), _const_spec((1, d))]
    args += [mod, g_norm_row]
    if mix is not None:
        in_specs.append(_const_spec(w_out.shape))
        args.append(w_out)
    in_specs += [_const_spec(w_up.shape), _const_spec(w_dn.shape), _const_spec((1, d))]
    args += [w_up, w_dn, g_final]
    kern = functools.partial(_ffn_kernel, nb=nb, tl=tl, sub=sub, with_mix=mix is not None,
                             with_final=final)
    return pl.pallas_call(
        kern,
        out_shape=jax.ShapeDtypeStruct((NB, L, d), F32),
        grid=grid,
        in_specs=in_specs,
        out_specs=row(d),
        scratch_shapes=scratch,
        compiler_params=_cparams(2),
        name="ffn%d" % sub,
    )(*args)


def _proj_kernel(x_ref, mod_ref, gn_ref, w_ref, wg_ref, bg_ref, cw_ref, cb_ref, cprev_ref,
                 q_ref, k_ref, v_ref, o_ref, gates_ref, aq_ref, ak_ref, av_ref,
                 kt_ref, vt_ref, cnew_ref, hb_ref, ext_ref, *, nb, tl):
    m = nb * tl
    d = x_ref.shape[-1]
    j = pl.program_id(1)
    mod = mod_ref[...]
    h = _norm_mod(x_ref[...], gn_ref[...], mod[:, 3:4, :], mod[:, 4:5, :])
    hb_ref[...] = h.reshape(m, d).astype(BF16)

    @pl.when(j == 0)
    def _():
        ext_ref[:, 0:SUBLANES, :] = jnp.zeros((nb, SUBLANES, 2 * D_M), F32)
        ext_ref[:, SUBLANES - (CONV_W - 1):SUBLANES, :] = cprev_ref[...]

    sr = m // CONV_SLICES

    def conv_slice(c, s, zero):
        lo = c * FF_CHUNK
        if nb > 1:
            b0, r0 = (s * sr) // tl, (s * sr) % tl
            b_sel = slice(b0, b0 + 1)
        else:
            b_sel, r0 = slice(None), s * sr
        acc = cb_ref[:, lo:lo + FF_CHUNK] + zero
        for t in reversed(range(CONV_W)):
            s0 = SUBLANES - (CONV_W - 1) + t + r0
            acc = acc + cw_ref[t:t + 1, lo:lo + FF_CHUNK] * ext_ref[b_sel, s0:s0 + sr, lo:lo + FF_CHUNK]
        y = acc * _sigmoid(acc)
        if lo < D_M:
            q_ref[b_sel, r0:r0 + sr, lo:lo + FF_CHUNK] = y.astype(BF16)
        else:
            k_ref[b_sel, r0:r0 + sr, lo - D_M:lo - D_M + FF_CHUNK] = (y * DH_M ** -0.5).astype(BF16)

    conv_units = [(c, s) for c in range(2 * D_M // FF_CHUNK) for s in range(CONV_SLICES)]

    def tie_conv(result, n):
        for i in range(n):
            if conv_units:
                flat = result.reshape(m, result.shape[-1])
                r0 = (i * m // n) // SUBLANES * SUBLANES
                bits = pltpu.bitcast(flat[r0:r0 + SUBLANES, 0:LANES], jnp.uint32)
                zero = pltpu.bitcast((bits >> 16) >> 16, F32)[0:1, :]
                zero = jnp.concatenate([zero] * (FF_CHUNK // LANES), axis=1)
                conv_slice(*conv_units.pop(0), zero)

    def product(c0, width=FF_CHUNK):
        return _dot(hb_ref[...], w_ref[:, c0:c0 + width]).reshape(nb, tl, width)

    n_qk = 2 * D_M // FF_CHUNK
    n_products = (n_qk - 1) + 5 * (D_M // FF_CHUNK) + 1
    per_product = -(-len(conv_units) // n_products)
    assert per_product <= CONV_SLICES
    for c in range(n_qk):
        lo = c * FF_CHUNK
        r = product(lo)
        ext_ref[:, SUBLANES:SUBLANES + tl, lo:lo + FF_CHUNK] = r
        if c > 0:
            tie_conv(r, per_product)
    for half in range(D_M // FF_CHUNK):
        cs = slice(half * FF_CHUNK, (half + 1) * FF_CHUNK)
        r = product(2 * D_M + half * FF_CHUNK)
        v_ref[:, :, cs] = r.astype(BF16)
        tie_conv(r, per_product)
        r = product(3 * D_M + half * FF_CHUNK)
        o_ref[:, :, cs] = r
        tie_conv(r, per_product)
        r = product(4 * D_M + half * FF_CHUNK)
        aq_ref[:, :, cs] = (r * (DH_A ** -0.5 * LOG2E)).astype(BF16)
        tie_conv(r, per_product)
        r = product(4 * D_M + D_A + half * FF_CHUNK)
        ak_ref[:, :, cs] = r.astype(BF16)
        kt_ref[:, :, cs] = r
        tie_conv(r, per_product)
        r = product(4 * D_M + 2 * D_A + half * FF_CHUNK)
        av_ref[:, :, cs] = r.astype(BF16)
        vt_ref[:, :, cs] = r
        tie_conv(r, per_product)
    r = (_dot(hb_ref[...], wg_ref[...]) + bg_ref[...]).reshape(nb, tl, LANES)
    gates_ref[...] = r
    tie_conv(r, len(conv_units))

    cnew_ref[...] = ext_ref[:, SUBLANES + tl - (CONV_W - 1):SUBLANES + tl, :]
    ext_ref[:, 0:SUBLANES, :] = ext_ref[:, tl:tl + SUBLANES, :]


def _proj(x, mod, g_norm_row, w_main, w_gates, b_gates, conv_w, conv_b, conv_prev, *, nb, tl):
    NB, L, d = x.shape
    assert tl == min(ATT_WINDOW, L)
    m = nb * tl
    grid = (NB // nb, L // tl)
    row = lambda w: pl.BlockSpec((nb, tl, w), lambda i, j: (i, j, 0))
    per_b = lambda r, w: pl.BlockSpec((nb, r, w), lambda i, j: (i, 0, 0))
    bshape = lambda w, dt: jax.ShapeDtypeStruct((NB, L, w), dt)
    in_specs = [row(d), per_b(N_MOD, d), _const_spec((1, d)), _const_spec(w_main.shape),
                _const_spec(w_gates.shape), _const_spec((1, LANES)), _const_spec(conv_w.shape),
                _const_spec((1, 2 * D_M)), per_b(CONV_W - 1, 2 * D_M)]
    out_shape = [bshape(D_M, BF16), bshape(D_M, BF16), bshape(D_M, BF16), bshape(D_M, F32),
                 bshape(LANES, F32), bshape(D_A, BF16), bshape(D_A, BF16), bshape(D_A, BF16),
                 jax.ShapeDtypeStruct((NB, tl, D_A), F32), jax.ShapeDtypeStruct((NB, tl, D_A), F32),
                 jax.ShapeDtypeStruct((NB, CONV_W - 1, 2 * D_M), F32)]
    out_specs = [row(D_M), row(D_M), row(D_M), row(D_M), row(LANES), row(D_A), row(D_A), row(D_A),
                 per_b(tl, D_A), per_b(tl, D_A), per_b(CONV_W - 1, 2 * D_M)]
    return pl.pallas_call(
        functools.partial(_proj_kernel, nb=nb, tl=tl),
        out_shape=out_shape,
        grid=grid,
        in_specs=in_specs,
        out_specs=out_specs,
        scratch_shapes=[pltpu.VMEM((m, d), BF16), pltpu.VMEM((nb, SUBLANES + tl, 2 * D_M), F32)],
        compiler_params=_cparams(2),
        name="proj",
    )(x, mod, g_norm_row, w_main, w_gates, b_gates, conv_w, conv_b, conv_prev)


def _split_bf16(x):
    hi = x.astype(BF16)
    r = x - hi.astype(F32)
    mid = r.astype(BF16)
    return hi, mid, (r - mid.astype(F32)).astype(BF16)


def _mlstm_kernel(q_ref, k_ref, v_ref, gates_ref, mo_ref, c0_ref, n0_ref, m0_ref, gm_ref,
                  hm_ref, c_ref, n_ref, m_ref, ct_ref, *, nbm, lc, nck, single_chunk):
    j = pl.program_id(1)

    def transpose_states(src_ref, dst_ref):
        def one(i, carry):
            dst_ref[i // NH_M, i % NH_M] = jnp.transpose(src_ref[i // NH_M, i % NH_M])
            return carry
        lax.fori_loop(0, nbm * NH_M, one, 0)

    if single_chunk:
        n_ref[...] = n0_ref[...]
        m_ref[...] = m0_ref[...]
        ct_get = lambda b, h: jnp.transpose(c0_ref[b, h])

        def ct_put(b, h, val):
            c_ref[b, h] = jnp.transpose(val)
    else:
        @pl.when(j == 0)
        def _():
            transpose_states(c0_ref, ct_ref)
            n_ref[...] = n0_ref[...]
            m_ref[...] = m0_ref[...]
        ct_get = lambda b, h: ct_ref[b, h]

        def ct_put(b, h, val):
            ct_ref[b, h] = val

    s_i = lax.broadcasted_iota(jnp.int32, (lc, lc), 0)
    t_i = lax.broadcasted_iota(jnp.int32, (lc, lc), 1)
    causal = s_i <= t_i
    upper = jnp.where(causal, 1.0, 0.0).astype(BF16)
    head_row = lax.broadcasted_iota(jnp.int32, (SUBLANES, 1), 0)

    pairs = [(b, h) for b in range(nbm) for h in range(NH_M)]
    units = [(ck, b) for ck in range(nck) for b in range(nbm)]
    hsl = lambda h: slice(h * DH_M, (h + 1) * DH_M)
    tok = lambda ck: slice(ck * lc, (ck + 1) * lc)

    gate = {}
    for ck, b in units:
        gates_t = jnp.transpose(gates_ref[b, tok(ck)])
        ig_t = gates_t[0:SUBLANES]
        lf_t = _log_sigmoid(gates_t[SUBLANES:2 * SUBLANES])
        b_t = sum(_dot(part, upper) for part in _split_bf16(lf_t))
        c_t = ig_t - b_t
        c2_t = c_t * LOG2E
        c2_cols = jnp.transpose(jnp.concatenate([c2_t, jnp.zeros((LANES - SUBLANES, lc), F32)], axis=0))
        c2_masked = [jnp.where(causal, c2_cols[:, h:h + 1], NEG) for h in range(NH_M)]
        cmax2_t = jnp.concatenate(
            [jnp.max(cm, axis=0, keepdims=True) for cm in c2_masked]
            + [jnp.zeros((SUBLANES - NH_M, lc), F32)], axis=0)
        gate[ck, b] = dict(b_t=b_t, c_t=c_t, c2_masked=c2_masked, cmax2=cmax2_t,
                           m_intra=b_t + cmax2_t * (1.0 / LOG2E))

    early = {}
    for ck, b in units:
        for h in range(NH_M):
            early[ck, b, h] = dict(
                s_t=_dot_nt(k_ref[b, tok(ck), hsl(h)], q_ref[b, tok(ck), hsl(h)]),
                v_t=jnp.transpose(v_ref[b, tok(ck), hsl(h)].astype(F32)))

    for ck in range(nck):
        rows = []
        for b in range(nbm):
            g = gate[ck, b]
            b_t, c_t = g["b_t"], g["c_t"]
            b_end = b_t[:, lc - 1:lc]
            m_prev = m_ref[b][:, 0:1]
            log_g = b_end + c_t
            m_end = jnp.maximum(b_end + m_prev, jnp.max(log_g, axis=-1, keepdims=True))
            g_rows = jnp.exp(log_g - m_end)
            log_inter = b_t + m_prev
            m_tok = jnp.maximum(log_inter, g["m_intra"])
            rows.append(dict(decay=jnp.exp(b_end + m_prev - m_end), g=g_rows,
                             g_b=g_rows.astype(BF16), m_end=m_end,
                             r_intra=jnp.exp(g["m_intra"] - m_tok),
                             w_inter=jnp.exp(log_inter - m_tok), floor=jnp.exp(-m_tok),
                             n_prev_b=n_ref[b].astype(BF16)))

        for b, h in pairs:
            q = q_ref[b, tok(ck), hsl(h)]
            ct_prev = ct_get(b, h)
            early[ck, b, h].update(
                ct=ct_prev if single_chunk else None,
                cq_t=_dot_nt(ct_prev.astype(BF16), q),
                nq=_dot_nt(rows[b]["n_prev_b"], q)[h:h + 1, :])

        n_upd = [jnp.zeros((SUBLANES, DH_M), F32) for _ in range(nbm)]
        for b, h in pairs:
            r, g, e = rows[b], gate[ck, b], early[ck, b, h]
            k = k_ref[b, tok(ck), hsl(h)]
            a_t = jnp.exp2(g["c2_masked"][h] - g["cmax2"][h:h + 1, :]) * e["s_t"]
            r_intra, w_inter = r["r_intra"][h:h + 1, :], r["w_inter"][h:h + 1, :]
            num_t = _dot(e["v_t"].astype(BF16), a_t.astype(BF16)) * r_intra + e["cq_t"] * w_inter
            den = jnp.sum(a_t, axis=0, keepdims=True) * r_intra + e["nq"] * w_inter
            h_t = num_t * (1.0 / jnp.maximum(jnp.abs(den), r["floor"][h:h + 1, :]))
            hn_t = h_t * lax.rsqrt(jnp.mean(h_t * h_t, axis=0, keepdims=True) + EPS)
            hn = jnp.transpose(hn_t) * gm_ref[:, hsl(h)]
            hm_ref[b, tok(ck), hsl(h)] = (_sigmoid(mo_ref[b, tok(ck), hsl(h)]) * hn).astype(BF16)

            vg_t = (e["v_t"] * r["g"][h:h + 1, :]).astype(BF16)
            ct_prev = e["ct"] if single_chunk else ct_get(b, h)
            ct_put(b, h, r["decay"][h:h + 1, :] * ct_prev + _dot(vg_t, k))
            n_upd[b] = n_upd[b] + jnp.where(head_row == h, _dot(r["g_b"], k), 0.0)

        for b in range(nbm):
            n_ref[b] = rows[b]["decay"] * n_ref[b] + n_upd[b]
            m_ref[b] = jnp.broadcast_to(rows[b]["m_end"], (SUBLANES, LANES))

    if not single_chunk:
        @pl.when(j == pl.num_programs(1) - 1)
        def _():
            transpose_states(ct_ref, c_ref)


def _mlstm(q, k, v, gates, mo, c0, n0, m0, g_mlstm_row, *, nbm, lc):
    NB, L, _ = q.shape
    nck = MLSTM_CHUNKS_PER_STEP if L % (MLSTM_CHUNKS_PER_STEP * lc) == 0 else 1
    grid = (NB // nbm, L // (nck * lc))
    row = lambda w: pl.BlockSpec((nbm, nck * lc, w), lambda i, j: (i, j, 0))
    st = lambda s: pl.BlockSpec((nbm,) + s, lambda i, j: (i,) + (0,) * len(s))
    c_s, n_s = (NH_M, DH_M, DH_M), (SUBLANES, DH_M)
    return pl.pallas_call(
        functools.partial(_mlstm_kernel, nbm=nbm, lc=lc, nck=nck, single_chunk=L == lc),
        out_shape=[jax.ShapeDtypeStruct((NB, L, D_M), BF16),
                   jax.ShapeDtypeStruct((NB,) + c_s, F32),
                   jax.ShapeDtypeStruct((NB,) + n_s, F32),
                   jax.ShapeDtypeStruct((NB,) + n_s, F32)],
        grid=grid,
        in_specs=[row(D_M), row(D_M), row(D_M), row(LANES), row(D_M), st(c_s), st(n_s), st(n_s),
                  _const_spec((1, D_M))],
        out_specs=[row(D_M), st(c_s), st(n_s), st(n_s)],
        scratch_shapes=[pltpu.VMEM((nbm,) + c_s, F32)],
        compiler_params=_cparams(2),
        name="mlstm",
    )(q, k, v, gates, mo, c0, n0, m0, g_mlstm_row)


def _build_band_bias(ext_ref, tq, klen, put):
    p_len = ext_ref.shape[-1]
    q_pos = lax.broadcasted_iota(jnp.int32, (tq, klen), 0) + (klen - tq)
    k_pos = lax.broadcasted_iota(jnp.int32, (tq, klen), 1)
    back = q_pos // CHUNK - k_pos // CHUNK
    for h in range(NH_A):
        base = jnp.broadcast_to(ext_ref[h:h + 1, :] * LOG2E, (tq, p_len))
        toeplitz = pltpu.roll(base, 0, 1, stride=1, stride_axis=0)[:, :klen]
        put(h, jnp.where(back >= 0, jnp.where(back <= LEFT_CHUNKS, toeplitz, NEG), NEG))


def _lane_tiles(s):
    n = s.shape[-1]
    if n % LANES:
        return [s]
    return [s[:, i * LANES:(i + 1) * LANES] for i in range(n // LANES)]


def _row_reduce(tiles, op, lane_op):
    by_width = {}
    for t in tiles:
        by_width[t.shape[-1]] = t if t.shape[-1] not in by_width else op(by_width[t.shape[-1]], t)
    return functools.reduce(op, [lane_op(t, axis=-1, keepdims=True) for t in by_width.values()])


def _softmax_pv(ss, vs):
    mx = _row_reduce([t for s in ss for t in _lane_tiles(s)], jnp.maximum, jnp.max)
    es = [jnp.exp2(s - mx) for s in ss]
    l = _row_reduce([t for e in es for t in _lane_tiles(e)], jnp.add, jnp.sum)
    o = functools.reduce(jnp.add, [_dot(e.astype(BF16), v) for e, v in zip(es, vs)])
    return o / l


def _attn_pairs(jobs):
    first = lax.broadcasted_iota(jnp.int32, (1, LANES), 1) < DH_A
    zero = jnp.zeros((), BF16)

    def scores(q, segs, p):
        sl = slice(p * LANES, (p + 1) * LANES)
        qp = q(sl)
        q2 = jnp.concatenate([jnp.where(first, qp, zero), jnp.where(first, zero, qp)], axis=0)
        ss = []
        for k, _, bias, pen in segs:
            s = _dot_nt(q2, k(sl)) + bias(p)
            ss.append(s if pen is None else s + pen)
        return ss

    def finish(ss, segs, put, p):
        sl = slice(p * LANES, (p + 1) * LANES)
        r = ss[0].shape[0] // 2
        o2 = _softmax_pv(ss, [v(sl) for _, v, _, _ in segs])
        put(sl, jnp.where(first, o2[:r], o2[r:]))

    pending = []
    for q, segs, put in jobs:
        for p in range(NH_A // 2):
            pending.append((scores(q, segs, p), segs, put, p))
            if len(pending) > PAIR_ATT_LOOKAHEAD:
                finish(*pending.pop(0))
    while pending:
        finish(*pending.pop(0))


def _attn_prompt_kernel(q_ref, kp_ref, ko_ref, vp_ref, vo_ref, ext_ref, o_ref, bias_ref):
    tq = q_ref.shape[1]
    j = pl.program_id(1)

    @pl.when((pl.program_id(0) == 0) & (j == 0))
    def _():
        def put_bias(h, tile):
            bias_ref[h // 2, (h % 2) * ATT_SUB:(h % 2 + 1) * ATT_SUB, :] = tile
        _build_band_bias(ext_ref, ATT_SUB, tq + ATT_SUB, put_bias)

    def run(pen_prev):
        jobs = []
        for i in range(tq // ATT_SUB):
            lo = i * ATT_SUB
            n_prev = tq - lo
            bias = lambda a, b: (lambda p: bias_ref[p, :, a:b])
            rows = lambda ref, a, b: (lambda sl: ref[0, a:b, sl])

            def put(sl, val, lo=lo):
                o_ref[0, lo:lo + ATT_SUB, sl] = val.astype(o_ref.dtype)

            segs = [(rows(kp_ref, lo, tq), rows(vp_ref, lo, tq), bias(0, n_prev), pen_prev),
                    (rows(ko_ref, 0, lo + ATT_SUB), rows(vo_ref, 0, lo + ATT_SUB),
                     bias(n_prev, tq + ATT_SUB), None)]
            jobs.append((rows(q_ref, lo, lo + ATT_SUB), segs, put))
        _attn_pairs(jobs)

    @pl.when(j >= 1)
    def _():
        run(None)

    @pl.when(j < 1)
    def _():
        run(jnp.full((), NEG, F32))


def _attn_prompt(aq, ak, av, ext):
    NB, L, _ = aq.shape
    tq = ATT_WINDOW
    assert L % tq == 0
    grid = (NB, L // tq)
    row = pl.BlockSpec((1, tq, D_A), lambda b, j: (b, j, 0))
    prev = pl.BlockSpec((1, tq, D_A), lambda b, j: (b, jnp.maximum(j - 1, 0), 0))
    return pl.pallas_call(
        _attn_prompt_kernel,
        out_shape=jax.ShapeDtypeStruct((NB, L, D_A), BF16),
        grid=grid,
        in_specs=[row, prev, row, prev, row, _const_spec(ext.shape)],
        out_specs=row,
        scratch_shapes=[pltpu.VMEM((NH_A // 2, 2 * ATT_SUB, tq + ATT_SUB), F32)],
        compiler_params=_cparams(2),
        name="attn_prompt",
    )(aq, ak, ak, av, av, ext)


def _attn_sample_kernel(q_ref, k_ref, v_ref, ckt_ref, cvt_ref, ext_ref, o_ref, bias_ref, *, nb):
    tq = q_ref.shape[1]
    nc = ckt_ref.shape[-1]

    @pl.when(pl.program_id(0) == 0)
    def _():
        def put_bias(h, tile):
            bias_ref[h] = tile
        _build_band_bias(ext_ref, tq, nc + tq, put_bias)

    def scores(b, h):
        hs = slice(h * DH_A, (h + 1) * DH_A)
        qh = q_ref[b, :, hs]
        return [_dot(qh, ckt_ref[b, h].astype(BF16)) + bias_ref[h, :, 0:nc],
                _dot_nt(qh, k_ref[b, :, hs]) + bias_ref[h, :, nc:nc + tq]]

    def finish(b, h, ss):
        hs = slice(h * DH_A, (h + 1) * DH_A)
        mx = jnp.maximum(*[jnp.max(s, axis=-1, keepdims=True) for s in ss])
        e_c, e_o = [jnp.exp2(s - mx) for s in ss]
        l = jnp.sum(e_c, axis=-1, keepdims=True) + jnp.sum(e_o, axis=-1, keepdims=True)
        o = (_dot_nt(e_c.astype(BF16), cvt_ref[b, h].astype(BF16))
             + _dot(e_o.astype(BF16), v_ref[b, :, hs]))
        return o / l

    outs = {b: [] for b in range(nb)}
    pending = []

    def retire():
        b, h, ss = pending.pop(0)
        outs[b].append(finish(b, h, ss))
        if h == NH_A - 1:
            o_ref[b] = jnp.concatenate(outs[b], axis=-1).astype(o_ref.dtype)

    for b in range(nb):
        for h in range(NH_A):
            pending.append((b, h, scores(b, h)))
            if len(pending) > SAMPLE_ATT_LOOKAHEAD:
                retire()
    while pending:
        retire()


def _attn_sample(aq, ak, av, cache_kt, cache_vt, ext, *, nb):
    NB, L, _ = aq.shape
    nc = cache_kt.shape[-1]
    row = pl.BlockSpec((nb, L, D_A), lambda i: (i, 0, 0))
    crow = pl.BlockSpec((nb, NH_A, DH_A, nc), lambda i: (i, 0, 0, 0))
    return pl.pallas_call(
        functools.partial(_attn_sample_kernel, nb=nb),
        out_shape=jax.ShapeDtypeStruct((NB, L, D_A), BF16),
        grid=(NB // nb,),
        in_specs=[row, row, row, crow, crow, _const_spec(ext.shape)],
        out_specs=row,
        scratch_shapes=[pltpu.VMEM((NH_A, L, nc + L), F32)],
        compiler_params=_cparams(1),
        name="attn_sample",
    )(aq, ak, av, cache_kt, cache_vt, ext)


def _rel_ext(rel_table, tq, klen):
    p_len = -(-(klen + tq) // LANES) * LANES
    n_lo = tq - CHUNK
    n_hi = p_len - tq - MAX_REL
    rep = lambda col, n: jnp.broadcast_to(col, (NH_A, n))
    by_dist = jnp.concatenate([rep(rel_table[:, :1], n_lo), rel_table, rep(rel_table[:, -1:], n_hi)],
                              axis=1)
    rev = by_dist[:, ::-1]
    n = p_len - klen
    return jnp.concatenate([rev[:, n:], rev[:, :n]], axis=1).astype(F32)


def _gate_lanes(g):
    z = lambda n: jnp.zeros((g.shape[0], n), g.dtype)
    return jnp.concatenate([g[:, :NH_M], z(SUBLANES - NH_M), g[:, NH_M:], z(LANES - SUBLANES - NH_M)], axis=1)


def _layer(x, mod, conv_prev, state, cache, w, *, nb, tl, nbm, lc, final):
    NB, L, d = x.shape
    x1 = _ffn(x, mod, w["g0"], w["up1"], w["dn1"], w["g_final"], nb=nb, tl=tl, sub=0)
    (q, k, v, mo, gates, aq, ak, av, k_tail, v_tail, conv_new) = _proj(
        x1, mod, w["g1"], w["w_main"], w["w_gates"], w["b_gates"], w["conv_w"], w["conv_b"],
        conv_prev, nb=nb, tl=tl)
    c0, n0, m0 = state
    n0 = jnp.pad(n0, ((0, 0), (0, SUBLANES - NH_M), (0, 0)))
    m0 = jnp.broadcast_to(jnp.pad(m0, ((0, 0), (0, SUBLANES - NH_M)))[:, :, None], (NB, SUBLANES, LANES))
    hm, c_new, n_new, m_new = _mlstm(q, k, v, gates, mo, c0, n0, m0, w["g_mlstm"], nbm=nbm, lc=lc)
    if cache is None:
        att = _attn_prompt(aq, ak, av, w["ext_prompt"])
    else:
        att = _attn_sample(aq, ak, av, cache[0], cache[1], w["ext_sample"], nb=min(NB, 4))
    y = _ffn(x1, mod, w["g2"], w["up2"], w["dn2"], w["g_final"], nb=nb, tl=tl, sub=2,
             mix=(hm, att, w["w_out"]), final=final)
    n_keep = k_tail.shape[1]
    states = (c_new, n_new[:, :NH_M, :], m_new[:, :NH_M, 0], conv_new,
              k_tail.reshape(NB, n_keep, NH_A, DH_A), v_tail.reshape(NB, n_keep, NH_A, DH_A))
    return y, states


def kernel(x_prompt, x_sample, state_mlstm_C, state_mlstm_n, state_mlstm_m, state_conv, cache_att_k, cache_att_v, c_prompt, c_sample, w_ada, b_ada, g_norm, w_ffn1_up, w_ffn1_down, w_ffn2_up, w_ffn2_down, w_in, conv_w, conv_b, b_gates, g_mlstm, rel_bias_table, w_out, g_final):
    depth = w_ada.shape[0]
    bp, seq, d = x_prompt.shape
    bs, dseq, _ = x_sample.shape
    xp, xs = x_prompt, x_sample
    st_p, st_s = [], []
    n_c = cache_att_k.shape[2]
    tl_p = min(ATT_WINDOW, seq)
    lc_p = min(MLSTM_CHUNK, seq)
    nb_s = max(1, min(bs, ATT_WINDOW // dseq))
    for l in range(depth):
        rows = bp + bs
        pad = (-rows) % SUBLANES
        c_all = jnp.concatenate([c_prompt, c_sample, jnp.zeros((pad, d), F32)], axis=0)
        mod = _adaln(c_all, w_ada[l], b_ada[l])
        mod_p = mod[:bp].reshape(bp, N_MOD, d)
        mod_s = mod[bp:rows].reshape(bs, N_MOD, d)
        off_g = 4 * D_M
        off_a = off_g + 2 * NH_M
        wl = w_in[l]
        rel = rel_bias_table[l]
        w = {
            "g0": g_norm[l, 0:1], "g1": g_norm[l, 1:2], "g2": g_norm[l, 2:3],
            "g_final": g_final.reshape(1, d),
            "up1": w_ffn1_up[l].astype(BF16), "dn1": w_ffn1_down[l].astype(BF16),
            "up2": w_ffn2_up[l].astype(BF16), "dn2": w_ffn2_down[l].astype(BF16),
            "w_main": jnp.concatenate([wl[:, :off_g], wl[:, off_a:]], axis=1).astype(BF16),
            "w_gates": _gate_lanes(wl[:, off_g:off_a]).astype(BF16),
            "b_gates": _gate_lanes(b_gates[l].reshape(1, 2 * NH_M)),
            "conv_w": conv_w[l], "conv_b": conv_b[l].reshape(1, 2 * D_M),
            "g_mlstm": g_mlstm[l].reshape(1, D_M),
            "w_out": w_out[l].astype(BF16),
            "ext_prompt": _rel_ext(rel, ATT_SUB, ATT_WINDOW + ATT_SUB),
            "ext_sample": _rel_ext(rel, dseq, n_c + dseq),
        }
        zero_state = (jnp.zeros((bp, NH_M, DH_M, DH_M), F32), jnp.zeros((bp, NH_M, DH_M), F32),
                      jnp.zeros((bp, NH_M), F32))
        xp, sp = _layer(xp, mod_p, jnp.zeros((bp, CONV_W - 1, 2 * D_M), F32), zero_state, None, w,
                        nb=1, tl=tl_p, nbm=bp, lc=lc_p, final=l == depth - 1)
        state_s = (state_mlstm_C[l], state_mlstm_n[l], state_mlstm_m[l])
        cache = (jnp.transpose(cache_att_k[l], (0, 2, 3, 1)), jnp.transpose(cache_att_v[l], (0, 2, 3, 1)))
        xs, ss = _layer(xs, mod_s, state_conv[l], state_s, cache, w, nb=nb_s, tl=dseq,
                        nbm=min(bs, 4), lc=dseq, final=l == depth - 1)
        st_p.append(sp)
        st_s.append(ss)
    stk = lambda sts, i: jnp.stack([s[i] for s in sts])
    return ((xp, xs) + tuple(stk(st_p, i) for i in range(6)) + tuple(stk(st_s, i) for i in range(6)))
```

```python
import functools

import jax
import jax.numpy as jnp
from jax import lax
from jax.experimental import pallas as pl
from jax.experimental.pallas import tpu as pltpu

F32 = jnp.float32
BF16 = jnp.bfloat16

CHUNK = 64
NH_M = 4
DH_M = 128
D_M = NH_M * DH_M
NH_A = 8
DH_A = 64
D_A = NH_A * DH_A
CONV_W = 4
LEFT_CHUNKS = 8
ATT_WINDOW = LEFT_CHUNKS * CHUNK
MAX_REL = 2 * CHUNK
N_MOD = 9
EPS = 1e-6
NEG = -1e30
LOG2E = 1.4426950408889634

LANES = 128
SUBLANES = 8
V7X_VMEM_BYTES = 64 * 1024 * 1024
VMEM_LIMIT = V7X_VMEM_BYTES - 8 * 1024 * 1024

FF_CHUNK = 256
FFN_SLICES = 8
CONV_SLICES = 32
MLSTM_CHUNK = 256
MLSTM_CHUNKS_PER_STEP = 2
ATT_SUB = 2 * CHUNK
SAMPLE_ATT_LOOKAHEAD = 3
PAIR_ATT_LOOKAHEAD = 1


def _cparams(n_axes):
    return pltpu.CompilerParams(dimension_semantics=("arbitrary",) * n_axes,
                                vmem_limit_bytes=VMEM_LIMIT)


def _const_spec(shape):
    nd = len(shape)
    return pl.BlockSpec(shape, lambda *_: (0,) * nd, pipeline_mode=pl.Buffered(1))


def _dot(a, b):
    return jnp.dot(a, b, preferred_element_type=F32)


def _dot_nt(a, b):
    return lax.dot_general(a, b, (((1,), (1,)), ((), ())), preferred_element_type=F32)


def _sigmoid(x):
    return 1.0 / (1.0 + jnp.exp(-x))


def _log_sigmoid(x):
    return jnp.minimum(x, 0.0) - jnp.log(1.0 + jnp.exp(-jnp.abs(x)))


def _norm_mod(x, g, shift, scale):
    ms = jnp.mean(x * x, axis=-1, keepdims=True)
    return x * lax.rsqrt(ms + EPS) * (g * (1.0 + scale)) + shift


def _adaln_kernel(c_ref, w_ref, b_ref, o_ref):
    c = c_ref[...]
    o_ref[...] = _dot(c * _sigmoid(c), w_ref[...]) + b_ref[...]


def _adaln(c_all, w_ada, b_ada):
    rows, d = c_all.shape
    n = w_ada.shape[1]
    tn = d
    return pl.pallas_call(
        _adaln_kernel,
        out_shape=jax.ShapeDtypeStruct((rows, n), F32),
        grid=(n // tn,),
        in_specs=[pl.BlockSpec((rows, d), lambda j: (0, 0)),
                  pl.BlockSpec((d, tn), lambda j: (0, j)),
                  pl.BlockSpec((1, tn), lambda j: (0, j))],
        out_specs=pl.BlockSpec((rows, tn), lambda j: (0, j)),
        compiler_params=_cparams(1),
        name="adaln",
    )(c_all, w_ada, b_ada.reshape(1, n))


def _ffn_kernel(*refs, nb, tl, sub, with_mix, with_final):
    if with_mix:
        (x_ref, hm_ref, att_ref, mod_ref, gn_ref, wout_ref, wup_ref, wdn_ref, gfin_ref, o_ref, hb_ref,
         act_ref, mixb_ref) = refs
    else:
        x_ref, xn_ref, mod_ref, modn_ref, gn_ref, wup_ref, wdn_ref, gfin_ref, o_ref, hb_ref, act_ref = refs
    d = x_ref.shape[-1]
    dff = wdn_ref.shape[0]
    n_chunks = dff // FF_CHUNK
    if nb > 1:
        snb, stl = nb // 2, tl
        subs = [(slice(i * snb, (i + 1) * snb), slice(None)) for i in range(2)]
    else:
        snb, stl = nb, tl // 2
        subs = [(slice(None), slice(i * stl, (i + 1) * stl)) for i in range(2)]
    m = snb * stl
    sr = m // FFN_SLICES
    res_ref = o_ref if with_mix else x_ref

    def mods(b_sel):
        mod = mod_ref[b_sel]
        return [mod[:, i:i + 1, :] for i in range(N_MOD)]

    def mix_product(s):
        bs, ts = subs[s]
        mix = _dot(hm_ref[bs, ts].reshape(m, D_M), wout_ref[0:D_M, :])
        return mix + _dot(att_ref[bs, ts].reshape(m, D_A), wout_ref[D_M:, :])

    def prep_rows(s, b_sel, rows, flat, mix, zero, src=None):
        xs_ref, ms_ref = (x_ref, mod_ref) if src is None else src
        mod_rows = ms_ref[b_sel]
        mod = [mod_rows[:, i:i + 1, :] for i in range(N_MOD)]
        res = xs_ref[b_sel, rows]
        if with_mix:
            res = res + mod[5] * mix
            o_ref[b_sel, rows] = res
        xin = res if zero is None else res + zero
        h = _norm_mod(xin, gn_ref[...], mod[3 * sub], mod[3 * sub + 1])
        hb_ref[s, flat] = h.reshape(-1, d).astype(BF16)

    def tied_zero(product):
        bits = pltpu.bitcast(product[0:SUBLANES, 0:LANES], jnp.uint32)
        zero = pltpu.bitcast((bits >> 16) >> 16, F32)[0:1, :]
        return jnp.concatenate([zero] * (d // LANES), axis=1)

    def slice_index(s, i):
        flat = slice(i * sr, (i + 1) * sr)
        if nb > 1:
            b0 = s * snb + (i * sr) // stl
            return slice(b0, b0 + 1), slice((i * sr) % stl, (i * sr) % stl + sr), flat
        return slice(None), slice(s * stl + i * sr, s * stl + (i + 1) * sr), flat

    def up(s, c):
        lo = c * FF_CHUNK
        g = _dot(hb_ref[s], wup_ref[:, lo:lo + FF_CHUNK])
        u = _dot(hb_ref[s], wup_ref[:, dff + lo:dff + lo + FF_CHUNK])
        act_ref[s, :, lo:lo + FF_CHUNK] = (g * _sigmoid(g) * u).astype(BF16)
        return g

    def finish(s):
        bs, ts = subs[s]
        dn = _dot(act_ref[s], wdn_ref[...])
        y = res_ref[bs, ts] + 0.5 * mods(bs)[3 * sub + 2] * dn.reshape(snb, stl, d)
        if with_final:
            ms = jnp.mean(y * y, axis=-1, keepdims=True)
            y = y * lax.rsqrt(ms + EPS) * gfin_ref[...]
        o_ref[bs, ts] = y

    bs_a, ts_a = subs[0]
    if with_mix:
        prep_rows(0, bs_a, ts_a, slice(None), mix_product(0).reshape(snb, stl, d), None)
        mixb_ref[...] = mix_product(1)
    else:
        @pl.when((pl.program_id(0) == 0) & (pl.program_id(1) == 0))
        def _():
            prep_rows(0, bs_a, ts_a, slice(None), None, None)
    for c in range(n_chunks):
        g = up(0, c)
        if c < FFN_SLICES:
            b_sel, rows, flat = slice_index(1, c)
            mix = mixb_ref[flat].reshape(1, sr, d) if with_mix else None
            prep_rows(1, b_sel, rows, flat, mix, tied_zero(g))
    finish(0)
    for c in range(n_chunks):
        g = up(1, c)
        if not with_mix and c < FFN_SLICES:
            b_sel, rows, flat = slice_index(0, c)
            prep_rows(0, b_sel, rows, flat, None, tied_zero(g), src=(xn_ref, modn_ref))
    finish(1)


def _ffn(x, mod, g_norm_row, w_up, w_dn, g_final, *, nb, tl, sub, mix=None, final=False):
    NB, L, d = x.shape
    dff = w_dn.shape[0]
    m = nb * tl
    batch_split = nb > 1
    if batch_split:
        nb = 2 * nb
        assert NB % nb == 0
    else:
        tl = 2 * tl
        assert L % tl == 0
    assert m % (16 * FFN_SLICES) == 0
    ni, nj = NB // nb, L // tl
    grid = (ni, nj)
    row = lambda w: pl.BlockSpec((nb, tl, w), lambda i, j: (i, j, 0))
    mod_spec = pl.BlockSpec((nb, N_MOD, d), lambda i, j: (i, 0, 0))
    in_specs = [row(d)]
    args = [x]
    scratch = [pltpu.VMEM((2, m, d), BF16), pltpu.VMEM((2, m, dff), BF16)]
    if mix is not None:
        scratch.append(pltpu.VMEM((m, d), F32))
        hm, att, w_out = mix
        in_specs += [row(D_M), row(D_A), mod_spec]
        args += [hm, att, mod]
    else:
        def nxt(i, j):
            s = jnp.minimum(i * nj + j + 1, ni * nj - 1)
            return s // nj, s % nj
        if batch_split:
            sub_block = (nb // 2, tl)
            sub_index = lambda i, j: (2 * nxt(i, j)[0], nxt(i, j)[1])
        else:
            sub_block = (nb, tl // 2)
            sub_index = lambda i, j: (nxt(i, j)[0], 2 * nxt(i, j)[1])
        in_specs += [pl.BlockSpec(sub_block + (d,), lambda i, j: sub_index(i, j) + (0,)), mod_spec,
                     pl.BlockSpec((sub_block[0], N_MOD, d), lambda i, j: (sub_index(i, j)[0], 0, 0))]
        args += [x, mod, mod]
    in_specs.append(_const_spec((1, d)))
    args.append(g_norm_row)
    if mix is not None:
        in_specs.append(_const_spec(w_out.shape))
        args.append(w_out)
    in_specs += [_const_spec(w_up.shape), _const_spec(w_dn.shape), _const_spec((1, d))]
    args += [w_up, w_dn, g_final]
    kern = functools.partial(_ffn_kernel, nb=nb, tl=tl, sub=sub, with_mix=mix is not None,
                             with_final=final)
    return pl.pallas_call(
        kern,
        out_shape=jax.ShapeDtypeStruct((NB, L, d), F32),
        grid=grid,
        in_specs=in_specs,
        out_specs=row(d),
        scratch_shapes=scratch,
        compiler_params=_cparams(2),
        name="ffn%d" % sub,
    )(*args)


def _proj_kernel(x_ref, mod_ref, gn_ref, w_ref, wg_ref, bg_ref, cw_ref, cb_ref, cprev_ref,
                 q_ref, k_ref, v_ref, o_ref, gates_ref, aq_ref, ak_ref, av_ref,
                 kt_ref, vt_ref, cnew_ref, hb_ref, ext_ref, *, nb, tl):
    m = nb * tl
    d = x_ref.shape[-1]
    j = pl.program_id(1)
    mod = mod_ref[...]
    h = _norm_mod(x_ref[...], gn_ref[...], mod[:, 3:4, :], mod[:, 4:5, :])
    hb_ref[...] = h.reshape(m, d).astype(BF16)

    @pl.when(j == 0)
    def _():
        ext_ref[:, 0:SUBLANES, :] = jnp.zeros((nb, SUBLANES, 2 * D_M), F32)
        ext_ref[:, SUBLANES - (CONV_W - 1):SUBLANES, :] = cprev_ref[...]

    sr = m // CONV_SLICES

    def conv_slice(c, s, zero):
        lo = c * FF_CHUNK
        if nb > 1:
            b0, r0 = (s * sr) // tl, (s * sr) % tl
            b_sel = slice(b0, b0 + 1)
        else:
            b_sel, r0 = slice(None), s * sr
        acc = cb_ref[:, lo:lo + FF_CHUNK] + zero
        for t in reversed(range(CONV_W)):
            s0 = SUBLANES - (CONV_W - 1) + t + r0
            acc = acc + cw_ref[t:t + 1, lo:lo + FF_CHUNK] * ext_ref[b_sel, s0:s0 + sr, lo:lo + FF_CHUNK]
        y = acc * _sigmoid(acc)
        if lo < D_M:
            q_ref[b_sel, r0:r0 + sr, lo:lo + FF_CHUNK] = y.astype(BF16)
        else:
            k_ref[b_sel, r0:r0 + sr, lo - D_M:lo - D_M + FF_CHUNK] = (y * DH_M ** -0.5).astype(BF16)

    conv_units = [(c, s) for c in range(2 * D_M // FF_CHUNK) for s in range(CONV_SLICES)]

    def tie_conv(result, n):
        for i in range(n):
            if conv_units:
                flat = result.reshape(m, result.shape[-1])
                r0 = (i * m // n) // SUBLANES * SUBLANES
                bits = pltpu.bitcast(flat[r0:r0 + SUBLANES, 0:LANES], jnp.uint32)
                zero = pltpu.bitcast((bits >> 16) >> 16, F32)[0:1, :]
                zero = jnp.concatenate([zero] * (FF_CHUNK // LANES), axis=1)
                conv_slice(*conv_units.pop(0), zero)

    def product(c0, width=FF_CHUNK):
        return _dot(hb_ref[...], w_ref[:, c0:c0 + width]).reshape(nb, tl, width)

    n_qk = 2 * D_M // FF_CHUNK
    n_products = (n_qk - 1) + 5 * (D_M // FF_CHUNK) + 1
    per_product = -(-len(conv_units) // n_products)
    assert per_product <= CONV_SLICES
    for c in range(n_qk):
        lo = c * FF_CHUNK
        r = product(lo)
        ext_ref[:, SUBLANES:SUBLANES + tl, lo:lo + FF_CHUNK] = r
        if c > 0:
            tie_conv(r, per_product)
    for half in range(D_M // FF_CHUNK):
        cs = slice(half * FF_CHUNK, (half + 1) * FF_CHUNK)
        r = product(2 * D_M + half * FF_CHUNK)
        v_ref[:, :, cs] = r.astype(BF16)
        tie_conv(r, per_product)
        r = product(3 * D_M + half * FF_CHUNK)
        o_ref[:, :, cs] = r
        tie_conv(r, per_product)
        r = product(4 * D_M + half * FF_CHUNK)
        aq_ref[:, :, cs] = (r * (DH_A ** -0.5 * LOG2E)).astype(BF16)
        tie_conv(r, per_product)
        r = product(4 * D_M + D_A + half * FF_CHUNK)
        ak_ref[:, :, cs] = r.astype(BF16)
        kt_ref[:, :, cs] = r
        tie_conv(r, per_product)
        r = product(4 * D_M + 2 * D_A + half * FF_CHUNK)
        av_ref[:, :, cs] = r.astype(BF16)
        vt_ref[:, :, cs] = r
        tie_conv(r, per_product)
    r = (_dot(hb_ref[...], wg_ref[...]) + bg_ref[...]).reshape(nb, tl, LANES)
    gates_ref[...] = r
    tie_conv(r, len(conv_units))

    cnew_ref[...] = ext_ref[:, SUBLANES + tl - (CONV_W - 1):SUBLANES + tl, :]
    ext_ref[:, 0:SUBLANES, :] = ext_ref[:, tl:tl + SUBLANES, :]


def _proj(x, mod, g_norm_row, w_main, w_gates, b_gates, conv_w, conv_b, conv_prev, *, nb, tl):
    NB, L, d = x.shape
    assert tl == min(ATT_WINDOW, L)
    m = nb * tl
    grid = (NB // nb, L // tl)
    row = lambda w: pl.BlockSpec((nb, tl, w), lambda i, j: (i, j, 0))
    per_b = lambda r, w: pl.BlockSpec((nb, r, w), lambda i, j: (i, 0, 0))
    bshape = lambda w, dt: jax.ShapeDtypeStruct((NB, L, w), dt)
    in_specs = [row(d), per_b(N_MOD, d), _const_spec((1, d)), _const_spec(w_main.shape),
                _const_spec(w_gates.shape), _const_spec((1, LANES)), _const_spec(conv_w.shape),
                _const_spec((1, 2 * D_M)), per_b(CONV_W - 1, 2 * D_M)]
    out_shape = [bshape(D_M, BF16), bshape(D_M, BF16), bshape(D_M, BF16), bshape(D_M, F32),
                 bshape(LANES, F32), bshape(D_A, BF16), bshape(D_A, BF16), bshape(D_A, BF16),
                 jax.ShapeDtypeStruct((NB, tl, D_A), F32), jax.ShapeDtypeStruct((NB, tl, D_A), F32),
                 jax.ShapeDtypeStruct((NB, CONV_W - 1, 2 * D_M), F32)]
    out_specs = [row(D_M), row(D_M), row(D_M), row(D_M), row(LANES), row(D_A), row(D_A), row(D_A),
                 per_b(tl, D_A), per_b(tl, D_A), per_b(CONV_W - 1, 2 * D_M)]
    return pl.pallas_call(
        functools.partial(_proj_kernel, nb=nb, tl=tl),
        out_shape=out_shape,
        grid=grid,
        in_specs=in_specs,
        out_specs=out_specs,
        scratch_shapes=[pltpu.VMEM((m, d), BF16), pltpu.VMEM((nb, SUBLANES + tl, 2 * D_M), F32)],
        compiler_params=_cparams(2),
        name="proj",
    )(x, mod, g_norm_row, w_main, w_gates, b_gates, conv_w, conv_b, conv_prev)


def _split_bf16(x):
    hi = x.astype(BF16)
    r = x - hi.astype(F32)
    mid = r.astype(BF16)
    return hi, mid, (r - mid.astype(F32)).astype(BF16)


def _mlstm_kernel(q_ref, k_ref, v_ref, gates_ref, mo_ref, c0_ref, n0_ref, m0_ref, gm_ref,
                  hm_ref, c_ref, n_ref, m_ref, ct_ref, *, nbm, lc, nck, single_chunk):
    j = pl.program_id(1)

    def transpose_states(src_ref, dst_ref):
        def one(i, carry):
            dst_ref[i // NH_M, i % NH_M] = jnp.transpose(src_ref[i // NH_M, i % NH_M])
            return carry
        lax.fori_loop(0, nbm * NH_M, one, 0)

    if single_chunk:
        n_ref[...] = n0_ref[...]
        m_ref[...] = m0_ref[...]
        ct_get = lambda b, h: jnp.transpose(c0_ref[b, h])

        def ct_put(b, h, val):
            c_ref[b, h] = jnp.transpose(val)
    else:
        @pl.when(j == 0)
        def _():
            transpose_states(c0_ref, ct_ref)
            n_ref[...] = n0_ref[...]
            m_ref[...] = m0_ref[...]
        ct_get = lambda b, h: ct_ref[b, h]

        def ct_put(b, h, val):
            ct_ref[b, h] = val

    s_i = lax.broadcasted_iota(jnp.int32, (lc, lc), 0)
    t_i = lax.broadcasted_iota(jnp.int32, (lc, lc), 1)
    causal = s_i <= t_i
    upper = jnp.where(causal, 1.0, 0.0).astype(BF16)
    head_row = lax.broadcasted_iota(jnp.int32, (SUBLANES, 1), 0)

    pairs = [(b, h) for b in range(nbm) for h in range(NH_M)]
    units = [(ck, b) for ck in range(nck) for b in range(nbm)]
    hsl = lambda h: slice(h * DH_M, (h + 1) * DH_M)
    tok = lambda ck: slice(ck * lc, (ck + 1) * lc)

    gate = {}
    for ck, b in units:
        gates_t = jnp.transpose(gates_ref[b, tok(ck)])
        ig_t = gates_t[0:SUBLANES]
        lf_t = _log_sigmoid(gates_t[SUBLANES:2 * SUBLANES])
        b_t = sum(_dot(part, upper) for part in _split_bf16(lf_t))
        c_t = ig_t - b_t
        c2_t = c_t * LOG2E
        c2_cols = jnp.transpose(jnp.concatenate([c2_t, jnp.zeros((LANES - SUBLANES, lc), F32)], axis=0))
        c2_masked = [jnp.where(causal, c2_cols[:, h:h + 1], NEG) for h in range(NH_M)]
        cmax2_t = jnp.concatenate(
            [jnp.max(cm, axis=0, keepdims=True) for cm in c2_masked]
            + [jnp.zeros((SUBLANES - NH_M, lc), F32)], axis=0)
        gate[ck, b] = dict(b_t=b_t, c_t=c_t, c2_masked=c2_masked, cmax2=cmax2_t,
                           m_intra=b_t + cmax2_t * (1.0 / LOG2E))

    early = {}
    for ck, b in units:
        for h in range(NH_M):
            early[ck, b, h] = dict(
                s_t=_dot_nt(k_ref[b, tok(ck), hsl(h)], q_ref[b, tok(ck), hsl(h)]),
                v_t=jnp.transpose(v_ref[b, tok(ck), hsl(h)].astype(F32)))

    for ck in range(nck):
        rows = []
        for b in range(nbm):
            g = gate[ck, b]
            b_t, c_t = g["b_t"], g["c_t"]
            b_end = b_t[:, lc - 1:lc]
            m_prev = m_ref[b][:, 0:1]
            log_g = b_end + c_t
            m_end = jnp.maximum(b_end + m_prev, jnp.max(log_g, axis=-1, keepdims=True))
            g_rows = jnp.exp(log_g - m_end)
            log_inter = b_t + m_prev
            m_tok = jnp.maximum(log_inter, g["m_intra"])
            rows.append(dict(decay=jnp.exp(b_end + m_prev - m_end), g=g_rows,
                             g_b=g_rows.astype(BF16), m_end=m_end,
                             r_intra=jnp.exp(g["m_intra"] - m_tok),
                             w_inter=jnp.exp(log_inter - m_tok), floor=jnp.exp(-m_tok),
                             n_prev_b=n_ref[b].astype(BF16)))

        for b, h in pairs:
            q = q_ref[b, tok(ck), hsl(h)]
            ct_prev = ct_get(b, h)
            early[ck, b, h].update(
                ct=ct_prev if single_chunk else None,
                cq_t=_dot_nt(ct_prev.astype(BF16), q),
                nq=_dot_nt(rows[b]["n_prev_b"], q)[h:h + 1, :])

        n_upd = [jnp.zeros((SUBLANES, DH_M), F32) for _ in range(nbm)]
        for b, h in pairs:
            r, g, e = rows[b], gate[ck, b], early[ck, b, h]
            k = k_ref[b, tok(ck), hsl(h)]
            a_t = jnp.exp2(g["c2_masked"][h] - g["cmax2"][h:h + 1, :]) * e["s_t"]
            r_intra, w_inter = r["r_intra"][h:h + 1, :], r["w_inter"][h:h + 1, :]
            num_t = _dot(e["v_t"].astype(BF16), a_t.astype(BF16)) * r_intra + e["cq_t"] * w_inter
            den = jnp.sum(a_t, axis=0, keepdims=True) * r_intra + e["nq"] * w_inter
            h_t = num_t * (1.0 / jnp.maximum(jnp.abs(den), r["floor"][h:h + 1, :]))
            hn_t = h_t * lax.rsqrt(jnp.mean(h_t * h_t, axis=0, keepdims=True) + EPS)
            hn = jnp.transpose(hn_t) * gm_ref[:, hsl(h)]
            hm_ref[b, tok(ck), hsl(h)] = (_sigmoid(mo_ref[b, tok(ck), hsl(h)]) * hn).astype(BF16)

            vg_t = (e["v_t"] * r["g"][h:h + 1, :]).astype(BF16)
            ct_prev = e["ct"] if single_chunk else ct_get(b, h)
            ct_put(b, h, r["decay"][h:h + 1, :] * ct_prev + _dot(vg_t, k))
            n_upd[b] = n_upd[b] + jnp.where(head_row == h, _dot(r["g_b"], k), 0.0)

        for b in range(nbm):
            n_ref[b] = rows[b]["decay"] * n_ref[b] + n_upd[b]
            m_ref[b] = jnp.broadcast_to(rows[b]["m_end"], (SUBLANES, LANES))

    if not single_chunk:
        @pl.when(j == pl.num_programs(1) - 1)
        def _():
            transpose_states(ct_ref, c_ref)


def _mlstm(q, k, v, gates, mo, c0, n0, m0, g_mlstm_row, *, nbm, lc):
    NB, L, _ = q.shape
    nck = MLSTM_CHUNKS_PER_STEP if L % (MLSTM_CHUNKS_PER_STEP * lc) == 0 else 1
    grid = (NB // nbm, L // (nck * lc))
    row = lambda w: pl.BlockSpec((nbm, nck * lc, w), lambda i, j: (i, j, 0))
    st = lambda s: pl.BlockSpec((nbm,) + s, lambda i, j: (i,) + (0,) * len(s))
    c_s, n_s = (NH_M, DH_M, DH_M), (SUBLANES, DH_M)
    return pl.pallas_call(
        functools.partial(_mlstm_kernel, nbm=nbm, lc=lc, nck=nck, single_chunk=L == lc),
        out_shape=[jax.ShapeDtypeStruct((NB, L, D_M), BF16),
                   jax.ShapeDtypeStruct((NB,) + c_s, F32),
                   jax.ShapeDtypeStruct((NB,) + n_s, F32),
                   jax.ShapeDtypeStruct((NB,) + n_s, F32)],
        grid=grid,
        in_specs=[row(D_M), row(D_M), row(D_M), row(LANES), row(D_M), st(c_s), st(n_s), st(n_s),
                  _const_spec((1, D_M))],
        out_specs=[row(D_M), st(c_s), st(n_s), st(n_s)],
        scratch_shapes=[pltpu.VMEM((nbm,) + c_s, F32)],
        compiler_params=_cparams(2),
        name="mlstm",
    )(q, k, v, gates, mo, c0, n0, m0, g_mlstm_row)


def _build_band_bias(ext_ref, tq, klen, put):
    p_len = ext_ref.shape[-1]
    q_pos = lax.broadcasted_iota(jnp.int32, (tq, klen), 0) + (klen - tq)
    k_pos = lax.broadcasted_iota(jnp.int32, (tq, klen), 1)
    back = q_pos // CHUNK - k_pos // CHUNK
    for h in range(NH_A):
        base = jnp.broadcast_to(ext_ref[h:h + 1, :] * LOG2E, (tq, p_len))
        toeplitz = pltpu.roll(base, 0, 1, stride=1, stride_axis=0)[:, :klen]
        put(h, jnp.where(back >= 0, jnp.where(back <= LEFT_CHUNKS, toeplitz, NEG), NEG))


def _lane_tiles(s):
    n = s.shape[-1]
    if n % LANES:
        return [s]
    return [s[:, i * LANES:(i + 1) * LANES] for i in range(n // LANES)]


def _row_reduce(tiles, op, lane_op):
    by_width = {}
    for t in tiles:
        by_width[t.shape[-1]] = t if t.shape[-1] not in by_width else op(by_width[t.shape[-1]], t)
    return functools.reduce(op, [lane_op(t, axis=-1, keepdims=True) for t in by_width.values()])


def _softmax_pv(ss, vs):
    mx = _row_reduce([t for s in ss for t in _lane_tiles(s)], jnp.maximum, jnp.max)
    es = [jnp.exp2(s - mx) for s in ss]
    l = _row_reduce([t for e in es for t in _lane_tiles(e)], jnp.add, jnp.sum)
    o = functools.reduce(jnp.add, [_dot(e.astype(BF16), v) for e, v in zip(es, vs)])
    return o / l


def _attn_pairs(jobs):
    first = lax.broadcasted_iota(jnp.int32, (1, LANES), 1) < DH_A
    zero = jnp.zeros((), BF16)

    def scores(q, segs, p):
        sl = slice(p * LANES, (p + 1) * LANES)
        qp = q(sl)
        q2 = jnp.concatenate([jnp.where(first, qp, zero), jnp.where(first, zero, qp)], axis=0)
        ss = []
        for k, _, bias, pen in segs:
            s = _dot_nt(q2, k(sl)) + bias(p)
            ss.append(s if pen is None else s + pen)
        return ss

    def finish(ss, segs, put, p):
        sl = slice(p * LANES, (p + 1) * LANES)
        r = ss[0].shape[0] // 2
        o2 = _softmax_pv(ss, [v(sl) for _, v, _, _ in segs])
        put(sl, jnp.where(first, o2[:r], o2[r:]))

    pending = []
    for q, segs, put in jobs:
        for p in range(NH_A // 2):
            pending.append((scores(q, segs, p), segs, put, p))
            if len(pending) > PAIR_ATT_LOOKAHEAD:
                finish(*pending.pop(0))
    while pending:
        finish(*pending.pop(0))


def _attn_prompt_kernel(q_ref, kp_ref, ko_ref, vp_ref, vo_ref, ext_ref, o_ref, bias_ref):
    tq = q_ref.shape[1]
    j = pl.program_id(1)

    @pl.when((pl.program_id(0) == 0) & (j == 0))
    def _():
        def put_bias(h, tile):
            bias_ref[h // 2, (h % 2) * ATT_SUB:(h % 2 + 1) * ATT_SUB, :] = tile
        _build_band_bias(ext_ref, ATT_SUB, tq + ATT_SUB, put_bias)

    def run(pen_prev):
        jobs = []
        for i in range(tq // ATT_SUB):
            lo = i * ATT_SUB
            n_prev = tq - lo
            bias = lambda a, b: (lambda p: bias_ref[p, :, a:b])
            rows = lambda ref, a, b: (lambda sl: ref[0, a:b, sl])

            def put(sl, val, lo=lo):
                o_ref[0, lo:lo + ATT_SUB, sl] = val.astype(o_ref.dtype)

            segs = [(rows(kp_ref, lo, tq), rows(vp_ref, lo, tq), bias(0, n_prev), pen_prev),
                    (rows(ko_ref, 0, lo + ATT_SUB), rows(vo_ref, 0, lo + ATT_SUB),
                     bias(n_prev, tq + ATT_SUB), None)]
            jobs.append((rows(q_ref, lo, lo + ATT_SUB), segs, put))
        _attn_pairs(jobs)

    @pl.when(j >= 1)
    def _():
        run(None)

    @pl.when(j < 1)
    def _():
        run(jnp.full((), NEG, F32))


def _attn_prompt(aq, ak, av, ext):
    NB, L, _ = aq.shape
    tq = ATT_WINDOW
    assert L % tq == 0
    grid = (NB, L // tq)
    row = pl.BlockSpec((1, tq, D_A), lambda b, j: (b, j, 0))
    prev = pl.BlockSpec((1, tq, D_A), lambda b, j: (b, jnp.maximum(j - 1, 0), 0))
    return pl.pallas_call(
        _attn_prompt_kernel,
        out_shape=jax.ShapeDtypeStruct((NB, L, D_A), BF16),
        grid=grid,
        in_specs=[row, prev, row, prev, row, _const_spec(ext.shape)],
        out_specs=row,
        scratch_shapes=[pltpu.VMEM((NH_A // 2, 2 * ATT_SUB, tq + ATT_SUB), F32)],
        compiler_params=_cparams(2),
        name="attn_prompt",
    )(aq, ak, ak, av, av, ext)


def _attn_sample_kernel(q_ref, k_ref, v_ref, ckt_ref, cvt_ref, ext_ref, o_ref, bias_ref, *, nb):
    tq = q_ref.shape[1]
    nc = ckt_ref.shape[-1]

    @pl.when(pl.program_id(0) == 0)
    def _():
        def put_bias(h, tile):
            bias_ref[h] = tile
        _build_band_bias(ext_ref, tq, nc + tq, put_bias)

    def scores(b, h):
        hs = slice(h * DH_A, (h + 1) * DH_A)
        qh = q_ref[b, :, hs]
        return [_dot(qh, ckt_ref[b, h].astype(BF16)) + bias_ref[h, :, 0:nc],
                _dot_nt(qh, k_ref[b, :, hs]) + bias_ref[h, :, nc:nc + tq]]

    def finish(b, h, ss):
        hs = slice(h * DH_A, (h + 1) * DH_A)
        mx = jnp.maximum(*[jnp.max(s, axis=-1, keepdims=True) for s in ss])
        e_c, e_o = [jnp.exp2(s - mx) for s in ss]
        l = jnp.sum(e_c, axis=-1, keepdims=True) + jnp.sum(e_o, axis=-1, keepdims=True)
        o = (_dot_nt(e_c.astype(BF16), cvt_ref[b, h].astype(BF16))
             + _dot(e_o.astype(BF16), v_ref[b, :, hs]))
        return o / l

    outs = {b: [] for b in range(nb)}
    pending = []

    def retire():
        b, h, ss = pending.pop(0)
        outs[b].append(finish(b, h, ss))
        if h == NH_A - 1:
            o_ref[b] = jnp.concatenate(outs[b], axis=-1).astype(o_ref.dtype)

    for b in range(nb):
        for h in range(NH_A):
            pending.append((b, h, scores(b, h)))
            if len(pending) > SAMPLE_ATT_LOOKAHEAD:
                retire()
    while pending:
        retire()


def _attn_sample(aq, ak, av, cache_kt, cache_vt, ext, *, nb):
    NB, L, _ = aq.shape
    nc = cache_kt.shape[-1]
    row = pl.BlockSpec((nb, L, D_A), lambda i: (i, 0, 0))
    crow = pl.BlockSpec((nb, NH_A, DH_A, nc), lambda i: (i, 0, 0, 0))
    return pl.pallas_call(
        functools.partial(_attn_sample_kernel, nb=nb),
        out_shape=jax.ShapeDtypeStruct((NB, L, D_A), BF16),
        grid=(NB // nb,),
        in_specs=[row, row, row, crow, crow, _const_spec(ext.shape)],
        out_specs=row,
        scratch_shapes=[pltpu.VMEM((NH_A, L, nc + L), F32)],
        compiler_params=_cparams(1),
        name="attn_sample",
    )(aq, ak, av, cache_kt, cache_vt, ext)


def _rel_ext(rel_table, tq, klen):
    p_len = -(-(klen + tq) // LANES) * LANES
    n_lo = tq - CHUNK
    n_hi = p_len - tq - MAX_REL
    rep = lambda col, n: jnp.broadcast_to(col, (NH_A, n))
    by_dist = jnp.concatenate([rep(rel_table[:, :1], n_lo), rel_table, rep(rel_table[:, -1:], n_hi)],
                              axis=1)
    rev = by_dist[:, ::-1]
    n = p_len - klen
    return jnp.concatenate([rev[:, n:], rev[:, :n]], axis=1).astype(F32)


def _gate_lanes(g):
    z = lambda n: jnp.zeros((g.shape[0], n), g.dtype)
    return jnp.concatenate([g[:, :NH_M], z(SUBLANES - NH_M), g[:, NH_M:], z(LANES - SUBLANES - NH_M)], axis=1)


def _layer(x, mod, conv_prev, state, cache, w, *, nb, tl, nbm, lc, final):
    NB, L, d = x.shape
    x1 = _ffn(x, mod, w["g0"], w["up1"], w["dn1"], w["g_final"], nb=nb, tl=tl, sub=0)
    (q, k, v, mo, gates, aq, ak, av, k_tail, v_tail, conv_new) = _proj(
        x1, mod, w["g1"], w["w_main"], w["w_gates"], w["b_gates"], w["conv_w"], w["conv_b"],
        conv_prev, nb=nb, tl=tl)
    c0, n0, m0 = state
    n0 = jnp.pad(n0, ((0, 0), (0, SUBLANES - NH_M), (0, 0)))
    m0 = jnp.broadcast_to(jnp.pad(m0, ((0, 0), (0, SUBLANES - NH_M)))[:, :, None], (NB, SUBLANES, LANES))
    hm, c_new, n_new, m_new = _mlstm(q, k, v, gates, mo, c0, n0, m0, w["g_mlstm"], nbm=nbm, lc=lc)
    if cache is None:
        att = _attn_prompt(aq, ak, av, w["ext_prompt"])
    else:
        att = _attn_sample(aq, ak, av, cache[0], cache[1], w["ext_sample"], nb=min(NB, 4))
    y = _ffn(x1, mod, w["g2"], w["up2"], w["dn2"], w["g_final"], nb=nb, tl=tl, sub=2,
             mix=(hm, att, w["w_out"]), final=final)
    n_keep = k_tail.shape[1]
    states = (c_new, n_new[:, :NH_M, :], m_new[:, :NH_M, 0], conv_new,
              k_tail.reshape(NB, n_keep, NH_A, DH_A), v_tail.reshape(NB, n_keep, NH_A, DH_A))
    return y, states


def kernel(x_prompt, x_sample, state_mlstm_C, state_mlstm_n, state_mlstm_m, state_conv, cache_att_k, cache_att_v, c_prompt, c_sample, w_ada, b_ada, g_norm, w_ffn1_up, w_ffn1_down, w_ffn2_up, w_ffn2_down, w_in, conv_w, conv_b, b_gates, g_mlstm, rel_bias_table, w_out, g_final):
    depth = w_ada.shape[0]
    bp, seq, d = x_prompt.shape
    bs, dseq, _ = x_sample.shape
    xp, xs = x_prompt, x_sample
    st_p, st_s = [], []
    n_c = cache_att_k.shape[2]
    tl_p = min(ATT_WINDOW, seq)
    lc_p = min(MLSTM_CHUNK, seq)
    nb_s = max(1, min(bs, ATT_WINDOW // dseq))
    for l in range(depth):
        rows = bp + bs
        pad = (-rows) % SUBLANES
        c_all = jnp.concatenate([c_prompt, c_sample, jnp.zeros((pad, d), F32)], axis=0)
        mod = _adaln(c_all, w_ada[l], b_ada[l])
        mod_p = mod[:bp].reshape(bp, N_MOD, d)
        mod_s = mod[bp:rows].reshape(bs, N_MOD, d)
        off_g = 4 * D_M
        off_a = off_g + 2 * NH_M
        wl = w_in[l]
        rel = rel_bias_table[l]
        w = {
            "g0": g_norm[l, 0:1], "g1": g_norm[l, 1:2], "g2": g_norm[l, 2:3],
            "g_final": g_final.reshape(1, d),
            "up1": w_ffn1_up[l].astype(BF16), "dn1": w_ffn1_down[l].astype(BF16),
            "up2": w_ffn2_up[l].astype(BF16), "dn2": w_ffn2_down[l].astype(BF16),
            "w_main": jnp.concatenate([wl[:, :off_g], wl[:, off_a:]], axis=1).astype(BF16),
            "w_gates": _gate_lanes(wl[:, off_g:off_a]).astype(BF16),
            "b_gates": _gate_lanes(b_gates[l].reshape(1, 2 * NH_M)),
            "conv_w": conv_w[l], "conv_b": conv_b[l].reshape(1, 2 * D_M),
            "g_mlstm": g_mlstm[l].reshape(1, D_M),
            "w_out": w_out[l].astype(BF16),
            "ext_prompt": _rel_ext(rel, ATT_SUB, ATT_WINDOW + ATT_SUB),
            "ext_sample": _rel_ext(rel, dseq, n_c + dseq),
        }
        zero_state = (jnp.zeros((bp, NH_M, DH_M, DH_M), F32), jnp.zeros((bp, NH_M, DH_M), F32),
                      jnp.zeros((bp, NH_M), F32))
        xp, sp = _layer(xp, mod_p, jnp.zeros((bp, CONV_W - 1, 2 * D_M), F32), zero_state, None, w,
                        nb=1, tl=tl_p, nbm=bp, lc=lc_p, final=l == depth - 1)
        state_s = (state_mlstm_C[l], state_mlstm_n[l], state_mlstm_m[l])
        cache = (jnp.transpose(cache_att_k[l], (0, 2, 3, 1)), jnp.transpose(cache_att_v[l], (0, 2, 3, 1)))
        xs, ss = _layer(xs, mod_s, state_conv[l], state_s, cache, w, nb=nb_s, tl=dseq,
                        nbm=min(bs, 4), lc=dseq, final=l == depth - 1)
        st_p.append(sp)
        st_s.append(ss)
    stk = lambda sts, i: jnp.stack([s[i] for s in sts])
    return ((xp, xs) + tuple(stk(st_p, i) for i in range(6)) + tuple(stk(st_s, i) for i in range(6)))
```

```python
import functools

import jax
import jax.numpy as jnp
from jax import lax
from jax.experimental import pallas as pl
from jax.experimental.pallas import tpu as pltpu

F32 = jnp.float32
BF16 = jnp.bfloat16

CHUNK = 64
NH_M = 4
DH_M = 128
D_M = NH_M * DH_M
NH_A = 8
DH_A = 64
D_A = NH_A * DH_A
CONV_W = 4
LEFT_CHUNKS = 8
ATT_WINDOW = LEFT_CHUNKS * CHUNK
MAX_REL = 2 * CHUNK
N_MOD = 9
EPS = 1e-6
NEG = -1e30
LOG2E = 1.4426950408889634

LANES = 128
SUBLANES = 8
V7X_VMEM_BYTES = 64 * 1024 * 1024
VMEM_LIMIT = V7X_VMEM_BYTES - 8 * 1024 * 1024

FF_CHUNK = 256
FFN_SLICES = 8
CONV_SLICES = 32
MLSTM_CHUNK = 256
MLSTM_CHUNKS_PER_STEP = 2
ATT_SUB = 2 * CHUNK
SAMPLE_ATT_LOOKAHEAD = 3
PAIR_ATT_LOOKAHEAD = 1


def _cparams(n_axes):
    return pltpu.CompilerParams(dimension_semantics=("arbitrary",) * n_axes,
                                vmem_limit_bytes=VMEM_LIMIT)


def _const_spec(shape):
    nd = len(shape)
    return pl.BlockSpec(shape, lambda *_: (0,) * nd, pipeline_mode=pl.Buffered(1))


def _dot(a, b):
    return jnp.dot(a, b, preferred_element_type=F32)


def _dot_nt(a, b):
    return lax.dot_general(a, b, (((1,), (1,)), ((), ())), preferred_element_type=F32)


def _sigmoid(x):
    return 1.0 / (1.0 + jnp.exp(-x))


def _log_sigmoid(x):
    return jnp.minimum(x, 0.0) - jnp.log(1.0 + jnp.exp(-jnp.abs(x)))


def _norm_mod(x, g, shift, scale):
    ms = jnp.mean(x * x, axis=-1, keepdims=True)
    return x * lax.rsqrt(ms + EPS) * (g * (1.0 + scale)) + shift


def _adaln_kernel(c_ref, w_ref, b_ref, o_ref):
    c = c_ref[...]
    o_ref[...] = _dot(c * _sigmoid(c), w_ref[...]) + b_ref[...]


def _adaln(c_all, w_ada, b_ada):
    rows, d = c_all.shape
    n = w_ada.shape[1]
    tn = d
    return pl.pallas_call(
        _adaln_kernel,
        out_shape=jax.ShapeDtypeStruct((rows, n), F32),
        grid=(n // tn,),
        in_specs=[pl.BlockSpec((rows, d), lambda j: (0, 0)),
                  pl.BlockSpec((d, tn), lambda j: (0, j)),
                  pl.BlockSpec((1, tn), lambda j: (0, j))],
        out_specs=pl.BlockSpec((rows, tn), lambda j: (0, j)),
        compiler_params=_cparams(1),
        name="adaln",
    )(c_all, w_ada, b_ada.reshape(1, n))


def _ffn_kernel(*refs, nb, tl, sub, with_mix, with_final):
    if with_mix:
        (x_ref, hm_ref, att_ref, mod_ref, gn_ref, wout_ref, wup_ref, wdn_ref, gfin_ref, o_ref, hb_ref,
         act_ref, mixb_ref) = refs
    else:
        x_ref, mod_ref, gn_ref, wup_ref, wdn_ref, gfin_ref, o_ref, hb_ref, act_ref = refs
    d = x_ref.shape[-1]
    dff = wdn_ref.shape[0]
    n_chunks = dff // FF_CHUNK
    if nb > 1:
        snb, stl = nb // 2, tl
        subs = [(slice(i * snb, (i + 1) * snb), slice(None)) for i in range(2)]
    else:
        snb, stl = nb, tl // 2
        subs = [(slice(None), slice(i * stl, (i + 1) * stl)) for i in range(2)]
    m = snb * stl
    sr = m // FFN_SLICES
    res_ref = o_ref if with_mix else x_ref

    def mods(b_sel):
        mod = mod_ref[b_sel]
        return [mod[:, i:i + 1, :] for i in range(N_MOD)]

    def mix_product(s):
        bs, ts = subs[s]
        mix = _dot(hm_ref[bs, ts].reshape(m, D_M), wout_ref[0:D_M, :])
        return mix + _dot(att_ref[bs, ts].reshape(m, D_A), wout_ref[D_M:, :])

    def prep_rows(s, b_sel, rows, flat, mix, zero):
        mod = mods(b_sel)
        res = x_ref[b_sel, rows]
        if with_mix:
            res = res + mod[5] * mix
            o_ref[b_sel, rows] = res
        xin = res if zero is None else res + zero
        h = _norm_mod(xin, gn_ref[...], mod[3 * sub], mod[3 * sub + 1])
        hb_ref[s, flat] = h.reshape(-1, d).astype(BF16)

    def up(s, c):
        lo = c * FF_CHUNK
        g = _dot(hb_ref[s], wup_ref[:, lo:lo + FF_CHUNK])
        u = _dot(hb_ref[s], wup_ref[:, dff + lo:dff + lo + FF_CHUNK])
        act_ref[s, :, lo:lo + FF_CHUNK] = (g * _sigmoid(g) * u).astype(BF16)
        return g

    def finish(s):
        bs, ts = subs[s]
        dn = _dot(act_ref[s], wdn_ref[...])
        y = res_ref[bs, ts] + 0.5 * mods(bs)[3 * sub + 2] * dn.reshape(snb, stl, d)
        if with_final:
            ms = jnp.mean(y * y, axis=-1, keepdims=True)
            y = y * lax.rsqrt(ms + EPS) * gfin_ref[...]
        o_ref[bs, ts] = y

    bs_a, ts_a = subs[0]
    prep_rows(0, bs_a, ts_a, slice(None), mix_product(0).reshape(snb, stl, d) if with_mix else None,
              None)
    if with_mix:
        mixb_ref[...] = mix_product(1)
    for c in range(n_chunks):
        g = up(0, c)
        if c < FFN_SLICES:
            bits = pltpu.bitcast(g[0:SUBLANES, 0:LANES], jnp.uint32)
            zero = pltpu.bitcast((bits >> 16) >> 16, F32)[0:1, :]
            zero = jnp.concatenate([zero] * (d // LANES), axis=1)
            flat = slice(c * sr, (c + 1) * sr)
            if nb > 1:
                b0 = snb + (c * sr) // stl
                b_sel, rows = slice(b0, b0 + 1), slice((c * sr) % stl, (c * sr) % stl + sr)
            else:
                b_sel, rows = slice(None), slice(stl + c * sr, stl + (c + 1) * sr)
            mix = mixb_ref[flat].reshape(1, sr, d) if with_mix else None
            prep_rows(1, b_sel, rows, flat, mix, zero)
    finish(0)
    for c in range(n_chunks):
        up(1, c)
    finish(1)


def _ffn(x, mod, g_norm_row, w_up, w_dn, g_final, *, nb, tl, sub, mix=None, final=False):
    NB, L, d = x.shape
    dff = w_dn.shape[0]
    m = nb * tl
    if nb > 1:
        nb = 2 * nb
        assert NB % nb == 0
    else:
        tl = 2 * tl
        assert L % tl == 0
    assert m % (16 * FFN_SLICES) == 0
    grid = (NB // nb, L // tl)
    row = lambda w: pl.BlockSpec((nb, tl, w), lambda i, j: (i, j, 0))
    in_specs = [row(d)]
    args = [x]
    scratch = [pltpu.VMEM((2, m, d), BF16), pltpu.VMEM((2, m, dff), BF16)]
    if mix is not None:
        scratch.append(pltpu.VMEM((m, d), F32))
    if mix is not None:
        hm, att, w_out = mix
        in_specs += [row(D_M), row(D_A)]
        args += [hm, att]
    in_specs += [pl.BlockSpec((nb, N_MOD, d), lambda i, j: (i, 0, 0)), _const_spec((1, d))]
    args += [mod, g_norm_row]
    if mix is not None:
        in_specs.append(_const_spec(w_out.shape))
        args.append(w_out)
    in_specs += [_const_spec(w_up.shape), _const_spec(w_dn.shape), _const_spec((1, d))]
    args += [w_up, w_dn, g_final]
    kern = functools.partial(_ffn_kernel, nb=nb, tl=tl, sub=sub, with_mix=mix is not None,
                             with_final=final)
    return pl.pallas_call(
        kern,
        out_shape=jax.ShapeDtypeStruct((NB, L, d), F32),
        grid=grid,
        in_specs=in_specs,
        out_specs=row(d),
        scratch_shapes=scratch,
        compiler_params=_cparams(2),
        name="ffn%d" % sub,
    )(*args)


def _proj_kernel(x_ref, mod_ref, gn_ref, w_ref, wg_ref, bg_ref, cw_ref, cb_ref, cprev_ref,
                 q_ref, k_ref, v_ref, o_ref, gates_ref, aq_ref, ak_ref, av_ref,
                 kt_ref, vt_ref, cnew_ref, hb_ref, ext_ref, *, nb, tl):
    m = nb * tl
    d = x_ref.shape[-1]
    j = pl.program_id(1)
    mod = mod_ref[...]
    h = _norm_mod(x_ref[...], gn_ref[...], mod[:, 3:4, :], mod[:, 4:5, :])
    hb_ref[...] = h.reshape(m, d).astype(BF16)

    @pl.when(j == 0)
    def _():
        ext_ref[:, 0:SUBLANES, :] = jnp.zeros((nb, SUBLANES, 2 * D_M), F32)
        ext_ref[:, SUBLANES - (CONV_W - 1):SUBLANES, :] = cprev_ref[...]

    sr = m // CONV_SLICES

    def conv_slice(c, s, zero):
        lo = c * FF_CHUNK
        if nb > 1:
            b0, r0 = (s * sr) // tl, (s * sr) % tl
            b_sel = slice(b0, b0 + 1)
        else:
            b_sel, r0 = slice(None), s * sr
        acc = cb_ref[:, lo:lo + FF_CHUNK] + zero
        for t in reversed(range(CONV_W)):
            s0 = SUBLANES - (CONV_W - 1) + t + r0
            acc = acc + cw_ref[t:t + 1, lo:lo + FF_CHUNK] * ext_ref[b_sel, s0:s0 + sr, lo:lo + FF_CHUNK]
        y = acc * _sigmoid(acc)
        if lo < D_M:
            q_ref[b_sel, r0:r0 + sr, lo:lo + FF_CHUNK] = y.astype(BF16)
        else:
            k_ref[b_sel, r0:r0 + sr, lo - D_M:lo - D_M + FF_CHUNK] = (y * DH_M ** -0.5).astype(BF16)

    conv_units = [(c, s) for c in range(2 * D_M // FF_CHUNK) for s in range(CONV_SLICES)]

    def tie_conv(result, n):
        for i in range(n):
            if conv_units:
                flat = result.reshape(m, result.shape[-1])
                r0 = (i * m // n) // SUBLANES * SUBLANES
                bits = pltpu.bitcast(flat[r0:r0 + SUBLANES, 0:LANES], jnp.uint32)
                zero = pltpu.bitcast((bits >> 16) >> 16, F32)[0:1, :]
                zero = jnp.concatenate([zero] * (FF_CHUNK // LANES), axis=1)
                conv_slice(*conv_units.pop(0), zero)

    def product(c0, width=FF_CHUNK):
        return _dot(hb_ref[...], w_ref[:, c0:c0 + width]).reshape(nb, tl, width)

    n_qk = 2 * D_M // FF_CHUNK
    n_products = (n_qk - 1) + 5 * (D_M // FF_CHUNK) + 1
    per_product = -(-len(conv_units) // n_products)
    assert per_product <= CONV_SLICES
    for c in range(n_qk):
        lo = c * FF_CHUNK
        r = product(lo)
        ext_ref[:, SUBLANES:SUBLANES + tl, lo:lo + FF_CHUNK] = r
        if c > 0:
            tie_conv(r, per_product)
    for half in range(D_M // FF_CHUNK):
        cs = slice(half * FF_CHUNK, (half + 1) * FF_CHUNK)
        r = product(2 * D_M + half * FF_CHUNK)
        v_ref[:, :, cs] = r.astype(BF16)
        tie_conv(r, per_product)
        r = product(3 * D_M + half * FF_CHUNK)
        o_ref[:, :, cs] = r
        tie_conv(r, per_product)
        r = product(4 * D_M + half * FF_CHUNK)
        aq_ref[:, :, cs] = (r * (DH_A ** -0.5 * LOG2E)).astype(BF16)
        tie_conv(r, per_product)
        r = product(4 * D_M + D_A + half * FF_CHUNK)
        ak_ref[:, :, cs] = r.astype(BF16)
        kt_ref[:, :, cs] = r
        tie_conv(r, per_product)
        r = product(4 * D_M + 2 * D_A + half * FF_CHUNK)
        av_ref[:, :, cs] = r.astype(BF16)
        vt_ref[:, :, cs] = r
        tie_conv(r, per_product)
    r = (_dot(hb_ref[...], wg_ref[...]) + bg_ref[...]).reshape(nb, tl, LANES)
    gates_ref[...] = r
    tie_conv(r, len(conv_units))

    cnew_ref[...] = ext_ref[:, SUBLANES + tl - (CONV_W - 1):SUBLANES + tl, :]
    ext_ref[:, 0:SUBLANES, :] = ext_ref[:, tl:tl + SUBLANES, :]


def _proj(x, mod, g_norm_row, w_main, w_gates, b_gates, conv_w, conv_b, conv_prev, *, nb, tl):
    NB, L, d = x.shape
    assert tl == min(ATT_WINDOW, L)
    m = nb * tl
    grid = (NB // nb, L // tl)
    row = lambda w: pl.BlockSpec((nb, tl, w), lambda i, j: (i, j, 0))
    per_b = lambda r, w: pl.BlockSpec((nb, r, w), lambda i, j: (i, 0, 0))
    bshape = lambda w, dt: jax.ShapeDtypeStruct((NB, L, w), dt)
    in_specs = [row(d), per_b(N_MOD, d), _const_spec((1, d)), _const_spec(w_main.shape),
                _const_spec(w_gates.shape), _const_spec((1, LANES)), _const_spec(conv_w.shape),
                _const_spec((1, 2 * D_M)), per_b(CONV_W - 1, 2 * D_M)]
    out_shape = [bshape(D_M, BF16), bshape(D_M, BF16), bshape(D_M, BF16), bshape(D_M, F32),
                 bshape(LANES, F32), bshape(D_A, BF16), bshape(D_A, BF16), bshape(D_A, BF16),
                 jax.ShapeDtypeStruct((NB, tl, D_A), F32), jax.ShapeDtypeStruct((NB, tl, D_A), F32),
                 jax.ShapeDtypeStruct((NB, CONV_W - 1, 2 * D_M), F32)]
    out_specs = [row(D_M), row(D_M), row(D_M), row(D_M), row(LANES), row(D_A), row(D_A), row(D_A),
                 per_b(tl, D_A), per_b(tl, D_A), per_b(CONV_W - 1, 2 * D_M)]
    return pl.pallas_call(
        functools.partial(_proj_kernel, nb=nb, tl=tl),
        out_shape=out_shape,
        grid=grid,
        in_specs=in_specs,
        out_specs=out_specs,
        scratch_shapes=[pltpu.VMEM((m, d), BF16), pltpu.VMEM((nb, SUBLANES + tl, 2 * D_M), F32)],
        compiler_params=_cparams(2),
        name="proj",
    )(x, mod, g_norm_row, w_main, w_gates, b_gates, conv_w, conv_b, conv_prev)


def _split_bf16(x):
    hi = x.astype(BF16)
    r = x - hi.astype(F32)
    mid = r.astype(BF16)
    return hi, mid, (r - mid.astype(F32)).astype(BF16)


def _mlstm_kernel(q_ref, k_ref, v_ref, gates_ref, mo_ref, c0_ref, n0_ref, m0_ref, gm_ref,
                  hm_ref, c_ref, n_ref, m_ref, ct_ref, *, nbm, lc, nck, single_chunk):
    j = pl.program_id(1)

    def transpose_states(src_ref, dst_ref):
        def one(i, carry):
            dst_ref[i // NH_M, i % NH_M] = jnp.transpose(src_ref[i // NH_M, i % NH_M])
            return carry
        lax.fori_loop(0, nbm * NH_M, one, 0)

    if single_chunk:
        n_ref[...] = n0_ref[...]
        m_ref[...] = m0_ref[...]
        ct_get = lambda b, h: jnp.transpose(c0_ref[b, h])

        def ct_put(b, h, val):
            c_ref[b, h] = jnp.transpose(val)
    else:
        @pl.when(j == 0)
        def _():
            transpose_states(c0_ref, ct_ref)
            n_ref[...] = n0_ref[...]
            m_ref[...] = m0_ref[...]
        ct_get = lambda b, h: ct_ref[b, h]

        def ct_put(b, h, val):
            ct_ref[b, h] = val

    s_i = lax.broadcasted_iota(jnp.int32, (lc, lc), 0)
    t_i = lax.broadcasted_iota(jnp.int32, (lc, lc), 1)
    causal = s_i <= t_i
    upper = jnp.where(causal, 1.0, 0.0).astype(BF16)
    head_row = lax.broadcasted_iota(jnp.int32, (SUBLANES, 1), 0)

    pairs = [(b, h) for b in range(nbm) for h in range(NH_M)]
    units = [(ck, b) for ck in range(nck) for b in range(nbm)]
    hsl = lambda h: slice(h * DH_M, (h + 1) * DH_M)
    tok = lambda ck: slice(ck * lc, (ck + 1) * lc)

    gate = {}
    for ck, b in units:
        gates_t = jnp.transpose(gates_ref[b, tok(ck)])
        ig_t = gates_t[0:SUBLANES]
        lf_t = _log_sigmoid(gates_t[SUBLANES:2 * SUBLANES])
        b_t = sum(_dot(part, upper) for part in _split_bf16(lf_t))
        c_t = ig_t - b_t
        c2_t = c_t * LOG2E
        c2_cols = jnp.transpose(jnp.concatenate([c2_t, jnp.zeros((LANES - SUBLANES, lc), F32)], axis=0))
        c2_masked = [jnp.where(causal, c2_cols[:, h:h + 1], NEG) for h in range(NH_M)]
        cmax2_t = jnp.concatenate(
            [jnp.max(cm, axis=0, keepdims=True) for cm in c2_masked]
            + [jnp.zeros((SUBLANES - NH_M, lc), F32)], axis=0)
        gate[ck, b] = dict(b_t=b_t, c_t=c_t, c2_masked=c2_masked, cmax2=cmax2_t,
                           m_intra=b_t + cmax2_t * (1.0 / LOG2E))

    early = {}
    for ck, b in units:
        for h in range(NH_M):
            early[ck, b, h] = dict(
                s_t=_dot_nt(k_ref[b, tok(ck), hsl(h)], q_ref[b, tok(ck), hsl(h)]),
                v_t=jnp.transpose(v_ref[b, tok(ck), hsl(h)].astype(F32)))

    for ck in range(nck):
        rows = []
        for b in range(nbm):
            g = gate[ck, b]
            b_t, c_t = g["b_t"], g["c_t"]
            b_end = b_t[:, lc - 1:lc]
            m_prev = m_ref[b][:, 0:1]
            log_g = b_end + c_t
            m_end = jnp.maximum(b_end + m_prev, jnp.max(log_g, axis=-1, keepdims=True))
            g_rows = jnp.exp(log_g - m_end)
            log_inter = b_t + m_prev
            m_tok = jnp.maximum(log_inter, g["m_intra"])
            rows.append(dict(decay=jnp.exp(b_end + m_prev - m_end), g=g_rows,
                             g_b=g_rows.astype(BF16), m_end=m_end,
                             r_intra=jnp.exp(g["m_intra"] - m_tok),
                             w_inter=jnp.exp(log_inter - m_tok), floor=jnp.exp(-m_tok),
                             n_prev_b=n_ref[b].astype(BF16)))

        for b, h in pairs:
            q = q_ref[b, tok(ck), hsl(h)]
            ct_prev = ct_get(b, h)
            early[ck, b, h].update(
                ct=ct_prev if single_chunk else None,
                cq_t=_dot_nt(ct_prev.astype(BF16), q),
                nq=_dot_nt(rows[b]["n_prev_b"], q)[h:h + 1, :])

        n_upd = [jnp.zeros((SUBLANES, DH_M), F32) for _ in range(nbm)]
        for b, h in pairs:
            r, g, e = rows[b], gate[ck, b], early[ck, b, h]
            k = k_ref[b, tok(ck), hsl(h)]
            a_t = jnp.exp2(g["c2_masked"][h] - g["cmax2"][h:h + 1, :]) * e["s_t"]
            r_intra, w_inter = r["r_intra"][h:h + 1, :], r["w_inter"][h:h + 1, :]
            num_t = _dot(e["v_t"].astype(BF16), a_t.astype(BF16)) * r_intra + e["cq_t"] * w_inter
            den = jnp.sum(a_t, axis=0, keepdims=True) * r_intra + e["nq"] * w_inter
            h_t = num_t * (1.0 / jnp.maximum(jnp.abs(den), r["floor"][h:h + 1, :]))
            hn_t = h_t * lax.rsqrt(jnp.mean(h_t * h_t, axis=0, keepdims=True) + EPS)
            hn = jnp.transpose(hn_t) * gm_ref[:, hsl(h)]
            hm_ref[b, tok(ck), hsl(h)] = (_sigmoid(mo_ref[b, tok(ck), hsl(h)]) * hn).astype(BF16)

            vg_t = (e["v_t"] * r["g"][h:h + 1, :]).astype(BF16)
            ct_prev = e["ct"] if single_chunk else ct_get(b, h)
            ct_put(b, h, r["decay"][h:h + 1, :] * ct_prev + _dot(vg_t, k))
            n_upd[b] = n_upd[b] + jnp.where(head_row == h, _dot(r["g_b"], k), 0.0)

        for b in range(nbm):
            n_ref[b] = rows[b]["decay"] * n_ref[b] + n_upd[b]
            m_ref[b] = jnp.broadcast_to(rows[b]["m_end"], (SUBLANES, LANES))

    if not single_chunk:
        @pl.when(j == pl.num_programs(1) - 1)
        def _():
            transpose_states(ct_ref, c_ref)


def _mlstm(q, k, v, gates, mo, c0, n0, m0, g_mlstm_row, *, nbm, lc):
    NB, L, _ = q.shape
    nck = MLSTM_CHUNKS_PER_STEP if L % (MLSTM_CHUNKS_PER_STEP * lc) == 0 else 1
    grid = (NB // nbm, L // (nck * lc))
    row = lambda w: pl.BlockSpec((nbm, nck * lc, w), lambda i, j: (i, j, 0))
    st = lambda s: pl.BlockSpec((nbm,) + s, lambda i, j: (i,) + (0,) * len(s))
    c_s, n_s = (NH_M, DH_M, DH_M), (SUBLANES, DH_M)
    return pl.pallas_call(
        functools.partial(_mlstm_kernel, nbm=nbm, lc=lc, nck=nck, single_chunk=L == lc),
        out_shape=[jax.ShapeDtypeStruct((NB, L, D_M), BF16),
                   jax.ShapeDtypeStruct((NB,) + c_s, F32),
                   jax.ShapeDtypeStruct((NB,) + n_s, F32),
                   jax.ShapeDtypeStruct((NB,) + n_s, F32)],
        grid=grid,
        in_specs=[row(D_M), row(D_M), row(D_M), row(LANES), row(D_M), st(c_s), st(n_s), st(n_s),
                  _const_spec((1, D_M))],
        out_specs=[row(D_M), st(c_s), st(n_s), st(n_s)],
        scratch_shapes=[pltpu.VMEM((nbm,) + c_s, F32)],
        compiler_params=_cparams(2),
        name="mlstm",
    )(q, k, v, gates, mo, c0, n0, m0, g_mlstm_row)


def _build_band_bias(ext_ref, tq, klen, put):
    p_len = ext_ref.shape[-1]
    q_pos = lax.broadcasted_iota(jnp.int32, (tq, klen), 0) + (klen - tq)
    k_pos = lax.broadcasted_iota(jnp.int32, (tq, klen), 1)
    back = q_pos // CHUNK - k_pos // CHUNK
    for h in range(NH_A):
        base = jnp.broadcast_to(ext_ref[h:h + 1, :] * LOG2E, (tq, p_len))
        toeplitz = pltpu.roll(base, 0, 1, stride=1, stride_axis=0)[:, :klen]
        put(h, jnp.where(back >= 0, jnp.where(back <= LEFT_CHUNKS, toeplitz, NEG), NEG))


def _lane_tiles(s):
    n = s.shape[-1]
    if n % LANES:
        return [s]
    return [s[:, i * LANES:(i + 1) * LANES] for i in range(n // LANES)]


def _row_reduce(tiles, op, lane_op):
    by_width = {}
    for t in tiles:
        by_width[t.shape[-1]] = t if t.shape[-1] not in by_width else op(by_width[t.shape[-1]], t)
    return functools.reduce(op, [lane_op(t, axis=-1, keepdims=True) for t in by_width.values()])


def _softmax_pv(ss, vs):
    mx = _row_reduce([t for s in ss for t in _lane_tiles(s)], jnp.maximum, jnp.max)
    es = [jnp.exp2(s - mx) for s in ss]
    l = _row_reduce([t for e in es for t in _lane_tiles(e)], jnp.add, jnp.sum)
    o = functools.reduce(jnp.add, [_dot(e.astype(BF16), v) for e, v in zip(es, vs)])
    return o / l


def _attn_pairs(jobs):
    first = lax.broadcasted_iota(jnp.int32, (1, LANES), 1) < DH_A
    zero = jnp.zeros((), BF16)

    def scores(q, segs, p):
        sl = slice(p * LANES, (p + 1) * LANES)
        qp = q(sl)
        q2 = jnp.concatenate([jnp.where(first, qp, zero), jnp.where(first, zero, qp)], axis=0)
        ss = []
        for k, _, bias, pen in segs:
            s = _dot_nt(q2, k(sl)) + bias(p)
            ss.append(s if pen is None else s + pen)
        return ss

    def finish(ss, segs, put, p):
        sl = slice(p * LANES, (p + 1) * LANES)
        r = ss[0].shape[0] // 2
        o2 = _softmax_pv(ss, [v(sl) for _, v, _, _ in segs])
        put(sl, jnp.where(first, o2[:r], o2[r:]))

    pending = []
    for q, segs, put in jobs:
        for p in range(NH_A // 2):
            pending.append((scores(q, segs, p), segs, put, p))
            if len(pending) > PAIR_ATT_LOOKAHEAD:
                finish(*pending.pop(0))
    while pending:
        finish(*pending.pop(0))


def _attn_prompt_kernel(q_ref, ko_ref, vo_ref, ext_ref, o_ref, bias_ref, kp_ref, vp_ref):
    tq = q_ref.shape[1]
    j = pl.program_id(1)

    @pl.when((pl.program_id(0) == 0) & (j == 0))
    def _():
        def put_bias(h, tile):
            bias_ref[h // 2, (h % 2) * ATT_SUB:(h % 2 + 1) * ATT_SUB, :] = tile
        _build_band_bias(ext_ref, ATT_SUB, tq + ATT_SUB, put_bias)

    def run(pen_prev):
        jobs = []
        for i in range(tq // ATT_SUB):
            lo = i * ATT_SUB
            n_prev = tq - lo
            bias = lambda a, b: (lambda p: bias_ref[p, :, a:b])
            rows = lambda ref, a, b: (lambda sl: ref[0, a:b, sl])

            def put(sl, val, lo=lo):
                o_ref[0, lo:lo + ATT_SUB, sl] = val.astype(o_ref.dtype)

            segs = [(rows(kp_ref, lo, tq), rows(vp_ref, lo, tq), bias(0, n_prev), pen_prev),
                    (rows(ko_ref, 0, lo + ATT_SUB), rows(vo_ref, 0, lo + ATT_SUB),
                     bias(n_prev, tq + ATT_SUB), None)]
            jobs.append((rows(q_ref, lo, lo + ATT_SUB), segs, put))
        _attn_pairs(jobs)

    @pl.when(j >= 1)
    def _():
        run(None)

    @pl.when(j < 1)
    def _():
        kp_ref[...] = jnp.zeros(kp_ref.shape, BF16)
        vp_ref[...] = jnp.zeros(vp_ref.shape, BF16)
        run(jnp.full((), NEG, F32))

    kp_ref[...] = ko_ref[...]
    vp_ref[...] = vo_ref[...]


def _attn_prompt(aq, ak, av, ext):
    NB, L, _ = aq.shape
    tq = ATT_WINDOW
    assert L % tq == 0
    grid = (NB, L // tq)
    row = pl.BlockSpec((1, tq, D_A), lambda b, j: (b, j, 0))
    return pl.pallas_call(
        _attn_prompt_kernel,
        out_shape=jax.ShapeDtypeStruct((NB, L, D_A), BF16),
        grid=grid,
        in_specs=[row, row, row, _const_spec(ext.shape)],
        out_specs=row,
        scratch_shapes=[pltpu.VMEM((NH_A // 2, 2 * ATT_SUB, tq + ATT_SUB), F32),
                        pltpu.VMEM((1, tq, D_A), BF16), pltpu.VMEM((1, tq, D_A), BF16)],
        compiler_params=_cparams(2),
        name="attn_prompt",
    )(aq, ak, av, ext)


def _attn_sample_kernel(q_ref, k_ref, v_ref, ckt_ref, cvt_ref, ext_ref, o_ref, bias_ref, *, nb):
    tq = q_ref.shape[1]
    nc = ckt_ref.shape[-1]

    @pl.when(pl.program_id(0) == 0)
    def _():
        def put_bias(h, tile):
            bias_ref[h] = tile
        _build_band_bias(ext_ref, tq, nc + tq, put_bias)

    def scores(b, h):
        hs = slice(h * DH_A, (h + 1) * DH_A)
        qh = q_ref[b, :, hs]
        return [_dot(qh, ckt_ref[b, h].astype(BF16)) + bias_ref[h, :, 0:nc],
                _dot_nt(qh, k_ref[b, :, hs]) + bias_ref[h, :, nc:nc + tq]]

    def finish(b, h, ss):
        hs = slice(h * DH_A, (h + 1) * DH_A)
        mx = jnp.maximum(*[jnp.max(s, axis=-1, keepdims=True) for s in ss])
        e_c, e_o = [jnp.exp2(s - mx) for s in ss]
        l = jnp.sum(e_c, axis=-1, keepdims=True) + jnp.sum(e_o, axis=-1, keepdims=True)
        o = (_dot_nt(e_c.astype(BF16), cvt_ref[b, h].astype(BF16))
             + _dot(e_o.astype(BF16), v_ref[b, :, hs]))
        return o / l

    outs = {b: [] for b in range(nb)}
    pending = []

    def retire():
        b, h, ss = pending.pop(0)
        outs[b].append(finish(b, h, ss))
        if h == NH_A - 1:
            o_ref[b] = jnp.concatenate(outs[b], axis=-1).astype(o_ref.dtype)

    for b in range(nb):
        for h in range(NH_A):
            pending.append((b, h, scores(b, h)))
            if len(pending) > SAMPLE_ATT_LOOKAHEAD:
                retire()
    while pending:
        retire()


def _attn_sample(aq, ak, av, cache_kt, cache_vt, ext, *, nb):
    NB, L, _ = aq.shape
    nc = cache_kt.shape[-1]
    row = pl.BlockSpec((nb, L, D_A), lambda i: (i, 0, 0))
    crow = pl.BlockSpec((nb, NH_A, DH_A, nc), lambda i: (i, 0, 0, 0))
    return pl.pallas_call(
        functools.partial(_attn_sample_kernel, nb=nb),
        out_shape=jax.ShapeDtypeStruct((NB, L, D_A), BF16),
        grid=(NB // nb,),
        in_specs=[row, row, row, crow, crow, _const_spec(ext.shape)],
        out_specs=row,
        scratch_shapes=[pltpu.VMEM((NH_A, L, nc + L), F32)],
        compiler_params=_cparams(1),
        name="attn_sample",
    )(aq, ak, av, cache_kt, cache_vt, ext)


def _rel_ext(rel_table, tq, klen):
    p_len = -(-(klen + tq) // LANES) * LANES
    n_lo = tq - CHUNK
    n_hi = p_len - tq - MAX_REL
    rep = lambda col, n: jnp.broadcast_to(col, (NH_A, n))
    by_dist = jnp.concatenate([rep(rel_table[:, :1], n_lo), rel_table, rep(rel_table[:, -1:], n_hi)],
                              axis=1)
    rev = by_dist[:, ::-1]
    n = p_len - klen
    return jnp.concatenate([rev[:, n:], rev[:, :n]], axis=1).astype(F32)


def _gate_lanes(g):
    z = lambda n: jnp.zeros((g.shape[0], n), g.dtype)
    return jnp.concatenate([g[:, :NH_M], z(SUBLANES - NH_M), g[:, NH_M:], z(LANES - SUBLANES - NH_M)], axis=1)


def _layer(x, mod, conv_prev, state, cache, w, *, nb, tl, nbm, lc, final):
    NB, L, d = x.shape
    x1 = _ffn(x, mod, w["g0"], w["up1"], w["dn1"], w["g_final"], nb=nb, tl=tl, sub=0)
    (q, k, v, mo, gates, aq, ak, av, k_tail, v_tail, conv_new) = _proj(
        x1, mod, w["g1"], w["w_main"], w["w_gates"], w["b_gates"], w["conv_w"], w["conv_b"],
        conv_prev, nb=nb, tl=tl)
    c0, n0, m0 = state
    n0 = jnp.pad(n0, ((0, 0), (0, SUBLANES - NH_M), (0, 0)))
    m0 = jnp.broadcast_to(jnp.pad(m0, ((0, 0), (0, SUBLANES - NH_M)))[:, :, None], (NB, SUBLANES, LANES))
    hm, c_new, n_new, m_new = _mlstm(q, k, v, gates, mo, c0, n0, m0, w["g_mlstm"], nbm=nbm, lc=lc)
    if cache is None:
        att = _attn_prompt(aq, ak, av, w["ext_prompt"])
    else:
        att = _attn_sample(aq, ak, av, cache[0], cache[1], w["ext_sample"], nb=min(NB, 4))
    y = _ffn(x1, mod, w["g2"], w["up2"], w["dn2"], w["g_final"], nb=nb, tl=tl, sub=2,
             mix=(hm, att, w["w_out"]), final=final)
    n_keep = k_tail.shape[1]
    states = (c_new, n_new[:, :NH_M, :], m_new[:, :NH_M, 0], conv_new,
              k_tail.reshape(NB, n_keep, NH_A, DH_A), v_tail.reshape(NB, n_keep, NH_A, DH_A))
    return y, states


def kernel(x_prompt, x_sample, state_mlstm_C, state_mlstm_n, state_mlstm_m, state_conv, cache_att_k, cache_att_v, c_prompt, c_sample, w_ada, b_ada, g_norm, w_ffn1_up, w_ffn1_down, w_ffn2_up, w_ffn2_down, w_in, conv_w, conv_b, b_gates, g_mlstm, rel_bias_table, w_out, g_final):
    depth = w_ada.shape[0]
    bp, seq, d = x_prompt.shape
    bs, dseq, _ = x_sample.shape
    xp, xs = x_prompt, x_sample
    st_p, st_s = [], []
    n_c = cache_att_k.shape[2]
    tl_p = min(ATT_WINDOW, seq)
    lc_p = min(MLSTM_CHUNK, seq)
    nb_s = max(1, min(bs, ATT_WINDOW // dseq))
    for l in range(depth):
        rows = bp + bs
        pad = (-rows) % SUBLANES
        c_all = jnp.concatenate([c_prompt, c_sample, jnp.zeros((pad, d), F32)], axis=0)
        mod = _adaln(c_all, w_ada[l], b_ada[l])
        mod_p = mod[:bp].reshape(bp, N_MOD, d)
        mod_s = mod[bp:rows].reshape(bs, N_MOD, d)
        off_g = 4 * D_M
        off_a = off_g + 2 * NH_M
        wl = w_in[l]
        rel = rel_bias_table[l]
        w = {
            "g0": g_norm[l, 0:1], "g1": g_norm[l, 1:2], "g2": g_norm[l, 2:3],
            "g_final": g_final.reshape(1, d),
            "up1": w_ffn1_up[l].astype(BF16), "dn1": w_ffn1_down[l].astype(BF16),
            "up2": w_ffn2_up[l].astype(BF16), "dn2": w_ffn2_down[l].astype(BF16),
            "w_main": jnp.concatenate([wl[:, :off_g], wl[:, off_a:]], axis=1).astype(BF16),
            "w_gates": _gate_lanes(wl[:, off_g:off_a]).astype(BF16),
            "b_gates": _gate_lanes(b_gates[l].reshape(1, 2 * NH_M)),
            "conv_w": conv_w[l], "conv_b": conv_b[l].reshape(1, 2 * D_M),
            "g_mlstm": g_mlstm[l].reshape(1, D_M),
            "w_out": w_out[l].astype(BF16),
            "ext_prompt": _rel_ext(rel, ATT_SUB, ATT_WINDOW + ATT_SUB),
            "ext_sample": _rel_ext(rel, dseq, n_c + dseq),
        }
        zero_state = (jnp.zeros((bp, NH_M, DH_M, DH_M), F32), jnp.zeros((bp, NH_M, DH_M), F32),
                      jnp.zeros((bp, NH_M), F32))
        xp, sp = _layer(xp, mod_p, jnp.zeros((bp, CONV_W - 1, 2 * D_M), F32), zero_state, None, w,
                        nb=1, tl=tl_p, nbm=bp, lc=lc_p, final=l == depth - 1)
        state_s = (state_mlstm_C[l], state_mlstm_n[l], state_mlstm_m[l])
        cache = (jnp.transpose(cache_att_k[l], (0, 2, 3, 1)), jnp.transpose(cache_att_v[l], (0, 2, 3, 1)))
        xs, ss = _layer(xs, mod_s, state_conv[l], state_s, cache, w, nb=nb_s, tl=dseq,
                        nbm=min(bs, 4), lc=dseq, final=l == depth - 1)
        st_p.append(sp)
        st_s.append(ss)
    stk = lambda sts, i: jnp.stack([s[i] for s in sts])
    return ((xp, xs) + tuple(stk(st_p, i) for i in range(6)) + tuple(stk(st_s, i) for i in range(6)))
```

```python
import functools

import jax
import jax.numpy as jnp
from jax import lax
from jax.experimental import pallas as pl
from jax.experimental.pallas import tpu as pltpu

F32 = jnp.float32
BF16 = jnp.bfloat16

CHUNK = 64
NH_M = 4
DH_M = 128
D_M = NH_M * DH_M
NH_A = 8
DH_A = 64
D_A = NH_A * DH_A
CONV_W = 4
LEFT_CHUNKS = 8
ATT_WINDOW = LEFT_CHUNKS * CHUNK
MAX_REL = 2 * CHUNK
N_MOD = 9
EPS = 1e-6
NEG = -1e30
LOG2E = 1.4426950408889634

LANES = 128
SUBLANES = 8
V7X_VMEM_BYTES = 64 * 1024 * 1024
VMEM_LIMIT = V7X_VMEM_BYTES - 8 * 1024 * 1024

FF_CHUNK = 256
FFN_SLICES = 16
CONV_SLICES = 32
MLSTM_CHUNK = 256
MLSTM_CHUNKS_PER_STEP = 2
ATT_SUB = 2 * CHUNK
SAMPLE_ATT_LOOKAHEAD = 3
PAIR_ATT_LOOKAHEAD = 1


def _cparams(n_axes):
    return pltpu.CompilerParams(dimension_semantics=("arbitrary",) * n_axes,
                                vmem_limit_bytes=VMEM_LIMIT)


def _const_spec(shape):
    nd = len(shape)
    return pl.BlockSpec(shape, lambda *_: (0,) * nd, pipeline_mode=pl.Buffered(1))


def _dot(a, b):
    return jnp.dot(a, b, preferred_element_type=F32)


def _dot_nt(a, b):
    return lax.dot_general(a, b, (((1,), (1,)), ((), ())), preferred_element_type=F32)


def _sigmoid(x):
    return 1.0 / (1.0 + jnp.exp(-x))


def _log_sigmoid(x):
    return jnp.minimum(x, 0.0) - jnp.log(1.0 + jnp.exp(-jnp.abs(x)))


def _norm_mod(x, g, shift, scale):
    ms = jnp.mean(x * x, axis=-1, keepdims=True)
    return x * lax.rsqrt(ms + EPS) * (g * (1.0 + scale)) + shift


def _adaln_kernel(c_ref, w_ref, b_ref, o_ref):
    c = c_ref[...]
    o_ref[...] = _dot(c * _sigmoid(c), w_ref[...]) + b_ref[...]


def _adaln(c_all, w_ada, b_ada):
    rows, d = c_all.shape
    n = w_ada.shape[1]
    tn = d
    return pl.pallas_call(
        _adaln_kernel,
        out_shape=jax.ShapeDtypeStruct((rows, n), F32),
        grid=(n // tn,),
        in_specs=[pl.BlockSpec((rows, d), lambda j: (0, 0)),
                  pl.BlockSpec((d, tn), lambda j: (0, j)),
                  pl.BlockSpec((1, tn), lambda j: (0, j))],
        out_specs=pl.BlockSpec((rows, tn), lambda j: (0, j)),
        compiler_params=_cparams(1),
        name="adaln",
    )(c_all, w_ada, b_ada.reshape(1, n))


def _ffn_kernel(*refs, nb, tl, sub, with_mix, with_final):
    if with_mix:
        (x_ref, hm_ref, att_ref, mod_ref, gn_ref, wout_ref, wup_ref, wdn_ref, gfin_ref, o_ref, hb_ref,
         act_ref, mixb_ref) = refs
    else:
        x_ref, mod_ref, gn_ref, wup_ref, wdn_ref, gfin_ref, o_ref, hb_ref, act_ref = refs
    d = x_ref.shape[-1]
    dff = wdn_ref.shape[0]
    n_chunks = dff // FF_CHUNK
    if nb > 1:
        snb, stl = nb // 2, tl
        subs = [(slice(i * snb, (i + 1) * snb), slice(None)) for i in range(2)]
    else:
        snb, stl = nb, tl // 2
        subs = [(slice(None), slice(i * stl, (i + 1) * stl)) for i in range(2)]
    m = snb * stl
    sr = m // FFN_SLICES
    res_ref = o_ref if with_mix else x_ref

    def mods(b_sel):
        mod = mod_ref[b_sel]
        return [mod[:, i:i + 1, :] for i in range(N_MOD)]

    def mix_product(s):
        bs, ts = subs[s]
        mix = _dot(hm_ref[bs, ts].reshape(m, D_M), wout_ref[0:D_M, :])
        return mix + _dot(att_ref[bs, ts].reshape(m, D_A), wout_ref[D_M:, :])

    def prep_rows(s, b_sel, rows, flat, mix, zero):
        mod = mods(b_sel)
        res = x_ref[b_sel, rows]
        if with_mix:
            res = res + mod[5] * mix
            o_ref[b_sel, rows] = res
        xin = res if zero is None else res + zero
        h = _norm_mod(xin, gn_ref[...], mod[3 * sub], mod[3 * sub + 1])
        hb_ref[s, flat] = h.reshape(-1, d).astype(BF16)

    def up(s, c):
        lo = c * FF_CHUNK
        g = _dot(hb_ref[s], wup_ref[:, lo:lo + FF_CHUNK])
        u = _dot(hb_ref[s], wup_ref[:, dff + lo:dff + lo + FF_CHUNK])
        act_ref[s, :, lo:lo + FF_CHUNK] = (g * _sigmoid(g) * u).astype(BF16)
        return g

    def finish(s):
        bs, ts = subs[s]
        dn = _dot(act_ref[s], wdn_ref[...])
        y = res_ref[bs, ts] + 0.5 * mods(bs)[3 * sub + 2] * dn.reshape(snb, stl, d)
        if with_final:
            ms = jnp.mean(y * y, axis=-1, keepdims=True)
            y = y * lax.rsqrt(ms + EPS) * gfin_ref[...]
        o_ref[bs, ts] = y

    bs_a, ts_a = subs[0]
    if with_mix:
        mix_a = mix_product(0).reshape(snb, stl, d)
        mixb_ref[...] = mix_product(1)
        prep_rows(0, bs_a, ts_a, slice(None), mix_a, None)
    else:
        prep_rows(0, bs_a, ts_a, slice(None), None, None)
    for c in range(n_chunks):
        g = up(0, c)
        for half in range(2):
            i = 2 * c + half
            if i < FFN_SLICES:
                g0 = half * (m // 2)
                bits = pltpu.bitcast(g[g0:g0 + SUBLANES, 0:LANES], jnp.uint32)
                zero = pltpu.bitcast((bits >> 16) >> 16, F32)[0:1, :]
                zero = jnp.concatenate([zero] * (d // LANES), axis=1)
                flat = slice(i * sr, (i + 1) * sr)
                if nb > 1:
                    b0 = snb + (i * sr) // stl
                    b_sel, rows = slice(b0, b0 + 1), slice((i * sr) % stl, (i * sr) % stl + sr)
                else:
                    b_sel, rows = slice(None), slice(stl + i * sr, stl + (i + 1) * sr)
                mix = mixb_ref[flat].reshape(1, sr, d) if with_mix else None
                prep_rows(1, b_sel, rows, flat, mix, zero)
    finish(0)
    for c in range(n_chunks):
        up(1, c)
    finish(1)


def _ffn(x, mod, g_norm_row, w_up, w_dn, g_final, *, nb, tl, sub, mix=None, final=False):
    NB, L, d = x.shape
    dff = w_dn.shape[0]
    m = nb * tl
    if nb > 1:
        nb = 2 * nb
        assert NB % nb == 0
    else:
        tl = 2 * tl
        assert L % tl == 0
    assert m % (16 * FFN_SLICES) == 0
    grid = (NB // nb, L // tl)
    row = lambda w: pl.BlockSpec((nb, tl, w), lambda i, j: (i, j, 0))
    in_specs = [row(d)]
    args = [x]
    scratch = [pltpu.VMEM((2, m, d), BF16), pltpu.VMEM((2, m, dff), BF16)]
    if mix is not None:
        scratch.append(pltpu.VMEM((m, d), F32))
    if mix is not None:
        hm, att, w_out = mix
        in_specs += [row(D_M), row(D_A)]
        args += [hm, att]
    in_specs += [pl.BlockSpec((nb, N_MOD, d), lambda i, j: (i, 0, 0)), _const_spec((1, d))]
    args += [mod, g_norm_row]
    if mix is not None:
        in_specs.append(_const_spec(w_out.shape))
        args.append(w_out)
    in_specs += [_const_spec(w_up.shape), _const_spec(w_dn.shape), _const_spec((1, d))]
    args += [w_up, w_dn, g_final]
    kern = functools.partial(_ffn_kernel, nb=nb, tl=tl, sub=sub, with_mix=mix is not None,
                             with_final=final)
    return pl.pallas_call(
        kern,
        out_shape=jax.ShapeDtypeStruct((NB, L, d), F32),
        grid=grid,
        in_specs=in_specs,
        out_specs=row(d),
        scratch_shapes=scratch,
        compiler_params=_cparams(2),
        name="ffn%d" % sub,
    )(*args)


def _proj_kernel(x_ref, mod_ref, gn_ref, w_ref, wg_ref, bg_ref, cw_ref, cb_ref, cprev_ref,
                 q_ref, k_ref, v_ref, o_ref, gates_ref, aq_ref, ak_ref, av_ref,
                 kt_ref, vt_ref, cnew_ref, hb_ref, ext_ref, *, nb, tl):
    m = nb * tl
    d = x_ref.shape[-1]
    j = pl.program_id(1)
    mod = mod_ref[...]
    h = _norm_mod(x_ref[...], gn_ref[...], mod[:, 3:4, :], mod[:, 4:5, :])
    hb_ref[...] = h.reshape(m, d).astype(BF16)

    @pl.when(j == 0)
    def _():
        ext_ref[:, 0:SUBLANES, :] = jnp.zeros((nb, SUBLANES, 2 * D_M), F32)
        ext_ref[:, SUBLANES - (CONV_W - 1):SUBLANES, :] = cprev_ref[...]

    sr = m // CONV_SLICES

    def conv_slice(c, s, zero):
        lo = c * FF_CHUNK
        if nb > 1:
            b0, r0 = (s * sr) // tl, (s * sr) % tl
            b_sel = slice(b0, b0 + 1)
        else:
            b_sel, r0 = slice(None), s * sr
        acc = cb_ref[:, lo:lo + FF_CHUNK] + zero
        for t in reversed(range(CONV_W)):
            s0 = SUBLANES - (CONV_W - 1) + t + r0
            acc = acc + cw_ref[t:t + 1, lo:lo + FF_CHUNK] * ext_ref[b_sel, s0:s0 + sr, lo:lo + FF_CHUNK]
        y = acc * _sigmoid(acc)
        if lo < D_M:
            q_ref[b_sel, r0:r0 + sr, lo:lo + FF_CHUNK] = y.astype(BF16)
        else:
            k_ref[b_sel, r0:r0 + sr, lo - D_M:lo - D_M + FF_CHUNK] = (y * DH_M ** -0.5).astype(BF16)

    conv_units = [(c, s) for c in range(2 * D_M // FF_CHUNK) for s in range(CONV_SLICES)]

    def tie_conv(result, n):
        for i in range(n):
            if conv_units:
                flat = result.reshape(m, result.shape[-1])
                r0 = (i * m // n) // SUBLANES * SUBLANES
                bits = pltpu.bitcast(flat[r0:r0 + SUBLANES, 0:LANES], jnp.uint32)
                zero = pltpu.bitcast((bits >> 16) >> 16, F32)[0:1, :]
                zero = jnp.concatenate([zero] * (FF_CHUNK // LANES), axis=1)
                conv_slice(*conv_units.pop(0), zero)

    def product(c0, width=FF_CHUNK):
        return _dot(hb_ref[...], w_ref[:, c0:c0 + width]).reshape(nb, tl, width)

    n_qk = 2 * D_M // FF_CHUNK
    n_products = (n_qk - 1) + 5 * (D_M // FF_CHUNK) + 1
    per_product = -(-len(conv_units) // n_products)
    assert per_product <= CONV_SLICES
    for c in range(n_qk):
        lo = c * FF_CHUNK
        r = product(lo)
        ext_ref[:, SUBLANES:SUBLANES + tl, lo:lo + FF_CHUNK] = r
        if c > 0:
            tie_conv(r, per_product)
    for half in range(D_M // FF_CHUNK):
        cs = slice(half * FF_CHUNK, (half + 1) * FF_CHUNK)
        r = product(2 * D_M + half * FF_CHUNK)
        v_ref[:, :, cs] = r.astype(BF16)
        tie_conv(r, per_product)
        r = product(3 * D_M + half * FF_CHUNK)
        o_ref[:, :, cs] = r
        tie_conv(r, per_product)
        r = product(4 * D_M + half * FF_CHUNK)
        aq_ref[:, :, cs] = (r * (DH_A ** -0.5 * LOG2E)).astype(BF16)
        tie_conv(r, per_product)
        r = product(4 * D_M + D_A + half * FF_CHUNK)
        ak_ref[:, :, cs] = r.astype(BF16)
        kt_ref[:, :, cs] = r
        tie_conv(r, per_product)
        r = product(4 * D_M + 2 * D_A + half * FF_CHUNK)
        av_ref[:, :, cs] = r.astype(BF16)
        vt_ref[:, :, cs] = r
        tie_conv(r, per_product)
    r = (_dot(hb_ref[...], wg_ref[...]) + bg_ref[...]).reshape(nb, tl, LANES)
    gates_ref[...] = r
    tie_conv(r, len(conv_units))

    cnew_ref[...] = ext_ref[:, SUBLANES + tl - (CONV_W - 1):SUBLANES + tl, :]
    ext_ref[:, 0:SUBLANES, :] = ext_ref[:, tl:tl + SUBLANES, :]


def _proj(x, mod, g_norm_row, w_main, w_gates, b_gates, conv_w, conv_b, conv_prev, *, nb, tl):
    NB, L, d = x.shape
    assert tl == min(ATT_WINDOW, L)
    m = nb * tl
    grid = (NB // nb, L // tl)
    row = lambda w: pl.BlockSpec((nb, tl, w), lambda i, j: (i, j, 0))
    per_b = lambda r, w: pl.BlockSpec((nb, r, w), lambda i, j: (i, 0, 0))
    bshape = lambda w, dt: jax.ShapeDtypeStruct((NB, L, w), dt)
    in_specs = [row(d), per_b(N_MOD, d), _const_spec((1, d)), _const_spec(w_main.shape),
                _const_spec(w_gates.shape), _const_spec((1, LANES)), _const_spec(conv_w.shape),
                _const_spec((1, 2 * D_M)), per_b(CONV_W - 1, 2 * D_M)]
    out_shape = [bshape(D_M, BF16), bshape(D_M, BF16), bshape(D_M, BF16), bshape(D_M, F32),
                 bshape(LANES, F32), bshape(D_A, BF16), bshape(D_A, BF16), bshape(D_A, BF16),
                 jax.ShapeDtypeStruct((NB, tl, D_A), F32), jax.ShapeDtypeStruct((NB, tl, D_A), F32),
                 jax.ShapeDtypeStruct((NB, CONV_W - 1, 2 * D_M), F32)]
    out_specs = [row(D_M), row(D_M), row(D_M), row(D_M), row(LANES), row(D_A), row(D_A), row(D_A),
                 per_b(tl, D_A), per_b(tl, D_A), per_b(CONV_W - 1, 2 * D_M)]
    return pl.pallas_call(
        functools.partial(_proj_kernel, nb=nb, tl=tl),
        out_shape=out_shape,
        grid=grid,
        in_specs=in_specs,
        out_specs=out_specs,
        scratch_shapes=[pltpu.VMEM((m, d), BF16), pltpu.VMEM((nb, SUBLANES + tl, 2 * D_M), F32)],
        compiler_params=_cparams(2),
        name="proj",
    )(x, mod, g_norm_row, w_main, w_gates, b_gates, conv_w, conv_b, conv_prev)


def _split_bf16(x):
    hi = x.astype(BF16)
    r = x - hi.astype(F32)
    mid = r.astype(BF16)
    return hi, mid, (r - mid.astype(F32)).astype(BF16)


def _mlstm_kernel(q_ref, k_ref, v_ref, gates_ref, mo_ref, c0_ref, n0_ref, m0_ref, gm_ref,
                  hm_ref, c_ref, n_ref, m_ref, ct_ref, *, nbm, lc, nck, single_chunk):
    j = pl.program_id(1)

    def transpose_states(src_ref, dst_ref):
        def one(i, carry):
            dst_ref[i // NH_M, i % NH_M] = jnp.transpose(src_ref[i // NH_M, i % NH_M])
            return carry
        lax.fori_loop(0, nbm * NH_M, one, 0)

    if single_chunk:
        n_ref[...] = n0_ref[...]
        m_ref[...] = m0_ref[...]
        ct_get = lambda b, h: jnp.transpose(c0_ref[b, h])

        def ct_put(b, h, val):
            c_ref[b, h] = jnp.transpose(val)
    else:
        @pl.when(j == 0)
        def _():
            transpose_states(c0_ref, ct_ref)
            n_ref[...] = n0_ref[...]
            m_ref[...] = m0_ref[...]
        ct_get = lambda b, h: ct_ref[b, h]

        def ct_put(b, h, val):
            ct_ref[b, h] = val

    s_i = lax.broadcasted_iota(jnp.int32, (lc, lc), 0)
    t_i = lax.broadcasted_iota(jnp.int32, (lc, lc), 1)
    causal = s_i <= t_i
    upper = jnp.where(causal, 1.0, 0.0).astype(BF16)
    head_row = lax.broadcasted_iota(jnp.int32, (SUBLANES, 1), 0)

    pairs = [(b, h) for b in range(nbm) for h in range(NH_M)]
    units = [(ck, b) for ck in range(nck) for b in range(nbm)]
    hsl = lambda h: slice(h * DH_M, (h + 1) * DH_M)
    tok = lambda ck: slice(ck * lc, (ck + 1) * lc)

    gate = {}
    for ck, b in units:
        gates_t = jnp.transpose(gates_ref[b, tok(ck)])
        ig_t = gates_t[0:SUBLANES]
        lf_t = _log_sigmoid(gates_t[SUBLANES:2 * SUBLANES])
        b_t = sum(_dot(part, upper) for part in _split_bf16(lf_t))
        c_t = ig_t - b_t
        c2_t = c_t * LOG2E
        c2_cols = jnp.transpose(jnp.concatenate([c2_t, jnp.zeros((LANES - SUBLANES, lc), F32)], axis=0))
        c2_masked = [jnp.where(causal, c2_cols[:, h:h + 1], NEG) for h in range(NH_M)]
        cmax2_t = jnp.concatenate(
            [jnp.max(cm, axis=0, keepdims=True) for cm in c2_masked]
            + [jnp.zeros((SUBLANES - NH_M, lc), F32)], axis=0)
        gate[ck, b] = dict(b_t=b_t, c_t=c_t, c2_masked=c2_masked, cmax2=cmax2_t,
                           m_intra=b_t + cmax2_t * (1.0 / LOG2E))

    early = {}
    for ck, b in units:
        for h in range(NH_M):
            early[ck, b, h] = dict(
                s_t=_dot_nt(k_ref[b, tok(ck), hsl(h)], q_ref[b, tok(ck), hsl(h)]),
                v_t=jnp.transpose(v_ref[b, tok(ck), hsl(h)].astype(F32)))

    for ck in range(nck):
        rows = []
        for b in range(nbm):
            g = gate[ck, b]
            b_t, c_t = g["b_t"], g["c_t"]
            b_end = b_t[:, lc - 1:lc]
            m_prev = m_ref[b][:, 0:1]
            log_g = b_end + c_t
            m_end = jnp.maximum(b_end + m_prev, jnp.max(log_g, axis=-1, keepdims=True))
            g_rows = jnp.exp(log_g - m_end)
            log_inter = b_t + m_prev
            m_tok = jnp.maximum(log_inter, g["m_intra"])
            rows.append(dict(decay=jnp.exp(b_end + m_prev - m_end), g=g_rows,
                             g_b=g_rows.astype(BF16), m_end=m_end,
                             r_intra=jnp.exp(g["m_intra"] - m_tok),
                             w_inter=jnp.exp(log_inter - m_tok), floor=jnp.exp(-m_tok),
                             n_prev_b=n_ref[b].astype(BF16)))

        for b, h in pairs:
            q = q_ref[b, tok(ck), hsl(h)]
            ct_prev = ct_get(b, h)
            early[ck, b, h].update(
                ct=ct_prev if single_chunk else None,
                cq_t=_dot_nt(ct_prev.astype(BF16), q),
                nq=_dot_nt(rows[b]["n_prev_b"], q)[h:h + 1, :])

        n_upd = [jnp.zeros((SUBLANES, DH_M), F32) for _ in range(nbm)]
        for b, h in pairs:
            r, g, e = rows[b], gate[ck, b], early[ck, b, h]
            k = k_ref[b, tok(ck), hsl(h)]
            a_t = jnp.exp2(g["c2_masked"][h] - g["cmax2"][h:h + 1, :]) * e["s_t"]
            r_intra, w_inter = r["r_intra"][h:h + 1, :], r["w_inter"][h:h + 1, :]
            num_t = _dot(e["v_t"].astype(BF16), a_t.astype(BF16)) * r_intra + e["cq_t"] * w_inter
            den = jnp.sum(a_t, axis=0, keepdims=True) * r_intra + e["nq"] * w_inter
            h_t = num_t * (1.0 / jnp.maximum(jnp.abs(den), r["floor"][h:h + 1, :]))
            hn_t = h_t * lax.rsqrt(jnp.mean(h_t * h_t, axis=0, keepdims=True) + EPS)
            hn = jnp.transpose(hn_t) * gm_ref[:, hsl(h)]
            hm_ref[b, tok(ck), hsl(h)] = (_sigmoid(mo_ref[b, tok(ck), hsl(h)]) * hn).astype(BF16)

            vg_t = (e["v_t"] * r["g"][h:h + 1, :]).astype(BF16)
            ct_prev = e["ct"] if single_chunk else ct_get(b, h)
            ct_put(b, h, r["decay"][h:h + 1, :] * ct_prev + _dot(vg_t, k))
            n_upd[b] = n_upd[b] + jnp.where(head_row == h, _dot(r["g_b"], k), 0.0)

        for b in range(nbm):
            n_ref[b] = rows[b]["decay"] * n_ref[b] + n_upd[b]
            m_ref[b] = jnp.broadcast_to(rows[b]["m_end"], (SUBLANES, LANES))

    if not single_chunk:
        @pl.when(j == pl.num_programs(1) - 1)
        def _():
            transpose_states(ct_ref, c_ref)


def _mlstm(q, k, v, gates, mo, c0, n0, m0, g_mlstm_row, *, nbm, lc):
    NB, L, _ = q.shape
    nck = MLSTM_CHUNKS_PER_STEP if L % (MLSTM_CHUNKS_PER_STEP * lc) == 0 else 1
    grid = (NB // nbm, L // (nck * lc))
    row = lambda w: pl.BlockSpec((nbm, nck * lc, w), lambda i, j: (i, j, 0))
    st = lambda s: pl.BlockSpec((nbm,) + s, lambda i, j: (i,) + (0,) * len(s))
    c_s, n_s = (NH_M, DH_M, DH_M), (SUBLANES, DH_M)
    return pl.pallas_call(
        functools.partial(_mlstm_kernel, nbm=nbm, lc=lc, nck=nck, single_chunk=L == lc),
        out_shape=[jax.ShapeDtypeStruct((NB, L, D_M), BF16),
                   jax.ShapeDtypeStruct((NB,) + c_s, F32),
                   jax.ShapeDtypeStruct((NB,) + n_s, F32),
                   jax.ShapeDtypeStruct((NB,) + n_s, F32)],
        grid=grid,
        in_specs=[row(D_M), row(D_M), row(D_M), row(LANES), row(D_M), st(c_s), st(n_s), st(n_s),
                  _const_spec((1, D_M))],
        out_specs=[row(D_M), st(c_s), st(n_s), st(n_s)],
        scratch_shapes=[pltpu.VMEM((nbm,) + c_s, F32)],
        compiler_params=_cparams(2),
        name="mlstm",
    )(q, k, v, gates, mo, c0, n0, m0, g_mlstm_row)


def _build_band_bias(ext_ref, tq, klen, put):
    p_len = ext_ref.shape[-1]
    q_pos = lax.broadcasted_iota(jnp.int32, (tq, klen), 0) + (klen - tq)
    k_pos = lax.broadcasted_iota(jnp.int32, (tq, klen), 1)
    back = q_pos // CHUNK - k_pos // CHUNK
    for h in range(NH_A):
        base = jnp.broadcast_to(ext_ref[h:h + 1, :] * LOG2E, (tq, p_len))
        toeplitz = pltpu.roll(base, 0, 1, stride=1, stride_axis=0)[:, :klen]
        put(h, jnp.where(back >= 0, jnp.where(back <= LEFT_CHUNKS, toeplitz, NEG), NEG))


def _lane_tiles(s):
    n = s.shape[-1]
    if n % LANES:
        return [s]
    return [s[:, i * LANES:(i + 1) * LANES] for i in range(n // LANES)]


def _row_reduce(tiles, op, lane_op):
    by_width = {}
    for t in tiles:
        by_width[t.shape[-1]] = t if t.shape[-1] not in by_width else op(by_width[t.shape[-1]], t)
    return functools.reduce(op, [lane_op(t, axis=-1, keepdims=True) for t in by_width.values()])


def _softmax_pv(ss, vs):
    mx = _row_reduce([t for s in ss for t in _lane_tiles(s)], jnp.maximum, jnp.max)
    es = [jnp.exp2(s - mx) for s in ss]
    l = _row_reduce([t for e in es for t in _lane_tiles(e)], jnp.add, jnp.sum)
    o = functools.reduce(jnp.add, [_dot(e.astype(BF16), v) for e, v in zip(es, vs)])
    return o / l


def _attn_pairs(jobs):
    first = lax.broadcasted_iota(jnp.int32, (1, LANES), 1) < DH_A
    zero = jnp.zeros((), BF16)

    def scores(q, segs, p):
        sl = slice(p * LANES, (p + 1) * LANES)
        qp = q(sl)
        q2 = jnp.concatenate([jnp.where(first, qp, zero), jnp.where(first, zero, qp)], axis=0)
        ss = []
        for k, _, bias, pen in segs:
            s = _dot_nt(q2, k(sl)) + bias(p)
            ss.append(s if pen is None else s + pen)
        return ss

    def finish(ss, segs, put, p):
        sl = slice(p * LANES, (p + 1) * LANES)
        r = ss[0].shape[0] // 2
        o2 = _softmax_pv(ss, [v(sl) for _, v, _, _ in segs])
        put(sl, jnp.where(first, o2[:r], o2[r:]))

    pending = []
    for q, segs, put in jobs:
        for p in range(NH_A // 2):
            pending.append((scores(q, segs, p), segs, put, p))
            if len(pending) > PAIR_ATT_LOOKAHEAD:
                finish(*pending.pop(0))
    while pending:
        finish(*pending.pop(0))


def _attn_prompt_kernel(q_ref, kp_ref, ko_ref, vp_ref, vo_ref, ext_ref, o_ref, bias_ref):
    tq = q_ref.shape[1]
    j = pl.program_id(1)

    @pl.when((pl.program_id(0) == 0) & (j == 0))
    def _():
        def put_bias(h, tile):
            bias_ref[h // 2, (h % 2) * ATT_SUB:(h % 2 + 1) * ATT_SUB, :] = tile
        _build_band_bias(ext_ref, ATT_SUB, tq + ATT_SUB, put_bias)

    def run(pen_prev):
        jobs = []
        for i in range(tq // ATT_SUB):
            lo = i * ATT_SUB
            n_prev = tq - lo
            bias = lambda a, b: (lambda p: bias_ref[p, :, a:b])
            rows = lambda ref, a, b: (lambda sl: ref[0, a:b, sl])

            def put(sl, val, lo=lo):
                o_ref[0, lo:lo + ATT_SUB, sl] = val.astype(o_ref.dtype)

            segs = [(rows(kp_ref, lo, tq), rows(vp_ref, lo, tq), bias(0, n_prev), pen_prev),
                    (rows(ko_ref, 0, lo + ATT_SUB), rows(vo_ref, 0, lo + ATT_SUB),
                     bias(n_prev, tq + ATT_SUB), None)]
            jobs.append((rows(q_ref, lo, lo + ATT_SUB), segs, put))
        _attn_pairs(jobs)

    @pl.when(j >= 1)
    def _():
        run(None)

    @pl.when(j < 1)
    def _():
        run(jnp.full((), NEG, F32))


def _attn_prompt(aq, ak, av, ext):
    NB, L, _ = aq.shape
    tq = ATT_WINDOW
    assert L % tq == 0
    grid = (NB, L // tq)
    row = pl.BlockSpec((1, tq, D_A), lambda b, j: (b, j, 0))
    prev = pl.BlockSpec((1, tq, D_A), lambda b, j: (b, jnp.maximum(j - 1, 0), 0))
    return pl.pallas_call(
        _attn_prompt_kernel,
        out_shape=jax.ShapeDtypeStruct((NB, L, D_A), BF16),
        grid=grid,
        in_specs=[row, prev, row, prev, row, _const_spec(ext.shape)],
        out_specs=row,
        scratch_shapes=[pltpu.VMEM((NH_A // 2, 2 * ATT_SUB, tq + ATT_SUB), F32)],
        compiler_params=_cparams(2),
        name="attn_prompt",
    )(aq, ak, ak, av, av, ext)


def _attn_sample_kernel(q_ref, k_ref, v_ref, ckt_ref, cvt_ref, ext_ref, o_ref, bias_ref, *, nb):
    tq = q_ref.shape[1]
    nc = ckt_ref.shape[-1]

    @pl.when(pl.program_id(0) == 0)
    def _():
        def put_bias(h, tile):
            bias_ref[h] = tile
        _build_band_bias(ext_ref, tq, nc + tq, put_bias)

    def scores(b, h):
        hs = slice(h * DH_A, (h + 1) * DH_A)
        qh = q_ref[b, :, hs]
        return [_dot(qh, ckt_ref[b, h].astype(BF16)) + bias_ref[h, :, 0:nc],
                _dot_nt(qh, k_ref[b, :, hs]) + bias_ref[h, :, nc:nc + tq]]

    def finish(b, h, ss):
        hs = slice(h * DH_A, (h + 1) * DH_A)
        mx = jnp.maximum(*[jnp.max(s, axis=-1, keepdims=True) for s in ss])
        e_c, e_o = [jnp.exp2(s - mx) for s in ss]
        l = jnp.sum(e_c, axis=-1, keepdims=True) + jnp.sum(e_o, axis=-1, keepdims=True)
        o = (_dot_nt(e_c.astype(BF16), cvt_ref[b, h].astype(BF16))
             + _dot(e_o.astype(BF16), v_ref[b, :, hs]))
        return o / l

    outs = {b: [] for b in range(nb)}
    pending = []

    def retire():
        b, h, ss = pending.pop(0)
        outs[b].append(finish(b, h, ss))
        if h == NH_A - 1:
            o_ref[b] = jnp.concatenate(outs[b], axis=-1).astype(o_ref.dtype)

    for b in range(nb):
        for h in range(NH_A):
            pending.append((b, h, scores(b, h)))
            if len(pending) > SAMPLE_ATT_LOOKAHEAD:
                retire()
    while pending:
        retire()


def _attn_sample(aq, ak, av, cache_kt, cache_vt, ext, *, nb):
    NB, L, _ = aq.shape
    nc = cache_kt.shape[-1]
    row = pl.BlockSpec((nb, L, D_A), lambda i: (i, 0, 0))
    crow = pl.BlockSpec((nb, NH_A, DH_A, nc), lambda i: (i, 0, 0, 0))
    return pl.pallas_call(
        functools.partial(_attn_sample_kernel, nb=nb),
        out_shape=jax.ShapeDtypeStruct((NB, L, D_A), BF16),
        grid=(NB // nb,),
        in_specs=[row, row, row, crow, crow, _const_spec(ext.shape)],
        out_specs=row,
        scratch_shapes=[pltpu.VMEM((NH_A, L, nc + L), F32)],
        compiler_params=_cparams(1),
        name="attn_sample",
    )(aq, ak, av, cache_kt, cache_vt, ext)


def _rel_ext(rel_table, tq, klen):
    p_len = -(-(klen + tq) // LANES) * LANES
    n_lo = tq - CHUNK
    n_hi = p_len - tq - MAX_REL
    rep = lambda col, n: jnp.broadcast_to(col, (NH_A, n))
    by_dist = jnp.concatenate([rep(rel_table[:, :1], n_lo), rel_table, rep(rel_table[:, -1:], n_hi)],
                              axis=1)
    rev = by_dist[:, ::-1]
    n = p_len - klen
    return jnp.concatenate([rev[:, n:], rev[:, :n]], axis=1).astype(F32)


def _gate_lanes(g):
    z = lambda n: jnp.zeros((g.shape[0], n), g.dtype)
    return jnp.concatenate([g[:, :NH_M], z(SUBLANES - NH_M), g[:, NH_M:], z(LANES - SUBLANES - NH_M)], axis=1)


def _layer(x, mod, conv_prev, state, cache, w, *, nb, tl, nbm, lc, final):
    NB, L, d = x.shape
    x1 = _ffn(x, mod, w["g0"], w["up1"], w["dn1"], w["g_final"], nb=nb, tl=tl, sub=0)
    (q, k, v, mo, gates, aq, ak, av, k_tail, v_tail, conv_new) = _proj(
        x1, mod, w["g1"], w["w_main"], w["w_gates"], w["b_gates"], w["conv_w"], w["conv_b"],
        conv_prev, nb=nb, tl=tl)
    c0, n0, m0 = state
    n0 = jnp.pad(n0, ((0, 0), (0, SUBLANES - NH_M), (0, 0)))
    m0 = jnp.broadcast_to(jnp.pad(m0, ((0, 0), (0, SUBLANES - NH_M)))[:, :, None], (NB, SUBLANES, LANES))
    hm, c_new, n_new, m_new = _mlstm(q, k, v, gates, mo, c0, n0, m0, w["g_mlstm"], nbm=nbm, lc=lc)
    if cache is None:
        att = _attn_prompt(aq, ak, av, w["ext_prompt"])
    else:
        att = _attn_sample(aq, ak, av, cache[0], cache[1], w["ext_sample"], nb=min(NB, 4))
    y = _ffn(x1, mod, w["g2"], w["up2"], w["dn2"], w["g_final"], nb=nb, tl=tl, sub=2,
             mix=(hm, att, w["w_out"]), final=final)
    n_keep = k_tail.shape[1]
    states = (c_new, n_new[:, :NH_M, :], m_new[:, :NH_M, 0], conv_new,
              k_tail.reshape(NB, n_keep, NH_A, DH_A), v_tail.reshape(NB, n_keep, NH_A, DH_A))
    return y, states


def kernel(x_prompt, x_sample, state_mlstm_C, state_mlstm_n, state_mlstm_m, state_conv, cache_att_k, cache_att_v, c_prompt, c_sample, w_ada, b_ada, g_norm, w_ffn1_up, w_ffn1_down, w_ffn2_up, w_ffn2_down, w_in, conv_w, conv_b, b_gates, g_mlstm, rel_bias_table, w_out, g_final):
    depth = w_ada.shape[0]
    bp, seq, d = x_prompt.shape
    bs, dseq, _ = x_sample.shape
    xp, xs = x_prompt, x_sample
    st_p, st_s = [], []
    n_c = cache_att_k.shape[2]
    tl_p = min(ATT_WINDOW, seq)
    lc_p = min(MLSTM_CHUNK, seq)
    nb_s = max(1, min(bs, ATT_WINDOW // dseq))
    for l in range(depth):
        rows = bp + bs
        pad = (-rows) % SUBLANES
        c_all = jnp.concatenate([c_prompt, c_sample, jnp.zeros((pad, d), F32)], axis=0)
        mod = _adaln(c_all, w_ada[l], b_ada[l])
        mod_p = mod[:bp].reshape(bp, N_MOD, d)
        mod_s = mod[bp:rows].reshape(bs, N_MOD, d)
        off_g = 4 * D_M
        off_a = off_g + 2 * NH_M
        wl = w_in[l]
        rel = rel_bias_table[l]
        w = {
            "g0": g_norm[l, 0:1], "g1": g_norm[l, 1:2], "g2": g_norm[l, 2:3],
            "g_final": g_final.reshape(1, d),
            "up1": w_ffn1_up[l].astype(BF16), "dn1": w_ffn1_down[l].astype(BF16),
            "up2": w_ffn2_up[l].astype(BF16), "dn2": w_ffn2_down[l].astype(BF16),
            "w_main": jnp.concatenate([wl[:, :off_g], wl[:, off_a:]], axis=1).astype(BF16),
            "w_gates": _gate_lanes(wl[:, off_g:off_a]).astype(BF16),
            "b_gates": _gate_lanes(b_gates[l].reshape(1, 2 * NH_M)),
            "conv_w": conv_w[l], "conv_b": conv_b[l].reshape(1, 2 * D_M),
            "g_mlstm": g_mlstm[l].reshape(1, D_M),
            "w_out": w_out[l].astype(BF16),
            "ext_prompt": _rel_ext(rel, ATT_SUB, ATT_WINDOW + ATT_SUB),
            "ext_sample": _rel_ext(rel, dseq, n_c + dseq),
        }
        zero_state = (jnp.zeros((bp, NH_M, DH_M, DH_M), F32), jnp.zeros((bp, NH_M, DH_M), F32),
                      jnp.zeros((bp, NH_M), F32))
        xp, sp = _layer(xp, mod_p, jnp.zeros((bp, CONV_W - 1, 2 * D_M), F32), zero_state, None, w,
                        nb=1, tl=tl_p, nbm=bp, lc=lc_p, final=l == depth - 1)
        state_s = (state_mlstm_C[l], state_mlstm_n[l], state_mlstm_m[l])
        cache = (jnp.transpose(cache_att_k[l], (0, 2, 3, 1)), jnp.transpose(cache_att_v[l], (0, 2, 3, 1)))
        xs, ss = _layer(xs, mod_s, state_conv[l], state_s, cache, w, nb=nb_s, tl=dseq,
                        nbm=min(bs, 4), lc=dseq, final=l == depth - 1)
        st_p.append(sp)
        st_s.append(ss)
    stk = lambda sts, i: jnp.stack([s[i] for s in sts])
    return ((xp, xs) + tuple(stk(st_p, i) for i in range(6)) + tuple(stk(st_s, i) for i in range(6)))
```
